```python
import jax, jax.numpy as jnp
from jax import lax
import numpy as np

D_MODEL = 1024
BATCH = 16
SEQ = 4096
DEPTH = 4

GRID_W = 64
NA_HEADS = 8
NA_HEAD_DIM = 64
NA_WIDTH = NA_HEADS * NA_HEAD_DIM
NA_WIN_ROWS = 8
NA_WIN_COLS = 16
MLA_HEADS = 8
MLA_NOPE = 64
MLA_ROPE = 32
MLA_QK_DIM = MLA_NOPE + MLA_ROPE
MLA_V = 64
MLA_Q_LORA = 256
MLA_KV_LORA = 128
MLA_WIDTH = MLA_HEADS * MLA_V
ROPE_BASE = 10000.0
Q_BLOCK = 128
EPS = 1e-6
IN_SPLIT_SIZES = (NA_WIDTH, NA_WIDTH, NA_WIDTH, NA_WIDTH,
                  MLA_Q_LORA, MLA_KV_LORA, MLA_ROPE, MLA_WIDTH,
                  D_MODEL, D_MODEL)
D_IN = sum(IN_SPLIT_SIZES)

kernel_name = "hybrid_natten_mla_gated_encoder"


def rmsnorm(x, g):
    xf = x.astype(jnp.float32)
    y = xf * lax.rsqrt(jnp.mean(xf * xf, axis=-1, keepdims=True) + EPS)
    return (y * g.astype(jnp.float32)).astype(x.dtype)


def split_columns(p):
    parts, off = [], 0
    for size in IN_SPLIT_SIZES:
        parts.append(p[..., off:off + size])
        off += size
    return parts


def axial_rope_tables(seq_len):
    t = jnp.arange(seq_len)
    row = (t // GRID_W).astype(jnp.float32)
    col = (t % GRID_W).astype(jnp.float32)
    half = MLA_ROPE // 2
    n_freq = half // 2
    inv = jnp.power(jnp.float32(ROPE_BASE), -jnp.arange(n_freq, dtype=jnp.float32) / n_freq)
    ang = jnp.concatenate([row[:, None] * inv, col[:, None] * inv], axis=-1)
    return jnp.cos(ang), jnp.sin(ang)


def apply_rope(x, cos, sin):
    half = MLA_ROPE // 2
    x1, x2 = x[..., :half], x[..., half:]
    c, s = cos[None, :, None, :], sin[None, :, None, :]
    out = jnp.concatenate([x1 * c - x2 * s, x1 * s + x2 * c], axis=-1)
    return out.astype(x.dtype)


def neighborhood_attention(q, k, v, rel_bias):
    B, S, H, d = q.shape
    rows = S // GRID_W
    kr = min(NA_WIN_ROWS, rows)
    kw = NA_WIN_COLS
    kg = k.reshape(B, rows, GRID_W, H, d)
    vg = v.reshape(B, rows, GRID_W, H, d)
    qg = q.reshape(B, rows, GRID_W, H, d).transpose(1, 0, 2, 3, 4)
    cols = jnp.arange(GRID_W)
    col_start = jnp.clip(cols - kw // 2, 0, GRID_W - kw)
    col_idx = col_start[:, None] + jnp.arange(kw)[None, :]
    dcol = col_idx - cols[:, None] + (kw - 1)
    row_ids = jnp.arange(rows)
    row_start = jnp.clip(row_ids - kr // 2, 0, rows - kr)
    scale = d ** -0.5

    def one_row(args):
        q_row, r, rs = args
        k_band = lax.dynamic_slice_in_dim(kg, rs, kr, axis=1)
        v_band = lax.dynamic_slice_in_dim(vg, rs, kr, axis=1)
        k_win = k_band[:, :, col_idx]
        v_win = v_band[:, :, col_idx]
        s = jnp.einsum('bqhd,bnqjhd->bhqnj', q_row, k_win).astype(jnp.float32) * scale
        drow = rs + jnp.arange(kr) - r + (NA_WIN_ROWS - 1)
        bias = rel_bias[:, drow[None, :, None], dcol[:, None, :]]
        s = s + bias[None].astype(jnp.float32)
        p = jax.nn.softmax(s.reshape(B, H, GRID_W, kr * kw), axis=-1)
        p = p.reshape(B, H, GRID_W, kr, kw).astype(v.dtype)
        return jnp.einsum('bhqnj,bnqjhd->bqhd', p, v_win)

    o = lax.map(one_row, (qg, row_ids, row_start))
    return o.transpose(1, 0, 2, 3, 4).reshape(B, S, H * d)


def dense_block_attention(q, k, v):
    B, S, H, dk = q.shape
    dv = v.shape[-1]
    nblk = S // Q_BLOCK
    scale = dk ** -0.5
    qb = q.reshape(B, nblk, Q_BLOCK, H, dk).transpose(1, 0, 2, 3, 4)

    def one_block(q_blk):
        s = jnp.einsum('bqhd,bkhd->bhqk', q_blk, k).astype(jnp.float32) * scale
        p = jax.nn.softmax(s, axis=-1).astype(v.dtype)
        return jnp.einsum('bhqk,bkhd->bqhd', p, v)

    o = lax.map(one_block, qb)
    return o.transpose(1, 0, 2, 3, 4).reshape(B, S, H * dv)


def hybrid_layer(x, ln_g, w_in, na_q_norm, na_k_norm, na_rel_bias,
                 mla_cq_norm, mla_ckv_norm, w_uq, w_ukv, mla_q_norm, mla_k_norm,
                 w_o_na, w_o_mla, w_out, rope_cos, rope_sin):
    B, S, _ = x.shape
    h = rmsnorm(x, ln_g)
    proj = h @ w_in
    (na_q, na_k, na_v, na_gate, c_q, c_kv, k_pe, mla_gate,
     g_na, g_mla) = split_columns(proj)

    qa = rmsnorm(na_q.reshape(B, S, NA_HEADS, NA_HEAD_DIM), na_q_norm)
    ka = rmsnorm(na_k.reshape(B, S, NA_HEADS, NA_HEAD_DIM), na_k_norm)
    va = na_v.reshape(B, S, NA_HEADS, NA_HEAD_DIM)
    o_na = neighborhood_attention(qa, ka, va, na_rel_bias) * jax.nn.silu(na_gate)
    u_na = o_na @ w_o_na

    qb = (rmsnorm(c_q, mla_cq_norm) @ w_uq).reshape(B, S, MLA_HEADS, MLA_QK_DIM)
    kv = (rmsnorm(c_kv, mla_ckv_norm) @ w_ukv).reshape(B, S, MLA_HEADS, MLA_NOPE + MLA_V)
    k_nope, vb = kv[..., :MLA_NOPE], kv[..., MLA_NOPE:]
    k_rot = jnp.broadcast_to(k_pe[:, :, None, :], (B, S, MLA_HEADS, MLA_ROPE))
    kb = jnp.concatenate([k_nope, k_rot], axis=-1)
    qb = rmsnorm(qb, mla_q_norm)
    kb = rmsnorm(kb, mla_k_norm)
    qb = jnp.concatenate([qb[..., :MLA_NOPE], apply_rope(qb[..., MLA_NOPE:], rope_cos, rope_sin)], axis=-1)
    kb = jnp.concatenate([kb[..., :MLA_NOPE], apply_rope(kb[..., MLA_NOPE:], rope_cos, rope_sin)], axis=-1)
    o_mla = dense_block_attention(qb, kb, vb) * jax.nn.silu(mla_gate)
    u_mla = o_mla @ w_o_mla

    y = jax.nn.sigmoid(g_na) * u_na + jax.nn.sigmoid(g_mla) * u_mla
    return x + y @ w_out


def setup_inputs(seed: int = 0) -> dict:
    key = jax.random.key(seed)
    ks = jax.random.split(key, 16)
    f32 = jnp.float32

    def nrm(k, shape, scale):
        return jax.random.normal(k, shape, f32) * scale

    def gain(k, shape):
        return 1.0 + 0.02 * jax.random.normal(k, shape, f32)

    L = DEPTH
    return {
        "x": jax.random.normal(ks[0], (BATCH, SEQ, D_MODEL), f32),
        "ln_g": gain(ks[1], (L, D_MODEL)),
        "w_in": nrm(ks[2], (L, D_MODEL, D_IN), D_MODEL ** -0.5),
        "na_q_norm": gain(ks[3], (L, NA_HEAD_DIM)),
        "na_k_norm": gain(ks[4], (L, NA_HEAD_DIM)),
        "na_rel_bias": nrm(ks[5], (L, NA_HEADS, 2 * NA_WIN_ROWS - 1, 2 * NA_WIN_COLS - 1), 0.5),
        "mla_cq_norm": gain(ks[6], (L, MLA_Q_LORA)),
        "mla_ckv_norm": gain(ks[7], (L, MLA_KV_LORA)),
        "w_uq": nrm(ks[8], (L, MLA_Q_LORA, MLA_HEADS * MLA_QK_DIM), MLA_Q_LORA ** -0.5),
        "w_ukv": nrm(ks[9], (L, MLA_KV_LORA, MLA_HEADS * (MLA_NOPE + MLA_V)), MLA_KV_LORA ** -0.5),
        "mla_q_norm": gain(ks[10], (L, MLA_QK_DIM)),
        "mla_k_norm": gain(ks[11], (L, MLA_QK_DIM)),
        "w_o_na": nrm(ks[12], (L, NA_WIDTH, D_MODEL), NA_WIDTH ** -0.5),
        "w_o_mla": nrm(ks[13], (L, MLA_WIDTH, D_MODEL), MLA_WIDTH ** -0.5),
        "w_out": nrm(ks[14], (L, D_MODEL, D_MODEL), D_MODEL ** -0.5),
    }


def reference(x, ln_g, w_in, na_q_norm, na_k_norm, na_rel_bias, mla_cq_norm, mla_ckv_norm,
              w_uq, w_ukv, mla_q_norm, mla_k_norm, w_o_na, w_o_mla, w_out):
    rope_cos, rope_sin = axial_rope_tables(x.shape[1])
    for l in range(DEPTH):
        x = hybrid_layer(x, ln_g[l], w_in[l], na_q_norm[l], na_k_norm[l], na_rel_bias[l],
                         mla_cq_norm[l], mla_ckv_norm[l], w_uq[l], w_ukv[l],
                         mla_q_norm[l], mla_k_norm[l], w_o_na[l], w_o_mla[l], w_out[l],
                         rope_cos, rope_sin)
    return x
```

```python
import functools

import numpy as np
import jax
import jax.numpy as jnp
from jax import lax
from jax.experimental import pallas as pl
from jax.experimental.pallas import tpu as pltpu

D_MODEL = 1024
GRID_W = 64
NA_HEADS = 8
NA_HEAD_DIM = 64
NA_WIDTH = NA_HEADS * NA_HEAD_DIM
NA_WIN_ROWS = 8
NA_WIN_COLS = 16
MLA_HEADS = 8
MLA_NOPE = 64
MLA_ROPE = 32
MLA_QK_DIM = MLA_NOPE + MLA_ROPE
MLA_V = 64
MLA_Q_LORA = 256
MLA_KV_LORA = 128
MLA_WIDTH = MLA_HEADS * MLA_V
ROPE_BASE = 10000.0
EPS = 1e-6

LANES = 128
HEAD_PAD = LANES
MLA_PAD_WIDTH = MLA_HEADS * HEAD_PAD
VMEM_LIMIT_BYTES = 56 * 1024 * 1024

_OFF_NAQ = 0
_OFF_NAK = _OFF_NAQ + NA_WIDTH
_OFF_NAV = _OFF_NAK + NA_WIDTH
_OFF_NAG = _OFF_NAV + NA_WIDTH
_OFF_CQ = _OFF_NAG + NA_WIDTH
_OFF_CKV = _OFF_CQ + MLA_Q_LORA
_OFF_MG = _OFF_CKV + 2 * LANES
_OFF_GNA = _OFF_MG + MLA_WIDTH
_OFF_GMLA = _OFF_GNA + D_MODEL
_W_IN_COLS = _OFF_GMLA + D_MODEL

PROJ_TILE = 512
NA_QROWS = 4
NA_BAND = 12
NA_TQ = NA_QROWS * GRID_W
NA_TK = NA_BAND * GRID_W
MLA_TQ = 256
MASK_VALUE = -1e30

_NT_DIMS = (((1,), (1,)), ((), ()))


def _dot(a, b):
    return jnp.dot(a, b, preferred_element_type=jnp.float32)


def _dot_nt(a, b):
    return lax.dot_general(a, b, _NT_DIMS, preferred_element_type=jnp.float32)


def _lane_iota():
    return lax.broadcasted_iota(jnp.int32, (1, LANES), 1)


def _proj_kernel(x_ref, lng_ref, w_ref, gq_ref, gk_ref, gcq_ref, gckv_ref, wq_ref, wqs_ref,
                 wk_ref, wv_ref, gkf_ref, tq1_ref, tq2_ref, e1_ref, e2_ref,
                 qa_ref, ka_ref, va_ref, nag_ref, qf_ref, kf_ref, vb_ref, mg_ref,
                 gna_ref, gmla_ref):
    bf16 = jnp.bfloat16
    x = x_ref[...]
    h = x * lax.rsqrt(jnp.mean(x * x, axis=-1, keepdims=True) + EPS) * lng_ref[...]
    hb = h.astype(bf16)
    lo_half = _lane_iota() < NA_HEAD_DIM

    def piece(off, width):
        return _dot(hb, w_ref[:, off:off + width])

    def headnorm64(p, g2, out_ref):
        for c in range(NA_WIDTH // LANES):
            pc = p[:, c * LANES:(c + 1) * LANES]
            sq = pc * pc
            s_lo = jnp.sum(jnp.where(lo_half, sq, 0.0), axis=-1, keepdims=True)
            s_hi = jnp.sum(jnp.where(lo_half, 0.0, sq), axis=-1, keepdims=True)
            r = jnp.where(lo_half,
                          lax.rsqrt(s_lo * (1.0 / NA_HEAD_DIM) + EPS),
                          lax.rsqrt(s_hi * (1.0 / NA_HEAD_DIM) + EPS))
            out_ref[:, c * LANES:(c + 1) * LANES] = (pc * r * g2).astype(bf16)

    headnorm64(piece(_OFF_NAQ, NA_WIDTH), gq_ref[...], qa_ref)
    headnorm64(piece(_OFF_NAK, NA_WIDTH), gk_ref[...], ka_ref)
    va_ref[...] = piece(_OFF_NAV, NA_WIDTH).astype(bf16)
    g = piece(_OFF_NAG, NA_WIDTH)
    nag_ref[...] = (g * jax.nn.sigmoid(g)).astype(bf16)

    cq = piece(_OFF_CQ, MLA_Q_LORA)
    cqn = (cq * lax.rsqrt(jnp.mean(cq * cq, axis=-1, keepdims=True) + EPS) * gcq_ref[...]).astype(bf16)
    qfull = _dot(cqn, wq_ref[...])
    qswap = _dot(cqn, wqs_ref[...])
    tq1 = tq1_ref[...]
    tq2 = tq2_ref[...]
    for hh in range(MLA_HEADS):
        a = qfull[:, hh * HEAD_PAD:(hh + 1) * HEAD_PAD]
        b = qswap[:, hh * HEAD_PAD:(hh + 1) * HEAD_PAD]
        ss = jnp.sum(a * a, axis=-1, keepdims=True)
        r = lax.rsqrt(ss / MLA_QK_DIM + EPS)
        qf_ref[:, hh * HEAD_PAD:(hh + 1) * HEAD_PAD] = ((a * tq1 + b * tq2) * r).astype(bf16)

    ckvx = piece(_OFF_CKV, 2 * LANES)
    ckv = ckvx[:, :LANES]
    e = ckvx[:, LANES:]
    ckvn = (ckv * lax.rsqrt(jnp.mean(ckv * ckv, axis=-1, keepdims=True) + EPS) * gckv_ref[...]).astype(bf16)
    rope_lanes = _lane_iota() < MLA_ROPE
    ss_pe = jnp.sum(jnp.where(rope_lanes, e * e, 0.0), axis=-1, keepdims=True)
    rot = e * e1_ref[...] + pltpu.roll(e, LANES - MLA_ROPE, axis=1) * e2_ref[...]
    rot_hi = rot.astype(bf16).astype(jnp.float32)
    rot_lo = (rot - rot_hi).astype(bf16).astype(jnp.float32)
    lhs2 = (rot_hi + pltpu.roll(rot_lo, MLA_ROPE, axis=1)).astype(bf16)
    kfull = _dot(jnp.concatenate([ckvn, lhs2], axis=-1), wk_ref[...])
    gkf = gkf_ref[...]
    for hh in range(MLA_HEADS):
        a = kfull[:, hh * HEAD_PAD:(hh + 1) * HEAD_PAD]
        ss = jnp.sum(jnp.where(lo_half, a * a, 0.0), axis=-1, keepdims=True) + ss_pe
        r = lax.rsqrt(ss / MLA_QK_DIM + EPS)
        kf_ref[:, hh * HEAD_PAD:(hh + 1) * HEAD_PAD] = (a * gkf * r).astype(bf16)
    vb_ref[...] = _dot(ckvn, wv_ref[...]).astype(bf16)

    g = piece(_OFF_MG, MLA_WIDTH)
    mg_ref[...] = (g * jax.nn.sigmoid(g)).astype(bf16)
    gna_ref[...] = jax.nn.sigmoid(piece(_OFF_GNA, D_MODEL)).astype(bf16)
    gmla_ref[...] = jax.nn.sigmoid(piece(_OFF_GMLA, D_MODEL)).astype(bf16)


def _const_spec(shape):
    nd = len(shape)
    return pl.BlockSpec(shape, lambda i: (0,) * nd, pipeline_mode=pl.Buffered(1))


def _proj_call(x2d, lp, tabs, seq_len):
    n_tok = x2d.shape[0]
    tm = PROJ_TILE
    assert n_tok % tm == 0 and seq_len % tm == 0
    tiles_per_seq = seq_len // tm
    bf16 = jnp.bfloat16

    def row_spec(width):
        return pl.BlockSpec((tm, width), lambda i: (i, 0))

    def tab_spec():
        return pl.BlockSpec((tm, LANES), lambda i: (i % tiles_per_seq, 0))

    consts = [lp["ln_g"], lp["w_in"], lp["gq2"], lp["gk2"], lp["gcq"], lp["gckv"],
              lp["wq"], lp["wqs"], lp["wk"], lp["wv"], lp["gkf"]]
    out_widths = [NA_WIDTH, NA_WIDTH, NA_WIDTH, NA_WIDTH, MLA_PAD_WIDTH, MLA_PAD_WIDTH,
                  MLA_WIDTH, MLA_WIDTH, D_MODEL, D_MODEL]
    return pl.pallas_call(
        _proj_kernel,
        grid=(n_tok // tm,),
        in_specs=[row_spec(D_MODEL)] + [_const_spec(c.shape) for c in consts] + [tab_spec()] * 4,
        out_specs=[row_spec(w) for w in out_widths],
        out_shape=[jax.ShapeDtypeStruct((n_tok, w), bf16) for w in out_widths],
        compiler_params=pltpu.CompilerParams(
            dimension_semantics=("arbitrary",), vmem_limit_bytes=VMEM_LIMIT_BYTES),
        name="proj",
    )(x2d, *consts, tabs["tq1"], tabs["tq2"], tabs["e1"], tabs["e2"])


def _na_band_start(i, rows):
    return jnp.clip(NA_QROWS * i - NA_WIN_ROWS // 2, 0, rows - NA_BAND)


def _natt_kernel(q_ref, k_ref, v_ref, gate_ref, tbl_ref, o_ref):
    bf16 = jnp.bfloat16
    i = pl.program_id(1)
    start = pl.multiple_of(_na_band_start(i, k_ref.shape[1] // GRID_W) * GRID_W, GRID_W)
    lo_half = _lane_iota() < NA_HEAD_DIM
    for c in range(NA_WIDTH // LANES):
        cols = slice(c * LANES, (c + 1) * LANES)
        qp = q_ref[0, :, cols]
        kp = k_ref[0, pl.ds(start, NA_TK), cols]
        vp = v_ref[0, pl.ds(start, NA_TK), cols]
        halves = []
        for half in range(2):
            sel = lo_half if half == 0 else jnp.logical_not(lo_half)
            qm = jnp.where(sel, qp, jnp.zeros_like(qp))
            s = _dot_nt(qm, kp) + tbl_ref[0, 2 * c + half]
            m = jnp.max(s, axis=-1, keepdims=True)
            p = jnp.exp(s - m)
            l = jnp.sum(p, axis=-1, keepdims=True)
            halves.append(_dot(p.astype(bf16), vp) / l)
        o = jnp.where(lo_half, halves[0], halves[1])
        o_ref[0, :, cols] = (o * gate_ref[0, :, cols].astype(jnp.float32)).astype(bf16)


def _natt_call(qa, ka, va, nag, tbl):
    bsz, seq_len, _ = qa.shape
    n_blk = seq_len // NA_TQ

    def pattern(i):
        return jnp.minimum(i, 1) + (i == n_blk - 1).astype(jnp.int32)

    blk = pl.BlockSpec((1, NA_TQ, NA_WIDTH), lambda b, i: (b, i, 0))
    full = pl.BlockSpec((1, seq_len, NA_WIDTH), lambda b, i: (b, 0, 0))
    return pl.pallas_call(
        _natt_kernel,
        grid=(bsz, n_blk),
        in_specs=[blk, full, full, blk,
                  pl.BlockSpec((1, NA_HEADS, NA_TQ, NA_TK), lambda b, i: (pattern(i), 0, 0, 0))],
        out_specs=blk,
        out_shape=jax.ShapeDtypeStruct((bsz, seq_len, NA_WIDTH), jnp.bfloat16),
        compiler_params=pltpu.CompilerParams(
            dimension_semantics=("arbitrary", "arbitrary"), vmem_limit_bytes=VMEM_LIMIT_BYTES),
        name="natt",
    )(qa, ka, va, nag, tbl)


def _mla_kernel(q_ref, k_ref, v_ref, gate_ref, o_ref):
    bf16 = jnp.bfloat16
    lo_half = _lane_iota() < MLA_V
    v = v_ref[0]
    halves = []
    for half in range(2):
        cols = slice(half * HEAD_PAD, (half + 1) * HEAD_PAD)
        s = _dot_nt(q_ref[0, :, cols], k_ref[0, :, cols])
        m = jnp.max(s, axis=-1, keepdims=True)
        p = jnp.exp(s - m)
        l = jnp.sum(p, axis=-1, keepdims=True)
        halves.append(_dot(p.astype(bf16), v) / l)
    o = jnp.where(lo_half, halves[0], halves[1])
    o_ref[0] = (o * gate_ref[0].astype(jnp.float32)).astype(bf16)


def _mla_call(qf, kf, vb, mg):
    bsz, seq_len, _ = qf.shape
    n_pairs = MLA_HEADS // 2
    return pl.pallas_call(
        _mla_kernel,
        grid=(bsz, n_pairs, seq_len // MLA_TQ),
        in_specs=[pl.BlockSpec((1, MLA_TQ, 2 * HEAD_PAD), lambda b, p, j: (b, j, p)),
                  pl.BlockSpec((1, seq_len, 2 * HEAD_PAD), lambda b, p, j: (b, 0, p)),
                  pl.BlockSpec((1, seq_len, 2 * MLA_V), lambda b, p, j: (b, 0, p)),
                  pl.BlockSpec((1, MLA_TQ, 2 * MLA_V), lambda b, p, j: (b, j, p))],
        out_specs=pl.BlockSpec((1, MLA_TQ, 2 * MLA_V), lambda b, p, j: (b, j, p)),
        out_shape=jax.ShapeDtypeStruct((bsz, seq_len, MLA_WIDTH), jnp.bfloat16),
        compiler_params=pltpu.CompilerParams(
            dimension_semantics=("arbitrary", "arbitrary", "arbitrary"),
            vmem_limit_bytes=VMEM_LIMIT_BYTES),
        name="mla",
    )(qf, kf, vb, mg)


def _out_kernel(x_ref, ana_ref, amla_ref, gna_ref, gmla_ref, wna_ref, wmla_ref, wout_ref, o_ref):
    u_na = _dot(ana_ref[...], wna_ref[...])
    u_mla = _dot(amla_ref[...], wmla_ref[...])
    y = gna_ref[...].astype(jnp.float32) * u_na + gmla_ref[...].astype(jnp.float32) * u_mla
    o_ref[...] = x_ref[...] + _dot(y.astype(jnp.bfloat16), wout_ref[...])


def _out_call(x2d, a_na, a_mla, gna, gmla, lp):
    n_tok = x2d.shape[0]
    tm = PROJ_TILE

    def row_spec(width):
        return pl.BlockSpec((tm, width), lambda i: (i, 0))

    consts = [lp["w_o_na"], lp["w_o_mla"], lp["w_out"]]
    return pl.pallas_call(
        _out_kernel,
        grid=(n_tok // tm,),
        in_specs=[row_spec(D_MODEL), row_spec(NA_WIDTH), row_spec(MLA_WIDTH), row_spec(D_MODEL),
                  row_spec(D_MODEL)] + [_const_spec(c.shape) for c in consts],
        out_specs=row_spec(D_MODEL),
        out_shape=jax.ShapeDtypeStruct((n_tok, D_MODEL), jnp.float32),
        compiler_params=pltpu.CompilerParams(
            dimension_semantics=("arbitrary",), vmem_limit_bytes=VMEM_LIMIT_BYTES),
        name="out",
    )(x2d, a_na, a_mla, gna, gmla, *consts)


def _na_table_indices(seq_len):
    rows = seq_len // GRID_W
    n_blk = rows // NA_QROWS
    qr = np.arange(NA_TQ) // GRID_W
    qc = np.arange(NA_TQ) % GRID_W
    kn = np.arange(NA_TK) // GRID_W
    kj = np.arange(NA_TK) % GRID_W
    drow_all, dcol_all, valid_all = [], [], []
    for i in (0, 1, n_blk - 1):
        band = int(np.clip(NA_QROWS * i - NA_WIN_ROWS // 2, 0, rows - NA_BAND))
        r = NA_QROWS * i + qr
        rs = np.clip(r - NA_WIN_ROWS // 2, 0, rows - NA_WIN_ROWS)
        cs = np.clip(qc - NA_WIN_COLS // 2, 0, GRID_W - NA_WIN_COLS)
        krow = band + kn
        valid = ((krow[None, :] >= rs[:, None]) & (krow[None, :] < rs[:, None] + NA_WIN_ROWS)
                 & (kj[None, :] >= cs[:, None]) & (kj[None, :] < cs[:, None] + NA_WIN_COLS))
        drow = krow[None, :] - r[:, None] + (NA_WIN_ROWS - 1)
        dcol = kj[None, :] - qc[:, None] + (NA_WIN_COLS - 1)
        drow_all.append(np.where(valid, drow, 0))
        dcol_all.append(np.where(valid, dcol, 0))
        valid_all.append(valid)
    return (np.stack(drow_all).astype(np.int32), np.stack(dcol_all).astype(np.int32),
            np.stack(valid_all))


def _rope_tables(seq_len):
    t = jnp.arange(seq_len)
    row = (t // GRID_W).astype(jnp.float32)
    col = (t % GRID_W).astype(jnp.float32)
    half = MLA_ROPE // 2
    n_freq = half // 2
    inv = jnp.power(jnp.float32(ROPE_BASE), -jnp.arange(n_freq, dtype=jnp.float32) / n_freq)
    ang = jnp.concatenate([row[:, None] * inv, col[:, None] * inv], axis=-1)
    return jnp.cos(ang), jnp.sin(ang)


def _place(cols_to_blocks, total):
    parts = []
    for arr, width in cols_to_blocks:
        pad = width - arr.shape[-1]
        parts.append(jnp.pad(arr, [(0, 0)] * (arr.ndim - 1) + [(0, pad)]) if pad else arr)
    out = jnp.concatenate(parts, axis=-1)
    assert out.shape[-1] == total
    return out


def _layer_params(l, p, cos, sin):
    f32, bf16 = jnp.float32, jnp.bfloat16
    half = MLA_ROPE // 2
    w_in = p["w_in"][l]
    o = 0
    pieces = {}
    for name, size in (("naq", NA_WIDTH), ("nak", NA_WIDTH), ("nav", NA_WIDTH), ("nag", NA_WIDTH),
                       ("cq", MLA_Q_LORA), ("ckv", MLA_KV_LORA), ("kpe", MLA_ROPE),
                       ("mg", MLA_WIDTH), ("gna", D_MODEL), ("gmla", D_MODEL)):
        pieces[name] = w_in[:, o:o + size]
        o += size
    kpe = pieces["kpe"]
    kpe_sw = jnp.concatenate([kpe[:, half:], kpe[:, :half]], axis=-1)
    ckvx = _place([(pieces["ckv"], LANES), (jnp.concatenate([kpe, kpe_sw], axis=-1), LANES)], 2 * LANES)
    w_in_packed = jnp.concatenate(
        [pieces["naq"], pieces["nak"], pieces["nav"], pieces["nag"], pieces["cq"], ckvx,
         pieces["mg"], pieces["gna"], pieces["gmla"]], axis=-1).astype(bf16)
    assert w_in_packed.shape[-1] == _W_IN_COLS

    w_uq = p["w_uq"][l].reshape(MLA_Q_LORA, MLA_HEADS, MLA_QK_DIM)
    nope, x1, x2 = w_uq[..., :MLA_NOPE], w_uq[..., MLA_NOPE:MLA_NOPE + half], w_uq[..., MLA_NOPE + half:]
    zq = jnp.zeros_like(nope)
    wq = _place([(jnp.concatenate([nope, x1, x2], -1), HEAD_PAD)], HEAD_PAD)
    wqs = _place([(jnp.concatenate([zq, x2, x1], -1), HEAD_PAD)], HEAD_PAD)
    wq = wq.reshape(MLA_Q_LORA, MLA_PAD_WIDTH).astype(bf16)
    wqs = wqs.reshape(MLA_Q_LORA, MLA_PAD_WIDTH).astype(bf16)

    w_ukv = p["w_ukv"][l].reshape(MLA_KV_LORA, MLA_HEADS, MLA_NOPE + MLA_V)
    k_nope = _place([(w_ukv[..., :MLA_NOPE], HEAD_PAD)], HEAD_PAD).reshape(MLA_KV_LORA, MLA_PAD_WIDTH)
    eye = jnp.eye(MLA_ROPE, dtype=f32)
    put = jnp.pad(eye, ((0, 0), (MLA_NOPE, HEAD_PAD - MLA_QK_DIM)))
    put = jnp.tile(put, (1, MLA_HEADS))
    wk = jnp.concatenate([k_nope, put, put, jnp.zeros((2 * LANES - MLA_KV_LORA - 2 * MLA_ROPE,
                                                        MLA_PAD_WIDTH), f32)], axis=0).astype(bf16)
    wv = w_ukv[..., MLA_NOPE:].reshape(MLA_KV_LORA, MLA_WIDTH).astype(bf16)

    gq = p["mla_q_norm"][l]
    gk = p["mla_k_norm"][l]
    scale = MLA_QK_DIM ** -0.5
    seq_len = cos.shape[0]
    ones = jnp.ones((seq_len, 1), f32)
    tq1 = _place([(ones * gq[None, :MLA_NOPE], MLA_NOPE),
                  (cos * gq[None, MLA_NOPE:MLA_NOPE + half], half),
                  (cos * gq[None, MLA_NOPE + half:], half)], MLA_QK_DIM) * scale
    tq2 = _place([(jnp.zeros((seq_len, MLA_NOPE), f32), MLA_NOPE),
                  (-sin * gq[None, MLA_NOPE + half:], half),
                  (sin * gq[None, MLA_NOPE:MLA_NOPE + half], half)], MLA_QK_DIM) * scale
    e1 = jnp.concatenate([cos * gk[None, MLA_NOPE:MLA_NOPE + half], cos * gk[None, MLA_NOPE + half:]], -1)
    e2 = jnp.concatenate([-sin * gk[None, MLA_NOPE + half:], sin * gk[None, MLA_NOPE:MLA_NOPE + half]], -1)
    tabs = {"tq1": _place([(tq1, LANES)], LANES), "tq2": _place([(tq2, LANES)], LANES),
            "e1": _place([(e1, LANES)], LANES), "e2": _place([(e2, LANES)], LANES)}
    gkf = _place([(gk[None, :MLA_NOPE], MLA_NOPE), (jnp.ones((1, MLA_ROPE), f32), LANES - MLA_NOPE)], LANES)

    na_scale = NA_HEAD_DIM ** -0.5
    lp = {
        "ln_g": p["ln_g"][l][None, :],
        "w_in": w_in_packed,
        "gq2": jnp.tile(p["na_q_norm"][l], 2)[None, :] * na_scale,
        "gk2": jnp.tile(p["na_k_norm"][l], 2)[None, :],
        "gcq": p["mla_cq_norm"][l][None, :],
        "gckv": p["mla_ckv_norm"][l][None, :],
        "wq": wq, "wqs": wqs, "wk": wk, "wv": wv, "gkf": gkf,
        "w_o_na": p["w_o_na"][l].astype(bf16),
        "w_o_mla": p["w_o_mla"][l].astype(bf16),
        "w_out": p["w_out"][l].astype(bf16),
    }
    return lp, tabs


@jax.jit
def _forward(x, p):
    bsz, seq_len, _ = x.shape
    depth = p["w_in"].shape[0]
    cos, sin = _rope_tables(seq_len)
    drow, dcol, valid = _na_table_indices(seq_len)
    x2d = x.reshape(bsz * seq_len, D_MODEL)
    for l in range(depth):
        lp, tabs = _layer_params(l, p, cos, sin)
        bias = p["na_rel_bias"][l][:, drow, dcol]
        tbl = jnp.where(valid[None], bias, MASK_VALUE).transpose(1, 0, 2, 3)
        qa, ka, va, nag, qf, kf, vb, mg, gna, gmla = _proj_call(x2d, lp, tabs, seq_len)

        def b3(a):
            return a.reshape(bsz, seq_len, a.shape[-1])

        a_na = _natt_call(b3(qa), b3(ka), b3(va), b3(nag), tbl)
        a_mla = _mla_call(b3(qf), b3(kf), b3(vb), b3(mg))
        x2d = _out_call(x2d, a_na.reshape(-1, NA_WIDTH), a_mla.reshape(-1, MLA_WIDTH), gna, gmla, lp)
    return x2d.reshape(bsz, seq_len, D_MODEL)


def kernel(x, ln_g, w_in, na_q_norm, na_k_norm, na_rel_bias, mla_cq_norm, mla_ckv_norm, w_uq, w_ukv,
           mla_q_norm, mla_k_norm, w_o_na, w_o_mla, w_out):
    params = dict(ln_g=ln_g, w_in=w_in, na_q_norm=na_q_norm, na_k_norm=na_k_norm,
                  na_rel_bias=na_rel_bias, mla_cq_norm=mla_cq_norm, mla_ckv_norm=mla_ckv_norm,
                  w_uq=w_uq, w_ukv=w_ukv, mla_q_norm=mla_q_norm, mla_k_norm=mla_k_norm,
                  w_o_na=w_o_na, w_o_mla=w_o_mla, w_out=w_out)
    return _forward(x, params)
```

```python
import functools

import numpy as np
import jax
import jax.numpy as jnp
from jax import lax
from jax.experimental import pallas as pl
from jax.experimental.pallas import tpu as pltpu

D_MODEL = 1024
GRID_W = 64
NA_HEADS = 8
NA_HEAD_DIM = 64
NA_WIDTH = NA_HEADS * NA_HEAD_DIM
NA_WIN_ROWS = 8
NA_WIN_COLS = 16
MLA_HEADS = 8
MLA_NOPE = 64
MLA_ROPE = 32
MLA_QK_DIM = MLA_NOPE + MLA_ROPE
MLA_V = 64
MLA_Q_LORA = 256
MLA_KV_LORA = 128
MLA_WIDTH = MLA_HEADS * MLA_V
ROPE_BASE = 10000.0
EPS = 1e-6

LANES = 128
HEAD_PAD = LANES
MLA_PAD_WIDTH = MLA_HEADS * HEAD_PAD
VMEM_LIMIT_BYTES = 56 * 1024 * 1024

_OFF_NAQ = 0
_OFF_NAK = _OFF_NAQ + NA_WIDTH
_OFF_NAV = _OFF_NAK + NA_WIDTH
_OFF_NAG = _OFF_NAV + NA_WIDTH
_OFF_CQ = _OFF_NAG + NA_WIDTH
_OFF_CKV = _OFF_CQ + MLA_Q_LORA
_OFF_MG = _OFF_CKV + 2 * LANES
_OFF_GNA = _OFF_MG + MLA_WIDTH
_OFF_GMLA = _OFF_GNA + D_MODEL
_W_IN_COLS = _OFF_GMLA + D_MODEL

PROJ_TILE = 512
NA_QROWS = 4
NA_BAND = 12
NA_TQ = NA_QROWS * GRID_W
NA_TK = NA_BAND * GRID_W
MLA_TQ = 256
MLA_KC = 512
MASK_VALUE = -1e30
LOG2E = 1.4426950408889634

_NT_DIMS = (((1,), (1,)), ((), ()))


def _dot(a, b):
    return jnp.dot(a, b, preferred_element_type=jnp.float32)


def _dot_nt(a, b):
    return lax.dot_general(a, b, _NT_DIMS, preferred_element_type=jnp.float32)


def _lane_iota():
    return lax.broadcasted_iota(jnp.int32, (1, LANES), 1)


def _proj_kernel(x_ref, lng_ref, w_ref, gq_ref, gk_ref, gcq_ref, gckv_ref, wq_ref, wqs_ref,
                 wk_ref, wv_ref, gkf_ref, tq1_ref, tq2_ref, e1_ref, e2_ref,
                 qa_ref, ka_ref, va_ref, nag_ref, qf_ref, kf_ref, vb_ref, mg_ref,
                 gna_ref, gmla_ref):
    bf16 = jnp.bfloat16
    x = x_ref[...]
    h = x * lax.rsqrt(jnp.mean(x * x, axis=-1, keepdims=True) + EPS) * lng_ref[...]
    hb = h.astype(bf16)
    lo_half = _lane_iota() < NA_HEAD_DIM

    def piece(off, width):
        return _dot(hb, w_ref[:, off:off + width])

    def headnorm64(p, g2, out_ref):
        for c in range(NA_WIDTH // LANES):
            pc = p[:, c * LANES:(c + 1) * LANES]
            sq = pc * pc
            s_lo = jnp.sum(jnp.where(lo_half, sq, 0.0), axis=-1, keepdims=True)
            s_hi = jnp.sum(jnp.where(lo_half, 0.0, sq), axis=-1, keepdims=True)
            r = jnp.where(lo_half,
                          lax.rsqrt(s_lo * (1.0 / NA_HEAD_DIM) + EPS),
                          lax.rsqrt(s_hi * (1.0 / NA_HEAD_DIM) + EPS))
            out_ref[:, c * LANES:(c + 1) * LANES] = (pc * r * g2).astype(bf16)

    headnorm64(piece(_OFF_NAQ, NA_WIDTH), gq_ref[...], qa_ref)
    headnorm64(piece(_OFF_NAK, NA_WIDTH), gk_ref[...], ka_ref)
    va_ref[...] = piece(_OFF_NAV, NA_WIDTH).astype(bf16)
    g = piece(_OFF_NAG, NA_WIDTH)
    nag_ref[...] = (g * jax.nn.sigmoid(g)).astype(bf16)

    cq = piece(_OFF_CQ, MLA_Q_LORA)
    cqn = (cq * lax.rsqrt(jnp.mean(cq * cq, axis=-1, keepdims=True) + EPS) * gcq_ref[...]).astype(bf16)
    qfull = _dot(cqn, wq_ref[...])
    qswap = _dot(cqn, wqs_ref[...])
    tq1 = tq1_ref[...]
    tq2 = tq2_ref[...]
    for hh in range(MLA_HEADS):
        a = qfull[:, hh * HEAD_PAD:(hh + 1) * HEAD_PAD]
        b = qswap[:, hh * HEAD_PAD:(hh + 1) * HEAD_PAD]
        ss = jnp.sum(a * a, axis=-1, keepdims=True)
        r = lax.rsqrt(ss / MLA_QK_DIM + EPS)
        qf_ref[:, hh * HEAD_PAD:(hh + 1) * HEAD_PAD] = ((a * tq1 + b * tq2) * r).astype(bf16)

    ckvx = piece(_OFF_CKV, 2 * LANES)
    ckv = ckvx[:, :LANES]
    e = ckvx[:, LANES:]
    ckvn = (ckv * lax.rsqrt(jnp.mean(ckv * ckv, axis=-1, keepdims=True) + EPS) * gckv_ref[...]).astype(bf16)
    rope_lanes = _lane_iota() < MLA_ROPE
    ss_pe = jnp.sum(jnp.where(rope_lanes, e * e, 0.0), axis=-1, keepdims=True)
    rot = e * e1_ref[...] + pltpu.roll(e, LANES - MLA_ROPE, axis=1) * e2_ref[...]
    rot_hi = rot.astype(bf16).astype(jnp.float32)
    rot_lo = (rot - rot_hi).astype(bf16).astype(jnp.float32)
    lhs2 = (rot_hi + pltpu.roll(rot_lo, MLA_ROPE, axis=1)).astype(bf16)
    kfull = _dot(jnp.concatenate([ckvn, lhs2], axis=-1), wk_ref[...])
    gkf = gkf_ref[...]
    for hh in range(MLA_HEADS):
        a = kfull[:, hh * HEAD_PAD:(hh + 1) * HEAD_PAD]
        ss = jnp.sum(jnp.where(lo_half, a * a, 0.0), axis=-1, keepdims=True) + ss_pe
        r = lax.rsqrt(ss / MLA_QK_DIM + EPS)
        kf_ref[:, hh * HEAD_PAD:(hh + 1) * HEAD_PAD] = (a * gkf * r).astype(bf16)
    vfull = _dot(ckvn, wv_ref[...])
    ones_lane = _lane_iota() == MLA_V
    for hh in range(MLA_HEADS):
        cols = slice(hh * HEAD_PAD, (hh + 1) * HEAD_PAD)
        vb_ref[:, cols] = jnp.where(ones_lane, 1.0, vfull[:, cols]).astype(bf16)

    g = piece(_OFF_MG, MLA_WIDTH)
    mg_ref[...] = (g * jax.nn.sigmoid(g)).astype(bf16)
    gna_ref[...] = jax.nn.sigmoid(piece(_OFF_GNA, D_MODEL)).astype(bf16)
    gmla_ref[...] = jax.nn.sigmoid(piece(_OFF_GMLA, D_MODEL)).astype(bf16)


def _const_spec(shape):
    nd = len(shape)
    return pl.BlockSpec(shape, lambda i: (0,) * nd, pipeline_mode=pl.Buffered(1))


def _proj_call(x2d, lp, tabs, seq_len):
    n_tok = x2d.shape[0]
    tm = PROJ_TILE
    assert n_tok % tm == 0 and seq_len % tm == 0
    tiles_per_seq = seq_len // tm
    bf16 = jnp.bfloat16

    def row_spec(width):
        return pl.BlockSpec((tm, width), lambda i: (i, 0))

    def tab_spec():
        return pl.BlockSpec((tm, LANES), lambda i: (i % tiles_per_seq, 0))

    consts = [lp["ln_g"], lp["w_in"], lp["gq2"], lp["gk2"], lp["gcq"], lp["gckv"],
              lp["wq"], lp["wqs"], lp["wk"], lp["wv"], lp["gkf"]]
    out_widths = [NA_WIDTH, NA_WIDTH, NA_WIDTH, NA_WIDTH, MLA_PAD_WIDTH, MLA_PAD_WIDTH,
                  MLA_PAD_WIDTH, MLA_WIDTH, D_MODEL, D_MODEL]
    return pl.pallas_call(
        _proj_kernel,
        grid=(n_tok // tm,),
        in_specs=[row_spec(D_MODEL)] + [_const_spec(c.shape) for c in consts] + [tab_spec()] * 4,
        out_specs=[row_spec(w) for w in out_widths],
        out_shape=[jax.ShapeDtypeStruct((n_tok, w), bf16) for w in out_widths],
        compiler_params=pltpu.CompilerParams(
            dimension_semantics=("arbitrary",), vmem_limit_bytes=VMEM_LIMIT_BYTES),
        name="proj",
    )(x2d, *consts, tabs["tq1"], tabs["tq2"], tabs["e1"], tabs["e2"])


def _na_band_start(i, rows):
    return jnp.clip(NA_QROWS * i - NA_WIN_ROWS // 2, 0, rows - NA_BAND)


def _natt_kernel(q_ref, k_ref, v_ref, gate_ref, tbl_ref, o_ref):
    bf16 = jnp.bfloat16
    i = pl.program_id(1)
    start = pl.multiple_of(_na_band_start(i, k_ref.shape[1] // GRID_W) * GRID_W, GRID_W)
    lo_half = _lane_iota() < NA_HEAD_DIM

    def scores(head):
        cols = slice((head // 2) * LANES, (head // 2 + 1) * LANES)
        qp = q_ref[0, :, cols]
        sel = lo_half if head % 2 == 0 else jnp.logical_not(lo_half)
        qm = jnp.where(sel, qp, jnp.zeros_like(qp))
        return _dot_nt(qm, k_ref[0, pl.ds(start, NA_TK), cols]) + tbl_ref[0, head]

    s_next = scores(0)
    halves = []
    for head in range(NA_HEADS):
        s = s_next
        if head + 1 < NA_HEADS:
            s_next = scores(head + 1)
        cols = slice((head // 2) * LANES, (head // 2 + 1) * LANES)
        m = jnp.max(s, axis=-1, keepdims=True)
        p = jnp.exp2(s - m)
        l = jnp.sum(p, axis=-1, keepdims=True)
        halves.append(_dot(p.astype(bf16), v_ref[0, pl.ds(start, NA_TK), cols]) / l)
        if head % 2 == 1:
            o = jnp.where(lo_half, halves[0], halves[1])
            o_ref[0, :, cols] = (o * gate_ref[0, :, cols].astype(jnp.float32)).astype(bf16)
            halves = []


def _natt_call(qa, ka, va, nag, tbl):
    bsz, seq_len, _ = qa.shape
    n_blk = seq_len // NA_TQ

    def pattern(i):
        return jnp.minimum(i, 1) + (i == n_blk - 1).astype(jnp.int32)

    blk = pl.BlockSpec((1, NA_TQ, NA_WIDTH), lambda b, i: (b, i, 0))
    full = pl.BlockSpec((1, seq_len, NA_WIDTH), lambda b, i: (b, 0, 0))
    return pl.pallas_call(
        _natt_kernel,
        grid=(bsz, n_blk),
        in_specs=[blk, full, full, blk,
                  pl.BlockSpec((1, NA_HEADS, NA_TQ, NA_TK), lambda b, i: (pattern(i), 0, 0, 0))],
        out_specs=blk,
        out_shape=jax.ShapeDtypeStruct((bsz, seq_len, NA_WIDTH), jnp.bfloat16),
        compiler_params=pltpu.CompilerParams(
            dimension_semantics=("arbitrary", "arbitrary"), vmem_limit_bytes=VMEM_LIMIT_BYTES),
        name="natt",
    )(qa, ka, va, nag, tbl)


def _mla_kernel(q_ref, k_ref, v_ref, gate_ref, o_ref, s_scr, p_scr):
    bf16 = jnp.bfloat16
    seq_len = k_ref.shape[1]
    n_chunks = seq_len // MLA_KC
    lo_half = _lane_iota() < MLA_V

    def hcols(head):
        return slice(head * HEAD_PAD, (head + 1) * HEAD_PAD)

    def kslice(c):
        return slice(c * MLA_KC, (c + 1) * MLA_KC)

    def qk_chunk(head, c, run_max):
        s = _dot_nt(q_ref[0, :, hcols(head)], k_ref[0, kslice(c), hcols(head)])
        s_scr[head, :, kslice(c)] = s
        for i in range(MLA_KC // LANES):
            tile = s[:, i * LANES:(i + 1) * LANES]
            run_max = tile if run_max is None else jnp.maximum(run_max, tile)
        return run_max

    def exp_chunk(head, c, m):
        p_scr[head, :, kslice(c)] = jnp.exp2(s_scr[head, :, kslice(c)] - m).astype(bf16)

    def pv_chunk(head, c, acc):
        o = _dot(p_scr[head, :, kslice(c)], v_ref[0, kslice(c), hcols(head)])
        return o if acc is None else acc + o

    run0 = None
    for c in range(n_chunks):
        run0 = qk_chunk(0, c, run0)
    m0 = jnp.max(run0, axis=-1, keepdims=True)
    run1 = None
    for c in range(n_chunks):
        run1 = qk_chunk(1, c, run1)
        exp_chunk(0, c, m0)
    m1 = jnp.max(run1, axis=-1, keepdims=True)
    acc0 = None
    for c in range(n_chunks):
        acc0 = pv_chunk(0, c, acc0)
        exp_chunk(1, c, m1)
    acc1 = None
    for c in range(n_chunks):
        acc1 = pv_chunk(1, c, acc1)
    o0 = acc0 / acc0[:, MLA_V:MLA_V + 1]
    o1 = acc1 / acc1[:, MLA_V:MLA_V + 1]
    o = jnp.where(lo_half, o0, pltpu.roll(o1, MLA_V, axis=1))
    o_ref[0] = (o * gate_ref[0].astype(jnp.float32)).astype(bf16)


def _mla_call(qf, kf, vb, mg):
    bsz, seq_len, _ = qf.shape
    n_pairs = MLA_HEADS // 2
    assert seq_len % MLA_KC == 0 and seq_len % MLA_TQ == 0
    return pl.pallas_call(
        _mla_kernel,
        grid=(bsz, n_pairs, seq_len // MLA_TQ),
        in_specs=[pl.BlockSpec((1, MLA_TQ, 2 * HEAD_PAD), lambda b, p, j: (b, j, p)),
                  pl.BlockSpec((1, seq_len, 2 * HEAD_PAD), lambda b, p, j: (b, 0, p)),
                  pl.BlockSpec((1, seq_len, 2 * HEAD_PAD), lambda b, p, j: (b, 0, p)),
                  pl.BlockSpec((1, MLA_TQ, 2 * MLA_V), lambda b, p, j: (b, j, p))],
        out_specs=pl.BlockSpec((1, MLA_TQ, 2 * MLA_V), lambda b, p, j: (b, j, p)),
        out_shape=jax.ShapeDtypeStruct((bsz, seq_len, MLA_WIDTH), jnp.bfloat16),
        scratch_shapes=[pltpu.VMEM((2, MLA_TQ, seq_len), jnp.float32),
                        pltpu.VMEM((2, MLA_TQ, seq_len), jnp.bfloat16)],
        compiler_params=pltpu.CompilerParams(
            dimension_semantics=("arbitrary", "arbitrary", "arbitrary"),
            vmem_limit_bytes=VMEM_LIMIT_BYTES),
        name="mla",
    )(qf, kf, vb, mg)


def _out_kernel(x_ref, ana_ref, amla_ref, gna_ref, gmla_ref, wna_ref, wmla_ref, wout_ref, o_ref):
    u_na = _dot(ana_ref[...], wna_ref[...])
    u_mla = _dot(amla_ref[...], wmla_ref[...])
    y = gna_ref[...].astype(jnp.float32) * u_na + gmla_ref[...].astype(jnp.float32) * u_mla
    o_ref[...] = x_ref[...] + _dot(y.astype(jnp.bfloat16), wout_ref[...])


def _out_call(x2d, a_na, a_mla, gna, gmla, lp):
    n_tok = x2d.shape[0]
    tm = PROJ_TILE

    def row_spec(width):
        return pl.BlockSpec((tm, width), lambda i: (i, 0))

    consts = [lp["w_o_na"], lp["w_o_mla"], lp["w_out"]]
    return pl.pallas_call(
        _out_kernel,
        grid=(n_tok // tm,),
        in_specs=[row_spec(D_MODEL), row_spec(NA_WIDTH), row_spec(MLA_WIDTH), row_spec(D_MODEL),
                  row_spec(D_MODEL)] + [_const_spec(c.shape) for c in consts],
        out_specs=row_spec(D_MODEL),
        out_shape=jax.ShapeDtypeStruct((n_tok, D_MODEL), jnp.float32),
        compiler_params=pltpu.CompilerParams(
            dimension_semantics=("arbitrary",), vmem_limit_bytes=VMEM_LIMIT_BYTES),
        name="out",
    )(x2d, a_na, a_mla, gna, gmla, *consts)


_N_DROW = 2 * NA_WIN_ROWS - 1
_N_DCOL = 2 * NA_WIN_COLS - 1


def _na_table_static(seq_len):
    rows = seq_len // GRID_W
    n_blk = rows // NA_QROWS
    qc = np.arange(GRID_W)
    kj = np.arange(GRID_W)
    cs = np.clip(qc - NA_WIN_COLS // 2, 0, GRID_W - NA_WIN_COLS)
    col_valid = (kj[None, :] >= cs[:, None]) & (kj[None, :] < cs[:, None] + NA_WIN_COLS)
    dcol = kj[None, :] - qc[:, None] + (NA_WIN_COLS - 1)
    onehot = (dcol[None] == np.arange(_N_DCOL)[:, None, None]) & col_valid[None]
    col_mask = np.where(col_valid, 0.0, MASK_VALUE).astype(np.float32)
    qr = np.arange(NA_QROWS)
    kn = np.arange(NA_BAND)
    idx = []
    for i in (0, 1, n_blk - 1):
        band = int(np.clip(NA_QROWS * i - NA_WIN_ROWS // 2, 0, rows - NA_BAND))
        r = NA_QROWS * i + qr
        rs = np.clip(r - NA_WIN_ROWS // 2, 0, rows - NA_WIN_ROWS)
        krow = band + kn
        valid = (krow[None, :] >= rs[:, None]) & (krow[None, :] < rs[:, None] + NA_WIN_ROWS)
        drow = krow[None, :] - r[:, None] + (NA_WIN_ROWS - 1)
        idx.append(np.where(valid, drow, _N_DROW))
    return onehot.astype(np.float32), col_mask, np.stack(idx).astype(np.int32)


def _na_table(rel_bias, onehot, col_mask, idx):
    blocks = jnp.einsum("hdc,cqj->hdqj", rel_bias, onehot, precision=lax.Precision.HIGHEST) + col_mask
    blocks = jnp.concatenate(
        [blocks, jnp.full((NA_HEADS, 1, GRID_W, GRID_W), MASK_VALUE, jnp.float32)], axis=1)
    tbl = jnp.take(blocks, idx.reshape(-1), axis=1)
    tbl = tbl.reshape(NA_HEADS, 3, NA_QROWS, NA_BAND, GRID_W, GRID_W)
    tbl = tbl.transpose(1, 0, 2, 4, 3, 5).reshape(3, NA_HEADS, NA_TQ, NA_TK)
    return tbl * LOG2E


def _rope_tables(seq_len):
    t = jnp.arange(seq_len)
    row = (t // GRID_W).astype(jnp.float32)
    col = (t % GRID_W).astype(jnp.float32)
    half = MLA_ROPE // 2
    n_freq = half // 2
    inv = jnp.power(jnp.float32(ROPE_BASE), -jnp.arange(n_freq, dtype=jnp.float32) / n_freq)
    ang = jnp.concatenate([row[:, None] * inv, col[:, None] * inv], axis=-1)
    return jnp.cos(ang), jnp.sin(ang)


def _place(cols_to_blocks, total):
    parts = []
    for arr, width in cols_to_blocks:
        pad = width - arr.shape[-1]
        parts.append(jnp.pad(arr, [(0, 0)] * (arr.ndim - 1) + [(0, pad)]) if pad else arr)
    out = jnp.concatenate(parts, axis=-1)
    assert out.shape[-1] == total
    return out


def _layer_params(l, p, cos, sin):
    f32, bf16 = jnp.float32, jnp.bfloat16
    half = MLA_ROPE // 2
    w_in = p["w_in"][l]
    o = 0
    pieces = {}
    for name, size in (("naq", NA_WIDTH), ("nak", NA_WIDTH), ("nav", NA_WIDTH), ("nag", NA_WIDTH),
                       ("cq", MLA_Q_LORA), ("ckv", MLA_KV_LORA), ("kpe", MLA_ROPE),
                       ("mg", MLA_WIDTH), ("gna", D_MODEL), ("gmla", D_MODEL)):
        pieces[name] = w_in[:, o:o + size]
        o += size
    kpe = pieces["kpe"]
    kpe_sw = jnp.concatenate([kpe[:, half:], kpe[:, :half]], axis=-1)
    ckvx = _place([(pieces["ckv"], LANES), (jnp.concatenate([kpe, kpe_sw], axis=-1), LANES)], 2 * LANES)
    w_in_packed = jnp.concatenate(
        [pieces["naq"], pieces["nak"], pieces["nav"], pieces["nag"], pieces["cq"], ckvx,
         pieces["mg"], pieces["gna"], pieces["gmla"]], axis=-1).astype(bf16)
    assert w_in_packed.shape[-1] == _W_IN_COLS

    w_uq = p["w_uq"][l].reshape(MLA_Q_LORA, MLA_HEADS, MLA_QK_DIM)
    nope, x1, x2 = w_uq[..., :MLA_NOPE], w_uq[..., MLA_NOPE:MLA_NOPE + half], w_uq[..., MLA_NOPE + half:]
    zq = jnp.zeros_like(nope)
    wq = _place([(jnp.concatenate([nope, x1, x2], -1), HEAD_PAD)], HEAD_PAD)
    wqs = _place([(jnp.concatenate([zq, x2, x1], -1), HEAD_PAD)], HEAD_PAD)
    wq = wq.reshape(MLA_Q_LORA, MLA_PAD_WIDTH).astype(bf16)
    wqs = wqs.reshape(MLA_Q_LORA, MLA_PAD_WIDTH).astype(bf16)

    w_ukv = p["w_ukv"][l].reshape(MLA_KV_LORA, MLA_HEADS, MLA_NOPE + MLA_V)
    k_nope = _place([(w_ukv[..., :MLA_NOPE], HEAD_PAD)], HEAD_PAD).reshape(MLA_KV_LORA, MLA_PAD_WIDTH)
    eye = jnp.eye(MLA_ROPE, dtype=f32)
    put = jnp.pad(eye, ((0, 0), (MLA_NOPE, HEAD_PAD - MLA_QK_DIM)))
    put = jnp.tile(put, (1, MLA_HEADS))
    wk = jnp.concatenate([k_nope, put, put, jnp.zeros((2 * LANES - MLA_KV_LORA - 2 * MLA_ROPE,
                                                        MLA_PAD_WIDTH), f32)], axis=0).astype(bf16)
    wv = _place([(w_ukv[..., MLA_NOPE:], HEAD_PAD)], HEAD_PAD).reshape(MLA_KV_LORA, MLA_PAD_WIDTH).astype(bf16)

    gq = p["mla_q_norm"][l]
    gk = p["mla_k_norm"][l]
    scale = MLA_QK_DIM ** -0.5 * LOG2E
    seq_len = cos.shape[0]
    ones = jnp.ones((seq_len, 1), f32)
    tq1 = _place([(ones * gq[None, :MLA_NOPE], MLA_NOPE),
                  (cos * gq[None, MLA_NOPE:MLA_NOPE + half], half),
                  (cos * gq[None, MLA_NOPE + half:], half)], MLA_QK_DIM) * scale
    tq2 = _place([(jnp.zeros((seq_len, MLA_NOPE), f32), MLA_NOPE),
                  (-sin * gq[None, MLA_NOPE + half:], half),
                  (sin * gq[None, MLA_NOPE:MLA_NOPE + half], half)], MLA_QK_DIM) * scale
    e1 = jnp.concatenate([cos * gk[None, MLA_NOPE:MLA_NOPE + half], cos * gk[None, MLA_NOPE + half:]], -1)
    e2 = jnp.concatenate([-sin * gk[None, MLA_NOPE + half:], sin * gk[None, MLA_NOPE:MLA_NOPE + half]], -1)
    tabs = {"tq1": _place([(tq1, LANES)], LANES), "tq2": _place([(tq2, LANES)], LANES),
            "e1": _place([(e1, LANES)], LANES), "e2": _place([(e2, LANES)], LANES)}
    gkf = _place([(gk[None, :MLA_NOPE], MLA_NOPE), (jnp.ones((1, MLA_ROPE), f32), LANES - MLA_NOPE)], LANES)

    na_scale = NA_HEAD_DIM ** -0.5 * LOG2E
    lp = {
        "ln_g": p["ln_g"][l][None, :],
        "w_in": w_in_packed,
        "gq2": jnp.tile(p["na_q_norm"][l], 2)[None, :] * na_scale,
        "gk2": jnp.tile(p["na_k_norm"][l], 2)[None, :],
        "gcq": p["mla_cq_norm"][l][None, :],
        "gckv": p["mla_ckv_norm"][l][None, :],
        "wq": wq, "wqs": wqs, "wk": wk, "wv": wv, "gkf": gkf,
        "w_o_na": p["w_o_na"][l].astype(bf16),
        "w_o_mla": p["w_o_mla"][l].astype(bf16),
        "w_out": p["w_out"][l].astype(bf16),
    }
    return lp, tabs


@jax.jit
def _forward(x, p):
    bsz, seq_len, _ = x.shape
    depth = p["w_in"].shape[0]
    cos, sin = _rope_tables(seq_len)
    onehot, col_mask, tbl_idx = _na_table_static(seq_len)
    x2d = x.reshape(bsz * seq_len, D_MODEL)
    for l in range(depth):
        lp, tabs = _layer_params(l, p, cos, sin)
        tbl = _na_table(p["na_rel_bias"][l], onehot, col_mask, tbl_idx)
        qa, ka, va, nag, qf, kf, vb, mg, gna, gmla = _proj_call(x2d, lp, tabs, seq_len)

        def b3(a):
            return a.reshape(bsz, seq_len, a.shape[-1])

        a_na = _natt_call(b3(qa), b3(ka), b3(va), b3(nag), tbl)
        a_mla = _mla_call(b3(qf), b3(kf), b3(vb), b3(mg))
        x2d = _out_call(x2d, a_na.reshape(-1, NA_WIDTH), a_mla.reshape(-1, MLA_WIDTH), gna, gmla, lp)
    return x2d.reshape(bsz, seq_len, D_MODEL)


def kernel(x, ln_g, w_in, na_q_norm, na_k_norm, na_rel_bias, mla_cq_norm, mla_ckv_norm, w_uq, w_ukv,
           mla_q_norm, mla_k_norm, w_o_na, w_o_mla, w_out):
    params = dict(ln_g=ln_g, w_in=w_in, na_q_norm=na_q_norm, na_k_norm=na_k_norm,
                  na_rel_bias=na_rel_bias, mla_cq_norm=mla_cq_norm, mla_ckv_norm=mla_ckv_norm,
                  w_uq=w_uq, w_ukv=w_ukv, mla_q_norm=mla_q_norm, mla_k_norm=mla_k_norm,
                  w_o_na=w_o_na, w_o_mla=w_o_mla, w_out=w_out)
    return _forward(x, params)
```

```python
import functools

import numpy as np
import jax
import jax.numpy as jnp
from jax import lax
from jax.experimental import pallas as pl
from jax.experimental.pallas import tpu as pltpu

D_MODEL = 1024
GRID_W = 64
NA_HEADS = 8
NA_HEAD_DIM = 64
NA_WIDTH = NA_HEADS * NA_HEAD_DIM
NA_WIN_ROWS = 8
NA_WIN_COLS = 16
MLA_HEADS = 8
MLA_NOPE = 64
MLA_ROPE = 32
MLA_QK_DIM = MLA_NOPE + MLA_ROPE
MLA_V = 64
MLA_Q_LORA = 256
MLA_KV_LORA = 128
MLA_WIDTH = MLA_HEADS * MLA_V
ROPE_BASE = 10000.0
EPS = 1e-6

LANES = 128
HEAD_PAD = LANES
MLA_PAD_WIDTH = MLA_HEADS * HEAD_PAD
VMEM_LIMIT_BYTES = 56 * 1024 * 1024

_OFF_NAQ = 0
_OFF_NAK = _OFF_NAQ + NA_WIDTH
_OFF_NAV = _OFF_NAK + NA_WIDTH
_OFF_NAG = _OFF_NAV + NA_WIDTH
_OFF_CQ = _OFF_NAG + NA_WIDTH
_OFF_CKV = _OFF_CQ + MLA_Q_LORA
_OFF_MG = _OFF_CKV + 2 * LANES
_OFF_GNA = _OFF_MG + MLA_WIDTH
_OFF_GMLA = _OFF_GNA + D_MODEL
_W_IN_COLS = _OFF_GMLA + D_MODEL

PROJ_TILE = 512
PROJ_BLK = 256
NA_QROWS = 4
NA_BAND = 12
NA_TQ = NA_QROWS * GRID_W
NA_TK = NA_BAND * GRID_W
NA_KC = 768
MLA_TQ = 256
MLA_KC = 512
MLA_GROUP = 4
MASK_VALUE = -1e30
LOG2E = 1.4426950408889634

_NT_DIMS = (((1,), (1,)), ((), ()))


def _dot(a, b):
    return jnp.dot(a, b, preferred_element_type=jnp.float32)


def _dot_nt(a, b):
    return lax.dot_general(a, b, _NT_DIMS, preferred_element_type=jnp.float32)


def _lane_iota():
    return lax.broadcasted_iota(jnp.int32, (1, LANES), 1)


def _proj_kernel(x_ref, lng_ref, w_ref, gq_ref, gk_ref, gcq_ref, gckv_ref, wq_ref, wqs_ref,
                 wk_ref, wv_ref, gkf_ref, tq1_ref, tq2_ref, e1_ref, e2_ref,
                 qa_ref, ka_ref, va_ref, nag_ref, qf_ref, kf_ref, vb_ref, mg_ref,
                 gna_ref, gmla_ref, hb_scr, cqn_scr, lhsk_scr, sspe_scr):
    bf16 = jnp.bfloat16
    f32 = jnp.float32
    x = x_ref[...]
    hb_scr[...] = (x * lax.rsqrt(jnp.mean(x * x, axis=-1, keepdims=True) + EPS) * lng_ref[...]).astype(bf16)
    lo_half = _lane_iota() < NA_HEAD_DIM
    rope_lanes = _lane_iota() < MLA_ROPE
    ones_lane = _lane_iota() == MLA_V

    def tiles(y):
        return [y[:, i * LANES:(i + 1) * LANES] for i in range(y.shape[1] // LANES)]

    def in_dot(off):
        return lambda: _dot(hb_scr[...], w_ref[:, off:off + PROJ_BLK])

    def headnorm64(out_ref, g_ref, col0):
        def post(y):
            for i, pc in enumerate(tiles(y)):
                sq = pc * pc
                s_lo = jnp.sum(jnp.where(lo_half, sq, 0.0), axis=-1, keepdims=True)
                s_hi = jnp.sum(jnp.where(lo_half, 0.0, sq), axis=-1, keepdims=True)
                r = jnp.where(lo_half,
                              lax.rsqrt(s_lo * (1.0 / NA_HEAD_DIM) + EPS),
                              lax.rsqrt(s_hi * (1.0 / NA_HEAD_DIM) + EPS))
                out_ref[:, col0 + i * LANES:col0 + (i + 1) * LANES] = (pc * r * g_ref[...]).astype(bf16)
        return post

    def elementwise(out_ref, col0, fn):
        def post(y):
            out_ref[:, col0:col0 + PROJ_BLK] = fn(y).astype(bf16)
        return post

    def silu(g):
        return g * jax.nn.sigmoid(g)

    def post_cq(y):
        cqn_scr[...] = (y * lax.rsqrt(jnp.mean(y * y, axis=-1, keepdims=True) + EPS) * gcq_ref[...]).astype(bf16)

    def post_ckv(y):
        ckv, e = tiles(y)
        lhsk_scr[:, :LANES] = (ckv * lax.rsqrt(jnp.mean(ckv * ckv, axis=-1, keepdims=True) + EPS)
                               * gckv_ref[...]).astype(bf16)
        ss_pe = jnp.sum(jnp.where(rope_lanes, e * e, 0.0), axis=-1, keepdims=True)
        sspe_scr[...] = jnp.broadcast_to(ss_pe, sspe_scr.shape)
        rot = e * e1_ref[...] + pltpu.roll(e, LANES - MLA_ROPE, axis=1) * e2_ref[...]
        rot_hi = rot.astype(bf16).astype(f32)
        rot_lo = (rot - rot_hi).astype(bf16).astype(f32)
        lhsk_scr[:, LANES:] = (rot_hi + pltpu.roll(rot_lo, MLA_ROPE, axis=1)).astype(bf16)

    def q_dot(pair):
        cols = slice(pair * PROJ_BLK, (pair + 1) * PROJ_BLK)
        return lambda: (_dot(cqn_scr[...], wq_ref[:, cols]), _dot(cqn_scr[...], wqs_ref[:, cols]))

    def q_post(pair):
        def post(ys):
            for i, (a, b) in enumerate(zip(tiles(ys[0]), tiles(ys[1]))):
                r = lax.rsqrt(jnp.sum(a * a, axis=-1, keepdims=True) / MLA_QK_DIM + EPS)
                col0 = pair * PROJ_BLK + i * HEAD_PAD
                qf_ref[:, col0:col0 + HEAD_PAD] = ((a * tq1_ref[...] + b * tq2_ref[...]) * r).astype(bf16)
        return post

    def k_dot(pair):
        cols = slice(pair * PROJ_BLK, (pair + 1) * PROJ_BLK)
        return lambda: _dot(lhsk_scr[...], wk_ref[:, cols])

    def k_post(pair):
        def post(y):
            for i, a in enumerate(tiles(y)):
                ss = jnp.sum(jnp.where(lo_half, a * a, 0.0), axis=-1, keepdims=True) + sspe_scr[...]
                r = lax.rsqrt(ss / MLA_QK_DIM + EPS)
                col0 = pair * PROJ_BLK + i * HEAD_PAD
                kf_ref[:, col0:col0 + HEAD_PAD] = (a * gkf_ref[...] * r).astype(bf16)
        return post

    def v_dot(pair):
        cols = slice(pair * PROJ_BLK, (pair + 1) * PROJ_BLK)
        return lambda: _dot(lhsk_scr[:, :LANES], wv_ref[:, cols])

    def v_post(pair):
        def post(y):
            for i, a in enumerate(tiles(y)):
                col0 = pair * PROJ_BLK + i * HEAD_PAD
                vb_ref[:, col0:col0 + HEAD_PAD] = jnp.where(ones_lane, 1.0, a).astype(bf16)
        return post

    def in_blocks(off, width, make_post):
        return [(in_dot(off + c), make_post(c)) for c in range(0, width, PROJ_BLK)]

    tasks = [(in_dot(_OFF_CQ), post_cq), (in_dot(_OFF_CKV), post_ckv)]
    tasks += in_blocks(_OFF_NAQ, NA_WIDTH, lambda c: headnorm64(qa_ref, gq_ref, c))
    tasks += in_blocks(_OFF_NAK, NA_WIDTH, lambda c: headnorm64(ka_ref, gk_ref, c))
    tasks += [(q_dot(p), q_post(p)) for p in range(MLA_PAD_WIDTH // PROJ_BLK)]
    tasks += in_blocks(_OFF_NAV, NA_WIDTH, lambda c: elementwise(va_ref, c, lambda y: y))
    tasks += [(k_dot(p), k_post(p)) for p in range(MLA_PAD_WIDTH // PROJ_BLK)]
    tasks += in_blocks(_OFF_NAG, NA_WIDTH, lambda c: elementwise(nag_ref, c, silu))
    tasks += [(v_dot(p), v_post(p)) for p in range(MLA_PAD_WIDTH // PROJ_BLK)]
    tasks += in_blocks(_OFF_MG, MLA_WIDTH, lambda c: elementwise(mg_ref, c, silu))
    tasks += in_blocks(_OFF_GNA, D_MODEL, lambda c: elementwise(gna_ref, c, jax.nn.sigmoid))
    tasks += in_blocks(_OFF_GMLA, D_MODEL, lambda c: elementwise(gmla_ref, c, jax.nn.sigmoid))

    pending = None
    for dot_fn, post_fn in tasks:
        y = dot_fn()
        if pending is not None:
            pending[0](pending[1])
        pending = (post_fn, y)
    pending[0](pending[1])


def _const_spec(shape):
    nd = len(shape)
    return pl.BlockSpec(shape, lambda i: (0,) * nd, pipeline_mode=pl.Buffered(1))


def _proj_call(x2d, lp, tabs, seq_len):
    n_tok = x2d.shape[0]
    tm = PROJ_TILE
    assert n_tok % tm == 0 and seq_len % tm == 0
    tiles_per_seq = seq_len // tm
    bf16 = jnp.bfloat16

    def row_spec(width):
        return pl.BlockSpec((tm, width), lambda i: (i, 0))

    def tab_spec():
        return pl.BlockSpec((tm, LANES), lambda i: (i % tiles_per_seq, 0))

    consts = [lp["ln_g"], lp["w_in"], lp["gq2"], lp["gk2"], lp["gcq"], lp["gckv"],
              lp["wq"], lp["wqs"], lp["wk"], lp["wv"], lp["gkf"]]
    out_widths = [NA_WIDTH, NA_WIDTH, NA_WIDTH, NA_WIDTH, MLA_PAD_WIDTH, MLA_PAD_WIDTH,
                  MLA_PAD_WIDTH, MLA_WIDTH, D_MODEL, D_MODEL]
    return pl.pallas_call(
        _proj_kernel,
        grid=(n_tok // tm,),
        in_specs=[row_spec(D_MODEL)] + [_const_spec(c.shape) for c in consts] + [tab_spec()] * 4,
        out_specs=[row_spec(w) for w in out_widths],
        out_shape=[jax.ShapeDtypeStruct((n_tok, w), bf16) for w in out_widths],
        scratch_shapes=[pltpu.VMEM((tm, D_MODEL), bf16),
                        pltpu.VMEM((tm, MLA_Q_LORA), bf16),
                        pltpu.VMEM((tm, 2 * LANES), bf16),
                        pltpu.VMEM((tm, LANES), jnp.float32)],
        compiler_params=pltpu.CompilerParams(
            dimension_semantics=("arbitrary",), vmem_limit_bytes=VMEM_LIMIT_BYTES),
        name="proj",
    )(x2d, *consts, tabs["tq1"], tabs["tq2"], tabs["e1"], tabs["e2"])


def _na_band_start(i, rows):
    return jnp.clip(NA_QROWS * i - NA_WIN_ROWS // 2, 0, rows - NA_BAND)


def _natt_kernel(q_ref, k_ref, v_ref, gate_ref, tbl_ref, o_ref, s_scr, p_scr):
    bf16 = jnp.bfloat16
    i = pl.program_id(1)
    start = pl.multiple_of(_na_band_start(i, k_ref.shape[1] // GRID_W) * GRID_W, GRID_W)
    lo_half = _lane_iota() < NA_HEAD_DIM
    n_chunks = NA_TK // NA_KC

    def pcols(head):
        return slice((head // 2) * LANES, (head // 2 + 1) * LANES)

    def kslice(c):
        return slice(c * NA_KC, (c + 1) * NA_KC)

    def fold(x, run, op):
        for t in range(x.shape[1] // LANES):
            tile = x[:, t * LANES:(t + 1) * LANES]
            run = tile if run is None else op(run, tile)
        return run

    def masked_q(head):
        qp = q_ref[0, :, pcols(head)]
        sel = lo_half if head % 2 == 0 else jnp.logical_not(lo_half)
        return jnp.where(sel, qp, jnp.zeros_like(qp))

    def qk_chunk(head, qm, c, run_max):
        keys = k_ref[0, pl.ds(start + c * NA_KC, NA_KC), pcols(head)]
        s = _dot_nt(qm, keys) + tbl_ref[0, head, :, kslice(c)]
        s_scr[head % 2, :, kslice(c)] = s
        return fold(s, run_max, jnp.maximum)

    def exp_chunk(head, c, m, run_sum):
        p = jnp.exp2(s_scr[head % 2, :, kslice(c)] - m)
        p_scr[head % 2, :, kslice(c)] = p.astype(bf16)
        return fold(p, run_sum, jnp.add)

    def pv_chunk(head, c, acc):
        vals = v_ref[0, pl.ds(start + c * NA_KC, NA_KC), pcols(head)]
        o = _dot(p_scr[head % 2, :, kslice(c)], vals)
        return o if acc is None else acc + o

    def finish(head, acc, run_sum, prev):
        o = acc / jnp.sum(run_sum, axis=-1, keepdims=True)
        if head % 2 == 0:
            return o
        pair = jnp.where(lo_half, prev, o)
        o_ref[0, :, pcols(head)] = (pair * gate_ref[0, :, pcols(head)].astype(jnp.float32)).astype(bf16)
        return None

    qm = masked_q(0)
    run = None
    for c in range(n_chunks):
        run = qk_chunk(0, qm, c, run)
    m = jnp.max(run, axis=-1, keepdims=True)
    prev_sum, even_out = None, None
    for head in range(NA_HEADS):
        run, acc, cur_sum = None, None, None
        if head + 1 < NA_HEADS:
            qm = masked_q(head + 1)
        for c in range(n_chunks):
            if head + 1 < NA_HEADS:
                run = qk_chunk(head + 1, qm, c, run)
            if head >= 1:
                acc = pv_chunk(head - 1, c, acc)
            cur_sum = exp_chunk(head, c, m, cur_sum)
        if head >= 1:
            even_out = finish(head - 1, acc, prev_sum, even_out)
        prev_sum = cur_sum
        if head + 1 < NA_HEADS:
            m = jnp.max(run, axis=-1, keepdims=True)
    acc = None
    for c in range(n_chunks):
        acc = pv_chunk(NA_HEADS - 1, c, acc)
    finish(NA_HEADS - 1, acc, prev_sum, even_out)


def _natt_call(qa, ka, va, nag, tbl):
    bsz, seq_len, _ = qa.shape
    n_blk = seq_len // NA_TQ

    def pattern(i):
        return jnp.minimum(i, 1) + (i == n_blk - 1).astype(jnp.int32)

    blk = pl.BlockSpec((1, NA_TQ, NA_WIDTH), lambda b, i: (b, i, 0))
    full = pl.BlockSpec((1, seq_len, NA_WIDTH), lambda b, i: (b, 0, 0))
    return pl.pallas_call(
        _natt_kernel,
        grid=(bsz, n_blk),
        in_specs=[blk, full, full, blk,
                  pl.BlockSpec((1, NA_HEADS, NA_TQ, NA_TK), lambda b, i: (pattern(i), 0, 0, 0))],
        out_specs=blk,
        out_shape=jax.ShapeDtypeStruct((bsz, seq_len, NA_WIDTH), jnp.bfloat16),
        scratch_shapes=[pltpu.VMEM((2, NA_TQ, NA_TK), jnp.float32),
                        pltpu.VMEM((2, NA_TQ, NA_TK), jnp.bfloat16)],
        compiler_params=pltpu.CompilerParams(
            dimension_semantics=("arbitrary", "arbitrary"), vmem_limit_bytes=VMEM_LIMIT_BYTES),
        name="natt",
    )(qa, ka, va, nag, tbl)


def _mla_kernel(q_ref, k_ref, v_ref, gate_ref, o_ref, s_scr, p_scr):
    bf16 = jnp.bfloat16
    seq_len = k_ref.shape[1]
    n_chunks = seq_len // MLA_KC
    n_heads = q_ref.shape[2] // HEAD_PAD
    lo_half = _lane_iota() < MLA_V

    def hcols(head):
        return slice(head * HEAD_PAD, (head + 1) * HEAD_PAD)

    def kslice(c):
        return slice(c * MLA_KC, (c + 1) * MLA_KC)

    def qk_chunk(head, c, run_max):
        s = _dot_nt(q_ref[0, :, hcols(head)], k_ref[0, kslice(c), hcols(head)])
        s_scr[head % 2, :, kslice(c)] = s
        for i in range(MLA_KC // LANES):
            tile = s[:, i * LANES:(i + 1) * LANES]
            run_max = tile if run_max is None else jnp.maximum(run_max, tile)
        return run_max

    def exp_chunk(head, c, m):
        p_scr[head % 2, :, kslice(c)] = jnp.exp2(s_scr[head % 2, :, kslice(c)] - m).astype(bf16)

    def pv_chunk(head, c, acc):
        pair = slice((head // 2) * 2 * HEAD_PAD, (head // 2 + 1) * 2 * HEAD_PAD)
        o = _dot(p_scr[head % 2, :, kslice(c)], v_ref[0, kslice(c), pair])
        return o if acc is None else acc + o

    def finish(head, acc, prev):
        acc = acc[:, (head % 2) * HEAD_PAD:(head % 2 + 1) * HEAD_PAD]
        o = acc / acc[:, MLA_V:MLA_V + 1]
        if head % 2 == 0:
            return o
        cols = slice((head // 2) * LANES, (head // 2 + 1) * LANES)
        pair = jnp.where(lo_half, prev, pltpu.roll(o, MLA_V, axis=1))
        o_ref[0, :, cols] = (pair * gate_ref[0, :, cols].astype(jnp.float32)).astype(bf16)
        return None

    run = None
    for c in range(n_chunks):
        run = qk_chunk(0, c, run)
    m = jnp.max(run, axis=-1, keepdims=True)
    acc, even_out = None, None
    for head in range(n_heads):
        run, next_acc = None, None
        for c in range(n_chunks):
            if head + 1 < n_heads:
                run = qk_chunk(head + 1, c, run)
            if head >= 1:
                next_acc = pv_chunk(head - 1, c, next_acc)
            exp_chunk(head, c, m)
        if head >= 1:
            even_out = finish(head - 1, next_acc, even_out)
        if head + 1 < n_heads:
            m = jnp.max(run, axis=-1, keepdims=True)
    for c in range(n_chunks):
        acc = pv_chunk(n_heads - 1, c, acc)
    finish(n_heads - 1, acc, even_out)


def _mla_call(qf, kf, vb, mg):
    bsz, seq_len, _ = qf.shape
    n_groups = MLA_HEADS // MLA_GROUP
    assert seq_len % MLA_KC == 0 and seq_len % MLA_TQ == 0 and MLA_GROUP % 2 == 0
    return pl.pallas_call(
        _mla_kernel,
        grid=(bsz, n_groups, seq_len // MLA_TQ),
        in_specs=[pl.BlockSpec((1, MLA_TQ, MLA_GROUP * HEAD_PAD), lambda b, p, j: (b, j, p)),
                  pl.BlockSpec((1, seq_len, MLA_GROUP * HEAD_PAD), lambda b, p, j: (b, 0, p)),
                  pl.BlockSpec((1, seq_len, MLA_GROUP * HEAD_PAD), lambda b, p, j: (b, 0, p)),
                  pl.BlockSpec((1, MLA_TQ, MLA_GROUP * MLA_V), lambda b, p, j: (b, j, p))],
        out_specs=pl.BlockSpec((1, MLA_TQ, MLA_GROUP * MLA_V), lambda b, p, j: (b, j, p)),
        out_shape=jax.ShapeDtypeStruct((bsz, seq_len, MLA_WIDTH), jnp.bfloat16),
        scratch_shapes=[pltpu.VMEM((2, MLA_TQ, seq_len), jnp.float32),
                        pltpu.VMEM((2, MLA_TQ, seq_len), jnp.bfloat16)],
        compiler_params=pltpu.CompilerParams(
            dimension_semantics=("arbitrary", "arbitrary", "arbitrary"),
            vmem_limit_bytes=VMEM_LIMIT_BYTES),
        name="mla",
    )(qf, kf, vb, mg)


def _out_kernel(x_ref, ana_ref, amla_ref, gna_ref, gmla_ref, wna_ref, wmla_ref, wout_ref, o_ref, y_scr):
    f32 = jnp.float32
    n_blk = o_ref.shape[1] // PROJ_BLK

    def branch_dots(c):
        cols = slice(c * PROJ_BLK, (c + 1) * PROJ_BLK)
        return _dot(ana_ref[...], wna_ref[:, cols]), _dot(amla_ref[...], wmla_ref[:, cols])

    def merge(c, us):
        cols = slice(c * PROJ_BLK, (c + 1) * PROJ_BLK)
        y = gna_ref[:, cols].astype(f32) * us[0] + gmla_ref[:, cols].astype(f32) * us[1]
        y_scr[:, cols] = y.astype(jnp.bfloat16)

    us = branch_dots(0)
    for c in range(n_blk):
        nxt = branch_dots(c + 1) if c + 1 < n_blk else None
        merge(c, us)
        us = nxt
    y = None
    for c in range(n_blk):
        cols = slice(c * PROJ_BLK, (c + 1) * PROJ_BLK)
        nxt = _dot(y_scr[...], wout_ref[:, cols])
        if y is not None:
            pc = slice((c - 1) * PROJ_BLK, c * PROJ_BLK)
            o_ref[:, pc] = x_ref[:, pc] + y
        y = nxt
    pc = slice((n_blk - 1) * PROJ_BLK, n_blk * PROJ_BLK)
    o_ref[:, pc] = x_ref[:, pc] + y


def _out_call(x2d, a_na, a_mla, gna, gmla, lp):
    n_tok = x2d.shape[0]
    tm = PROJ_TILE

    def row_spec(width):
        return pl.BlockSpec((tm, width), lambda i: (i, 0))

    consts = [lp["w_o_na"], lp["w_o_mla"], lp["w_out"]]
    return pl.pallas_call(
        _out_kernel,
        grid=(n_tok // tm,),
        in_specs=[row_spec(D_MODEL), row_spec(NA_WIDTH), row_spec(MLA_WIDTH), row_spec(D_MODEL),
                  row_spec(D_MODEL)] + [_const_spec(c.shape) for c in consts],
        out_specs=row_spec(D_MODEL),
        out_shape=jax.ShapeDtypeStruct((n_tok, D_MODEL), jnp.float32),
        scratch_shapes=[pltpu.VMEM((tm, D_MODEL), jnp.bfloat16)],
        compiler_params=pltpu.CompilerParams(
            dimension_semantics=("arbitrary",), vmem_limit_bytes=VMEM_LIMIT_BYTES),
        name="out",
    )(x2d, a_na, a_mla, gna, gmla, *consts)


_N_DROW = 2 * NA_WIN_ROWS - 1
_N_DCOL = 2 * NA_WIN_COLS - 1


def _na_table_static(seq_len):
    rows = seq_len // GRID_W
    n_blk = rows // NA_QROWS
    qc = np.arange(GRID_W)
    kj = np.arange(GRID_W)
    cs = np.clip(qc - NA_WIN_COLS // 2, 0, GRID_W - NA_WIN_COLS)
    col_valid = (kj[None, :] >= cs[:, None]) & (kj[None, :] < cs[:, None] + NA_WIN_COLS)
    dcol = kj[None, :] - qc[:, None] + (NA_WIN_COLS - 1)
    onehot = (dcol[None] == np.arange(_N_DCOL)[:, None, None]) & col_valid[None]
    col_mask = np.where(col_valid, 0.0, MASK_VALUE).astype(np.float32)
    qr = np.arange(NA_QROWS)
    kn = np.arange(NA_BAND)
    idx = []
    for i in (0, 1, n_blk - 1):
        band = int(np.clip(NA_QROWS * i - NA_WIN_ROWS // 2, 0, rows - NA_BAND))
        r = NA_QROWS * i + qr
        rs = np.clip(r - NA_WIN_ROWS // 2, 0, rows - NA_WIN_ROWS)
        krow = band + kn
        valid = (krow[None, :] >= rs[:, None]) & (krow[None, :] < rs[:, None] + NA_WIN_ROWS)
        drow = krow[None, :] - r[:, None] + (NA_WIN_ROWS - 1)
        idx.append(np.where(valid, drow, _N_DROW))
    return onehot.astype(np.float32), col_mask, np.stack(idx).astype(np.int32)


def _na_table(rel_bias, onehot, col_mask, idx):
    blocks = jnp.einsum("hdc,cqj->hdqj", rel_bias, onehot, precision=lax.Precision.HIGHEST) + col_mask
    blocks = jnp.concatenate(
        [blocks, jnp.full((NA_HEADS, 1, GRID_W, GRID_W), MASK_VALUE, jnp.float32)], axis=1) * LOG2E
    tbl = jnp.take(blocks, idx.reshape(-1), axis=1)
    tbl = tbl.reshape(NA_HEADS, 3, NA_QROWS, NA_BAND, GRID_W, GRID_W)
    return tbl.transpose(1, 0, 2, 4, 3, 5).reshape(3, NA_HEADS, NA_TQ, NA_TK)


def _rope_tables(seq_len):
    t = jnp.arange(seq_len)
    row = (t // GRID_W).astype(jnp.float32)
    col = (t % GRID_W).astype(jnp.float32)
    half = MLA_ROPE // 2
    n_freq = half // 2
    inv = jnp.power(jnp.float32(ROPE_BASE), -jnp.arange(n_freq, dtype=jnp.float32) / n_freq)
    ang = jnp.concatenate([row[:, None] * inv, col[:, None] * inv], axis=-1)
    return jnp.cos(ang), jnp.sin(ang)


def _place(cols_to_blocks, total):
    parts = []
    for arr, width in cols_to_blocks:
        pad = width - arr.shape[-1]
        parts.append(jnp.pad(arr, [(0, 0)] * (arr.ndim - 1) + [(0, pad)]) if pad else arr)
    out = jnp.concatenate(parts, axis=-1)
    assert out.shape[-1] == total
    return out


def _layer_params(l, p, cos, sin):
    f32, bf16 = jnp.float32, jnp.bfloat16
    half = MLA_ROPE // 2
    w_in = p["w_in"][l]
    o = 0
    pieces = {}
    for name, size in (("naq", NA_WIDTH), ("nak", NA_WIDTH), ("nav", NA_WIDTH), ("nag", NA_WIDTH),
                       ("cq", MLA_Q_LORA), ("ckv", MLA_KV_LORA), ("kpe", MLA_ROPE),
                       ("mg", MLA_WIDTH), ("gna", D_MODEL), ("gmla", D_MODEL)):
        pieces[name] = w_in[:, o:o + size]
        o += size
    kpe = pieces["kpe"]
    kpe_sw = jnp.concatenate([kpe[:, half:], kpe[:, :half]], axis=-1)
    ckvx = _place([(pieces["ckv"], LANES), (jnp.concatenate([kpe, kpe_sw], axis=-1), LANES)], 2 * LANES)
    w_in_packed = jnp.concatenate(
        [pieces["naq"], pieces["nak"], pieces["nav"], pieces["nag"], pieces["cq"], ckvx,
         pieces["mg"], pieces["gna"], pieces["gmla"]], axis=-1).astype(bf16)
    assert w_in_packed.shape[-1] == _W_IN_COLS

    w_uq = p["w_uq"][l].reshape(MLA_Q_LORA, MLA_HEADS, MLA_QK_DIM)
    nope, x1, x2 = w_uq[..., :MLA_NOPE], w_uq[..., MLA_NOPE:MLA_NOPE + half], w_uq[..., MLA_NOPE + half:]
    zq = jnp.zeros_like(nope)
    wq = _place([(jnp.concatenate([nope, x1, x2], -1), HEAD_PAD)], HEAD_PAD)
    wqs = _place([(jnp.concatenate([zq, x2, x1], -1), HEAD_PAD)], HEAD_PAD)
    wq = wq.reshape(MLA_Q_LORA, MLA_PAD_WIDTH).astype(bf16)
    wqs = wqs.reshape(MLA_Q_LORA, MLA_PAD_WIDTH).astype(bf16)

    w_ukv = p["w_ukv"][l].reshape(MLA_KV_LORA, MLA_HEADS, MLA_NOPE + MLA_V)
    k_nope = _place([(w_ukv[..., :MLA_NOPE], HEAD_PAD)], HEAD_PAD).reshape(MLA_KV_LORA, MLA_PAD_WIDTH)
    eye = jnp.eye(MLA_ROPE, dtype=f32)
    put = jnp.pad(eye, ((0, 0), (MLA_NOPE, HEAD_PAD - MLA_QK_DIM)))
    put = jnp.tile(put, (1, MLA_HEADS))
    wk = jnp.concatenate([k_nope, put, put, jnp.zeros((2 * LANES - MLA_KV_LORA - 2 * MLA_ROPE,
                                                        MLA_PAD_WIDTH), f32)], axis=0).astype(bf16)
    wv = _place([(w_ukv[..., MLA_NOPE:], HEAD_PAD)], HEAD_PAD).reshape(MLA_KV_LORA, MLA_PAD_WIDTH).astype(bf16)

    gq = p["mla_q_norm"][l]
    gk = p["mla_k_norm"][l]
    scale = MLA_QK_DIM ** -0.5 * LOG2E
    seq_len = cos.shape[0]
    ones = jnp.ones((seq_len, 1), f32)
    tq1 = _place([(ones * gq[None, :MLA_NOPE], MLA_NOPE),
                  (cos * gq[None, MLA_NOPE:MLA_NOPE + half], half),
                  (cos * gq[None, MLA_NOPE + half:], half)], MLA_QK_DIM) * scale
    tq2 = _place([(jnp.zeros((seq_len, MLA_NOPE), f32), MLA_NOPE),
                  (-sin * gq[None, MLA_NOPE + half:], half),
                  (sin * gq[None, MLA_NOPE:MLA_NOPE + half], half)], MLA_QK_DIM) * scale
    e1 = jnp.concatenate([cos * gk[None, MLA_NOPE:MLA_NOPE + half], cos * gk[None, MLA_NOPE + half:]], -1)
    e2 = jnp.concatenate([-sin * gk[None, MLA_NOPE + half:], sin * gk[None, MLA_NOPE:MLA_NOPE + half]], -1)
    tabs = {"tq1": _place([(tq1, LANES)], LANES), "tq2": _place([(tq2, LANES)], LANES),
            "e1": _place([(e1, LANES)], LANES), "e2": _place([(e2, LANES)], LANES)}
    gkf = _place([(gk[None, :MLA_NOPE], MLA_NOPE), (jnp.ones((1, MLA_ROPE), f32), LANES - MLA_NOPE)], LANES)

    na_scale = NA_HEAD_DIM ** -0.5 * LOG2E
    lp = {
        "ln_g": p["ln_g"][l][None, :],
        "w_in": w_in_packed,
        "gq2": jnp.tile(p["na_q_norm"][l], 2)[None, :] * na_scale,
        "gk2": jnp.tile(p["na_k_norm"][l], 2)[None, :],
        "gcq": p["mla_cq_norm"][l][None, :],
        "gckv": p["mla_ckv_norm"][l][None, :],
        "wq": wq, "wqs": wqs, "wk": wk, "wv": wv, "gkf": gkf,
        "w_o_na": p["w_o_na"][l].astype(bf16),
        "w_o_mla": p["w_o_mla"][l].astype(bf16),
        "w_out": p["w_out"][l].astype(bf16),
    }
    return lp, tabs


@jax.jit
def _forward(x, p):
    bsz, seq_len, _ = x.shape
    depth = p["w_in"].shape[0]
    cos, sin = _rope_tables(seq_len)
    onehot, col_mask, tbl_idx = _na_table_static(seq_len)
    x2d = x.reshape(bsz * seq_len, D_MODEL)
    for l in range(depth):
        lp, tabs = _layer_params(l, p, cos, sin)
        tbl = _na_table(p["na_rel_bias"][l], onehot, col_mask, tbl_idx)
        qa, ka, va, nag, qf, kf, vb, mg, gna, gmla = _proj_call(x2d, lp, tabs, seq_len)

        def b3(a):
            return a.reshape(bsz, seq_len, a.shape[-1])

        a_na = _natt_call(b3(qa), b3(ka), b3(va), b3(nag), tbl)
        a_mla = _mla_call(b3(qf), b3(kf), b3(vb), b3(mg))
        x2d = _out_call(x2d, a_na.reshape(-1, NA_WIDTH), a_mla.reshape(-1, MLA_WIDTH), gna, gmla, lp)
    return x2d.reshape(bsz, seq_len, D_MODEL)


def kernel(x, ln_g, w_in, na_q_norm, na_k_norm, na_rel_bias, mla_cq_norm, mla_ckv_norm, w_uq, w_ukv,
           mla_q_norm, mla_k_norm, w_o_na, w_o_mla, w_out):
    params = dict(ln_g=ln_g, w_in=w_in, na_q_norm=na_q_norm, na_k_norm=na_k_norm,
                  na_rel_bias=na_rel_bias, mla_cq_norm=mla_cq_norm, mla_ckv_norm=mla_ckv_norm,
                  w_uq=w_uq, w_ukv=w_ukv, mla_q_norm=mla_q_norm, mla_k_norm=mla_k_norm,
                  w_o_na=w_o_na, w_o_mla=w_o_mla, w_out=w_out)
    return _forward(x, params)
```

```python
import functools

import numpy as np
import jax
import jax.numpy as jnp
from jax import lax
from jax.experimental import pallas as pl
from jax.experimental.pallas import tpu as pltpu

D_MODEL = 1024
GRID_W = 64
NA_HEADS = 8
NA_HEAD_DIM = 64
NA_WIDTH = NA_HEADS * NA_HEAD_DIM
NA_WIN_ROWS = 8
NA_WIN_COLS = 16
MLA_HEADS = 8
MLA_NOPE = 64
MLA_ROPE = 32
MLA_QK_DIM = MLA_NOPE + MLA_ROPE
MLA_V = 64
MLA_Q_LORA = 256
MLA_KV_LORA = 128
MLA_WIDTH = MLA_HEADS * MLA_V
ROPE_BASE = 10000.0
EPS = 1e-6

LANES = 128
HEAD_PAD = LANES
MLA_PAD_WIDTH = MLA_HEADS * HEAD_PAD
VMEM_LIMIT_BYTES = 56 * 1024 * 1024

_OFF_NAQ = 0
_OFF_NAK = _OFF_NAQ + NA_WIDTH
_OFF_NAV = _OFF_NAK + NA_WIDTH
_OFF_NAG = _OFF_NAV + NA_WIDTH
_OFF_CQ = _OFF_NAG + NA_WIDTH
_OFF_CKV = _OFF_CQ + MLA_Q_LORA
_OFF_MG = _OFF_CKV + 2 * LANES
_OFF_GNA = _OFF_MG + MLA_WIDTH
_OFF_GMLA = _OFF_GNA + D_MODEL
_W_IN_COLS = _OFF_GMLA + D_MODEL

PROJ_TILE = 512
PROJ_BLK = 256
NA_QROWS = 4
NA_BAND = 12
NA_TQ = NA_QROWS * GRID_W
NA_TK = NA_BAND * GRID_W
NA_KC = 768
MLA_TQ = 256
MLA_KC = 512
MLA_GROUP = 8
MASK_VALUE = -1e30
LOG2E = 1.4426950408889634

_NT_DIMS = (((1,), (1,)), ((), ()))


def _dot(a, b):
    return jnp.dot(a, b, preferred_element_type=jnp.float32)


def _dot_nt(a, b):
    return lax.dot_general(a, b, _NT_DIMS, preferred_element_type=jnp.float32)


def _lane_iota():
    return lax.broadcasted_iota(jnp.int32, (1, LANES), 1)


def _proj_kernel(x_ref, lng_ref, w_ref, gq_ref, gk_ref, gcq_ref, gckv_ref, wq_ref, wqs_ref,
                 wk_ref, wvt_ref, gkf_ref, tq1_ref, tq2_ref, e1_ref, e2_ref,
                 qa_ref, ka_ref, va_ref, nag_ref, qf_ref, kf_ref, mg_ref,
                 gna_ref, gmla_ref, vbt_ref, hb_scr, cqn_scr, lhsk_scr, sspe_scr):
    bf16 = jnp.bfloat16
    f32 = jnp.float32
    x = x_ref[...]
    hb_scr[...] = (x * lax.rsqrt(jnp.mean(x * x, axis=-1, keepdims=True) + EPS) * lng_ref[...]).astype(bf16)
    lo_half = _lane_iota() < NA_HEAD_DIM
    rope_lanes = _lane_iota() < MLA_ROPE
    ones_row = lax.broadcasted_iota(jnp.int32, (HEAD_PAD, 1), 0) == MLA_V

    def tiles(y):
        return [y[:, i * LANES:(i + 1) * LANES] for i in range(y.shape[1] // LANES)]

    def in_dot(off):
        return lambda: _dot(hb_scr[...], w_ref[:, off:off + PROJ_BLK])

    def headnorm64(out_ref, g_ref, col0):
        def post(y):
            for i, pc in enumerate(tiles(y)):
                sq = pc * pc
                s_lo = jnp.sum(jnp.where(lo_half, sq, 0.0), axis=-1, keepdims=True)
                s_hi = jnp.sum(jnp.where(lo_half, 0.0, sq), axis=-1, keepdims=True)
                r = jnp.where(lo_half,
                              lax.rsqrt(s_lo * (1.0 / NA_HEAD_DIM) + EPS),
                              lax.rsqrt(s_hi * (1.0 / NA_HEAD_DIM) + EPS))
                out_ref[:, col0 + i * LANES:col0 + (i + 1) * LANES] = (pc * r * g_ref[...]).astype(bf16)
        return post

    def elementwise(out_ref, col0, fn):
        def post(y):
            out_ref[:, col0:col0 + PROJ_BLK] = fn(y).astype(bf16)
        return post

    def silu(g):
        return g * jax.nn.sigmoid(g)

    def post_cq(y):
        cqn_scr[...] = (y * lax.rsqrt(jnp.mean(y * y, axis=-1, keepdims=True) + EPS) * gcq_ref[...]).astype(bf16)

    def post_ckv(y):
        ckv, e = tiles(y)
        lhsk_scr[:, :LANES] = (ckv * lax.rsqrt(jnp.mean(ckv * ckv, axis=-1, keepdims=True) + EPS)
                               * gckv_ref[...]).astype(bf16)
        ss_pe = jnp.sum(jnp.where(rope_lanes, e * e, 0.0), axis=-1, keepdims=True)
        sspe_scr[...] = jnp.broadcast_to(ss_pe, sspe_scr.shape)
        rot = e * e1_ref[...] + pltpu.roll(e, LANES - MLA_ROPE, axis=1) * e2_ref[...]
        rot_hi = rot.astype(bf16).astype(f32)
        rot_lo = (rot - rot_hi).astype(bf16).astype(f32)
        lhsk_scr[:, LANES:] = (rot_hi + pltpu.roll(rot_lo, MLA_ROPE, axis=1)).astype(bf16)

    def q_dot(pair):
        cols = slice(pair * PROJ_BLK, (pair + 1) * PROJ_BLK)
        return lambda: (_dot(cqn_scr[...], wq_ref[:, cols]), _dot(cqn_scr[...], wqs_ref[:, cols]))

    def q_post(pair):
        def post(ys):
            for i, (a, b) in enumerate(zip(tiles(ys[0]), tiles(ys[1]))):
                r = lax.rsqrt(jnp.sum(a * a, axis=-1, keepdims=True) / MLA_QK_DIM + EPS)
                col0 = pair * PROJ_BLK + i * HEAD_PAD
                qf_ref[:, col0:col0 + HEAD_PAD] = ((a * tq1_ref[...] + b * tq2_ref[...]) * r).astype(bf16)
        return post

    def k_dot(pair):
        cols = slice(pair * PROJ_BLK, (pair + 1) * PROJ_BLK)
        return lambda: _dot(lhsk_scr[...], wk_ref[:, cols])

    def k_post(pair):
        def post(y):
            for i, a in enumerate(tiles(y)):
                ss = jnp.sum(jnp.where(lo_half, a * a, 0.0), axis=-1, keepdims=True) + sspe_scr[...]
                r = lax.rsqrt(ss / MLA_QK_DIM + EPS)
                col0 = pair * PROJ_BLK + i * HEAD_PAD
                kf_ref[:, col0:col0 + HEAD_PAD] = (a * gkf_ref[...] * r).astype(bf16)
        return post

    def v_dot(pair):
        rows = slice(pair * PROJ_BLK, (pair + 1) * PROJ_BLK)
        return lambda: _dot_nt(wvt_ref[rows, :], lhsk_scr[:, :LANES])

    def v_post(pair):
        def post(y):
            for i in range(PROJ_BLK // HEAD_PAD):
                row0 = pair * PROJ_BLK + i * HEAD_PAD
                a = y[i * HEAD_PAD:(i + 1) * HEAD_PAD, :]
                vbt_ref[0, row0:row0 + HEAD_PAD, :] = jnp.where(ones_row, 1.0, a).astype(bf16)
        return post

    def in_blocks(off, width, make_post):
        return [(in_dot(off + c), make_post(c)) for c in range(0, width, PROJ_BLK)]

    tasks = [(in_dot(_OFF_CQ), post_cq), (in_dot(_OFF_CKV), post_ckv)]
    tasks += in_blocks(_OFF_NAQ, NA_WIDTH, lambda c: headnorm64(qa_ref, gq_ref, c))
    tasks += in_blocks(_OFF_NAK, NA_WIDTH, lambda c: headnorm64(ka_ref, gk_ref, c))
    tasks += [(q_dot(p), q_post(p)) for p in range(MLA_PAD_WIDTH // PROJ_BLK)]
    tasks += in_blocks(_OFF_NAV, NA_WIDTH, lambda c: elementwise(va_ref, c, lambda y: y))
    tasks += [(k_dot(p), k_post(p)) for p in range(MLA_PAD_WIDTH // PROJ_BLK)]
    tasks += in_blocks(_OFF_NAG, NA_WIDTH, lambda c: elementwise(nag_ref, c, silu))
    tasks += [(v_dot(p), v_post(p)) for p in range(MLA_PAD_WIDTH // PROJ_BLK)]
    tasks += in_blocks(_OFF_MG, MLA_WIDTH, lambda c: elementwise(mg_ref, c, silu))
    tasks += in_blocks(_OFF_GNA, D_MODEL, lambda c: elementwise(gna_ref, c, jax.nn.sigmoid))
    tasks += in_blocks(_OFF_GMLA, D_MODEL, lambda c: elementwise(gmla_ref, c, jax.nn.sigmoid))

    pending = None
    for dot_fn, post_fn in tasks:
        y = dot_fn()
        if pending is not None:
            pending[0](pending[1])
        pending = (post_fn, y)
    pending[0](pending[1])


def _const_spec(shape):
    nd = len(shape)
    return pl.BlockSpec(shape, lambda i: (0,) * nd, pipeline_mode=pl.Buffered(1))


def _proj_call(x2d, lp, tabs, seq_len):
    n_tok = x2d.shape[0]
    tm = PROJ_TILE
    assert n_tok % tm == 0 and seq_len % tm == 0
    tiles_per_seq = seq_len // tm
    bf16 = jnp.bfloat16

    def row_spec(width):
        return pl.BlockSpec((tm, width), lambda i: (i, 0))

    def tab_spec():
        return pl.BlockSpec((tm, LANES), lambda i: (i % tiles_per_seq, 0))

    consts = [lp["ln_g"], lp["w_in"], lp["gq2"], lp["gk2"], lp["gcq"], lp["gckv"],
              lp["wq"], lp["wqs"], lp["wk"], lp["wvt"], lp["gkf"]]
    out_widths = [NA_WIDTH, NA_WIDTH, NA_WIDTH, NA_WIDTH, MLA_PAD_WIDTH, MLA_PAD_WIDTH,
                  MLA_WIDTH, D_MODEL, D_MODEL]
    vbt_spec = pl.BlockSpec((1, MLA_PAD_WIDTH, tm), lambda i: (i // tiles_per_seq, 0, i % tiles_per_seq))
    vbt_shape = jax.ShapeDtypeStruct((n_tok // seq_len, MLA_PAD_WIDTH, seq_len), bf16)
    return pl.pallas_call(
        _proj_kernel,
        grid=(n_tok // tm,),
        in_specs=[row_spec(D_MODEL)] + [_const_spec(c.shape) for c in consts] + [tab_spec()] * 4,
        out_specs=[row_spec(w) for w in out_widths] + [vbt_spec],
        out_shape=[jax.ShapeDtypeStruct((n_tok, w), bf16) for w in out_widths] + [vbt_shape],
        scratch_shapes=[pltpu.VMEM((tm, D_MODEL), bf16),
                        pltpu.VMEM((tm, MLA_Q_LORA), bf16),
                        pltpu.VMEM((tm, 2 * LANES), bf16),
                        pltpu.VMEM((tm, LANES), jnp.float32)],
        compiler_params=pltpu.CompilerParams(
            dimension_semantics=("arbitrary",), vmem_limit_bytes=VMEM_LIMIT_BYTES),
        name="proj",
    )(x2d, *consts, tabs["tq1"], tabs["tq2"], tabs["e1"], tabs["e2"])


def _na_band_start(i, rows):
    return jnp.clip(NA_QROWS * i - NA_WIN_ROWS // 2, 0, rows - NA_BAND)


def _natt_kernel(q_ref, k_ref, v_ref, gate_ref, tbl_ref, o_ref, s_scr, p_scr):
    bf16 = jnp.bfloat16
    i = pl.program_id(1)
    start = pl.multiple_of(_na_band_start(i, k_ref.shape[1] // GRID_W) * GRID_W, GRID_W)
    lo_half = _lane_iota() < NA_HEAD_DIM
    n_chunks = NA_TK // NA_KC

    def pcols(head):
        return slice((head // 2) * LANES, (head // 2 + 1) * LANES)

    def kslice(c):
        return slice(c * NA_KC, (c + 1) * NA_KC)

    def fold(x, run, op):
        for t in range(x.shape[1] // LANES):
            tile = x[:, t * LANES:(t + 1) * LANES]
            run = tile if run is None else op(run, tile)
        return run

    def masked_q(head):
        qp = q_ref[0, :, pcols(head)]
        sel = lo_half if head % 2 == 0 else jnp.logical_not(lo_half)
        return jnp.where(sel, qp, jnp.zeros_like(qp))

    def qk_chunk(head, qm, c, run_max):
        keys = k_ref[0, pl.ds(start + c * NA_KC, NA_KC), pcols(head)]
        s = _dot_nt(qm, keys) + tbl_ref[0, head, :, kslice(c)]
        s_scr[head % 2, :, kslice(c)] = s
        return fold(s, run_max, jnp.maximum)

    def exp_chunk(head, c, m, run_sum):
        p = jnp.exp2(s_scr[head % 2, :, kslice(c)] - m)
        p_scr[head % 2, :, kslice(c)] = p.astype(bf16)
        return fold(p, run_sum, jnp.add)

    def pv_chunk(head, c, acc):
        vals = v_ref[0, pl.ds(start + c * NA_KC, NA_KC), pcols(head)]
        o = _dot(p_scr[head % 2, :, kslice(c)], vals)
        return o if acc is None else acc + o

    def finish(head, acc, run_sum, prev):
        o = acc / jnp.sum(run_sum, axis=-1, keepdims=True)
        if head % 2 == 0:
            return o
        pair = jnp.where(lo_half, prev, o)
        o_ref[0, :, pcols(head)] = (pair * gate_ref[0, :, pcols(head)].astype(jnp.float32)).astype(bf16)
        return None

    qm = masked_q(0)
    run = None
    for c in range(n_chunks):
        run = qk_chunk(0, qm, c, run)
    m = jnp.max(run, axis=-1, keepdims=True)
    prev_sum, even_out = None, None
    for head in range(NA_HEADS):
        run, acc, cur_sum = None, None, None
        if head + 1 < NA_HEADS:
            qm = masked_q(head + 1)
        for c in range(n_chunks):
            if head + 1 < NA_HEADS:
                run = qk_chunk(head + 1, qm, c, run)
            if head >= 1:
                acc = pv_chunk(head - 1, c, acc)
            cur_sum = exp_chunk(head, c, m, cur_sum)
        if head >= 1:
            even_out = finish(head - 1, acc, prev_sum, even_out)
        prev_sum = cur_sum
        if head + 1 < NA_HEADS:
            m = jnp.max(run, axis=-1, keepdims=True)
    acc = None
    for c in range(n_chunks):
        acc = pv_chunk(NA_HEADS - 1, c, acc)
    finish(NA_HEADS - 1, acc, prev_sum, even_out)


def _natt_call(qa, ka, va, nag, tbl):
    bsz, seq_len, _ = qa.shape
    n_blk = seq_len // NA_TQ

    def pattern(i):
        return jnp.minimum(i, 1) + (i == n_blk - 1).astype(jnp.int32)

    blk = pl.BlockSpec((1, NA_TQ, NA_WIDTH), lambda b, i: (b, i, 0))
    full = pl.BlockSpec((1, seq_len, NA_WIDTH), lambda b, i: (b, 0, 0))
    return pl.pallas_call(
        _natt_kernel,
        grid=(bsz, n_blk),
        in_specs=[blk, full, full, blk,
                  pl.BlockSpec((1, NA_HEADS, NA_TQ, NA_TK), lambda b, i: (pattern(i), 0, 0, 0))],
        out_specs=blk,
        out_shape=jax.ShapeDtypeStruct((bsz, seq_len, NA_WIDTH), jnp.bfloat16),
        scratch_shapes=[pltpu.VMEM((2, NA_TQ, NA_TK), jnp.float32),
                        pltpu.VMEM((2, NA_TQ, NA_TK), jnp.bfloat16)],
        compiler_params=pltpu.CompilerParams(
            dimension_semantics=("arbitrary", "arbitrary"), vmem_limit_bytes=VMEM_LIMIT_BYTES),
        name="natt",
    )(qa, ka, va, nag, tbl)


def _mla_kernel(q_ref, k_ref, vt_ref, gate_ref, o_ref, s_scr, p_scr):
    bf16 = jnp.bfloat16
    seq_len = k_ref.shape[1]
    n_chunks = seq_len // MLA_KC
    n_heads = q_ref.shape[2] // HEAD_PAD
    lo_half = _lane_iota() < MLA_V
    sublanes = 8

    def hcols(head):
        return slice(head * HEAD_PAD, (head + 1) * HEAD_PAD)

    def kslice(c):
        return slice(c * MLA_KC, (c + 1) * MLA_KC)

    def qk_chunk(head, c, run_max):
        s = _dot_nt(k_ref[0, kslice(c), hcols(head)], q_ref[0, :, hcols(head)])
        s_scr[head % 2, kslice(c), :] = s
        part = jnp.max(s.reshape(MLA_KC // sublanes, sublanes, MLA_TQ), axis=0)
        return part if run_max is None else jnp.maximum(run_max, part)

    def exp_chunk(head, c, m):
        p_scr[head % 2, kslice(c), :] = jnp.exp2(s_scr[head % 2, kslice(c), :] - m).astype(bf16)

    def pv_chunk(head, c, acc):
        o = _dot(vt_ref[0, hcols(head), kslice(c)], p_scr[head % 2, kslice(c), :])
        return o if acc is None else acc + o

    def finish(head, acc, prev):
        acc = acc.T
        o = acc / acc[:, MLA_V:MLA_V + 1]
        if head % 2 == 0:
            return o
        cols = slice((head // 2) * LANES, (head // 2 + 1) * LANES)
        pair = jnp.where(lo_half, prev, pltpu.roll(o, MLA_V, axis=1))
        o_ref[0, :, cols] = (pair * gate_ref[0, :, cols].astype(jnp.float32)).astype(bf16)
        return None

    run = None
    for c in range(n_chunks):
        run = qk_chunk(0, c, run)
    m = jnp.max(run, axis=0, keepdims=True)
    acc, even_out = None, None
    for head in range(n_heads):
        run, next_acc = None, None
        for c in range(n_chunks):
            if head + 1 < n_heads:
                run = qk_chunk(head + 1, c, run)
            if head >= 1:
                next_acc = pv_chunk(head - 1, c, next_acc)
            exp_chunk(head, c, m)
        if head >= 1:
            even_out = finish(head - 1, next_acc, even_out)
        if head + 1 < n_heads:
            m = jnp.max(run, axis=0, keepdims=True)
    for c in range(n_chunks):
        acc = pv_chunk(n_heads - 1, c, acc)
    finish(n_heads - 1, acc, even_out)


def _mla_call(qf, kf, vbt, mg):
    bsz, seq_len, _ = qf.shape
    n_groups = MLA_HEADS // MLA_GROUP
    assert seq_len % MLA_KC == 0 and seq_len % MLA_TQ == 0 and MLA_GROUP % 2 == 0
    return pl.pallas_call(
        _mla_kernel,
        grid=(bsz, n_groups, seq_len // MLA_TQ),
        in_specs=[pl.BlockSpec((1, MLA_TQ, MLA_GROUP * HEAD_PAD), lambda b, p, j: (b, j, p)),
                  pl.BlockSpec((1, seq_len, MLA_GROUP * HEAD_PAD), lambda b, p, j: (b, 0, p)),
                  pl.BlockSpec((1, MLA_GROUP * HEAD_PAD, seq_len), lambda b, p, j: (b, p, 0)),
                  pl.BlockSpec((1, MLA_TQ, MLA_GROUP * MLA_V), lambda b, p, j: (b, j, p))],
        out_specs=pl.BlockSpec((1, MLA_TQ, MLA_GROUP * MLA_V), lambda b, p, j: (b, j, p)),
        out_shape=jax.ShapeDtypeStruct((bsz, seq_len, MLA_WIDTH), jnp.bfloat16),
        scratch_shapes=[pltpu.VMEM((2, seq_len, MLA_TQ), jnp.float32),
                        pltpu.VMEM((2, seq_len, MLA_TQ), jnp.bfloat16)],
        compiler_params=pltpu.CompilerParams(
            dimension_semantics=("arbitrary", "arbitrary", "arbitrary"),
            vmem_limit_bytes=VMEM_LIMIT_BYTES),
        name="mla",
    )(qf, kf, vbt, mg)


def _out_kernel(x_ref, ana_ref, amla_ref, gna_ref, gmla_ref, wna_ref, wmla_ref, wout_ref, o_ref, y_scr):
    f32 = jnp.float32
    n_blk = o_ref.shape[1] // PROJ_BLK

    def branch_dots(c):
        cols = slice(c * PROJ_BLK, (c + 1) * PROJ_BLK)
        return _dot(ana_ref[...], wna_ref[:, cols]), _dot(amla_ref[...], wmla_ref[:, cols])

    def merge(c, us):
        cols = slice(c * PROJ_BLK, (c + 1) * PROJ_BLK)
        y = gna_ref[:, cols].astype(f32) * us[0] + gmla_ref[:, cols].astype(f32) * us[1]
        y_scr[:, cols] = y.astype(jnp.bfloat16)

    us = branch_dots(0)
    for c in range(n_blk):
        nxt = branch_dots(c + 1) if c + 1 < n_blk else None
        merge(c, us)
        us = nxt
    y = None
    for c in range(n_blk):
        cols = slice(c * PROJ_BLK, (c + 1) * PROJ_BLK)
        nxt = _dot(y_scr[...], wout_ref[:, cols])
        if y is not None:
            pc = slice((c - 1) * PROJ_BLK, c * PROJ_BLK)
            o_ref[:, pc] = x_ref[:, pc] + y
        y = nxt
    pc = slice((n_blk - 1) * PROJ_BLK, n_blk * PROJ_BLK)
    o_ref[:, pc] = x_ref[:, pc] + y


def _out_call(x2d, a_na, a_mla, gna, gmla, lp):
    n_tok = x2d.shape[0]
    tm = PROJ_TILE

    def row_spec(width):
        return pl.BlockSpec((tm, width), lambda i: (i, 0))

    consts = [lp["w_o_na"], lp["w_o_mla"], lp["w_out"]]
    return pl.pallas_call(
        _out_kernel,
        grid=(n_tok // tm,),
        in_specs=[row_spec(D_MODEL), row_spec(NA_WIDTH), row_spec(MLA_WIDTH), row_spec(D_MODEL),
                  row_spec(D_MODEL)] + [_const_spec(c.shape) for c in consts],
        out_specs=row_spec(D_MODEL),
        out_shape=jax.ShapeDtypeStruct((n_tok, D_MODEL), jnp.float32),
        scratch_shapes=[pltpu.VMEM((tm, D_MODEL), jnp.bfloat16)],
        compiler_params=pltpu.CompilerParams(
            dimension_semantics=("arbitrary",), vmem_limit_bytes=VMEM_LIMIT_BYTES),
        name="out",
    )(x2d, a_na, a_mla, gna, gmla, *consts)


_N_DROW = 2 * NA_WIN_ROWS - 1
_N_DCOL = 2 * NA_WIN_COLS - 1


def _na_table_static(seq_len):
    rows = seq_len // GRID_W
    n_blk = rows // NA_QROWS
    qc = np.arange(GRID_W)
    kj = np.arange(GRID_W)
    cs = np.clip(qc - NA_WIN_COLS // 2, 0, GRID_W - NA_WIN_COLS)
    col_valid = (kj[None, :] >= cs[:, None]) & (kj[None, :] < cs[:, None] + NA_WIN_COLS)
    dcol = kj[None, :] - qc[:, None] + (NA_WIN_COLS - 1)
    onehot = (dcol[None] == np.arange(_N_DCOL)[:, None, None]) & col_valid[None]
    col_mask = np.where(col_valid, 0.0, MASK_VALUE).astype(np.float32)
    qr = np.arange(NA_QROWS)
    kn = np.arange(NA_BAND)
    idx = []
    for i in (0, 1, n_blk - 1):
        band = int(np.clip(NA_QROWS * i - NA_WIN_ROWS // 2, 0, rows - NA_BAND))
        r = NA_QROWS * i + qr
        rs = np.clip(r - NA_WIN_ROWS // 2, 0, rows - NA_WIN_ROWS)
        krow = band + kn
        valid = (krow[None, :] >= rs[:, None]) & (krow[None, :] < rs[:, None] + NA_WIN_ROWS)
        drow = krow[None, :] - r[:, None] + (NA_WIN_ROWS - 1)
        idx.append(np.where(valid, drow, _N_DROW))
    return onehot.astype(np.float32), col_mask, np.stack(idx).astype(np.int32)


def _na_table(rel_bias, onehot, col_mask, idx):
    blocks = jnp.einsum("hdc,cqj->hdqj", rel_bias, onehot, precision=lax.Precision.HIGHEST) + col_mask
    blocks = jnp.concatenate(
        [blocks, jnp.full((NA_HEADS, 1, GRID_W, GRID_W), MASK_VALUE, jnp.float32)], axis=1) * LOG2E
    tbl = jnp.take(blocks, idx.reshape(-1), axis=1)
    tbl = tbl.reshape(NA_HEADS, 3, NA_QROWS, NA_BAND, GRID_W, GRID_W)
    return tbl.transpose(1, 0, 2, 4, 3, 5).reshape(3, NA_HEADS, NA_TQ, NA_TK)


def _rope_tables(seq_len):
    t = jnp.arange(seq_len)
    row = (t // GRID_W).astype(jnp.float32)
    col = (t % GRID_W).astype(jnp.float32)
    half = MLA_ROPE // 2
    n_freq = half // 2
    inv = jnp.power(jnp.float32(ROPE_BASE), -jnp.arange(n_freq, dtype=jnp.float32) / n_freq)
    ang = jnp.concatenate([row[:, None] * inv, col[:, None] * inv], axis=-1)
    return jnp.cos(ang), jnp.sin(ang)


def _place(cols_to_blocks, total):
    parts = []
    for arr, width in cols_to_blocks:
        pad = width - arr.shape[-1]
        parts.append(jnp.pad(arr, [(0, 0)] * (arr.ndim - 1) + [(0, pad)]) if pad else arr)
    out = jnp.concatenate(parts, axis=-1)
    assert out.shape[-1] == total
    return out


def _layer_params(l, p, cos, sin):
    f32, bf16 = jnp.float32, jnp.bfloat16
    half = MLA_ROPE // 2
    w_in = p["w_in"][l]
    o = 0
    pieces = {}
    for name, size in (("naq", NA_WIDTH), ("nak", NA_WIDTH), ("nav", NA_WIDTH), ("nag", NA_WIDTH),
                       ("cq", MLA_Q_LORA), ("ckv", MLA_KV_LORA), ("kpe", MLA_ROPE),
                       ("mg", MLA_WIDTH), ("gna", D_MODEL), ("gmla", D_MODEL)):
        pieces[name] = w_in[:, o:o + size]
        o += size
    kpe = pieces["kpe"]
    kpe_sw = jnp.concatenate([kpe[:, half:], kpe[:, :half]], axis=-1)
    ckvx = _place([(pieces["ckv"], LANES), (jnp.concatenate([kpe, kpe_sw], axis=-1), LANES)], 2 * LANES)
    w_in_packed = jnp.concatenate(
        [pieces["naq"], pieces["nak"], pieces["nav"], pieces["nag"], pieces["cq"], ckvx,
         pieces["mg"], pieces["gna"], pieces["gmla"]], axis=-1).astype(bf16)
    assert w_in_packed.shape[-1] == _W_IN_COLS

    w_uq = p["w_uq"][l].reshape(MLA_Q_LORA, MLA_HEADS, MLA_QK_DIM)
    nope, x1, x2 = w_uq[..., :MLA_NOPE], w_uq[..., MLA_NOPE:MLA_NOPE + half], w_uq[..., MLA_NOPE + half:]
    zq = jnp.zeros_like(nope)
    wq = _place([(jnp.concatenate([nope, x1, x2], -1), HEAD_PAD)], HEAD_PAD)
    wqs = _place([(jnp.concatenate([zq, x2, x1], -1), HEAD_PAD)], HEAD_PAD)
    wq = wq.reshape(MLA_Q_LORA, MLA_PAD_WIDTH).astype(bf16)
    wqs = wqs.reshape(MLA_Q_LORA, MLA_PAD_WIDTH).astype(bf16)

    w_ukv = p["w_ukv"][l].reshape(MLA_KV_LORA, MLA_HEADS, MLA_NOPE + MLA_V)
    k_nope = _place([(w_ukv[..., :MLA_NOPE], HEAD_PAD)], HEAD_PAD).reshape(MLA_KV_LORA, MLA_PAD_WIDTH)
    eye = jnp.eye(MLA_ROPE, dtype=f32)
    put = jnp.pad(eye, ((0, 0), (MLA_NOPE, HEAD_PAD - MLA_QK_DIM)))
    put = jnp.tile(put, (1, MLA_HEADS))
    wk = jnp.concatenate([k_nope, put, put, jnp.zeros((2 * LANES - MLA_KV_LORA - 2 * MLA_ROPE,
                                                        MLA_PAD_WIDTH), f32)], axis=0).astype(bf16)
    wvt = _place([(w_ukv[..., MLA_NOPE:], HEAD_PAD)], HEAD_PAD).reshape(MLA_KV_LORA, MLA_PAD_WIDTH).T.astype(bf16)

    gq = p["mla_q_norm"][l]
    gk = p["mla_k_norm"][l]
    scale = MLA_QK_DIM ** -0.5 * LOG2E
    seq_len = cos.shape[0]
    ones = jnp.ones((seq_len, 1), f32)
    tq1 = _place([(ones * gq[None, :MLA_NOPE], MLA_NOPE),
                  (cos * gq[None, MLA_NOPE:MLA_NOPE + half], half),
                  (cos * gq[None, MLA_NOPE + half:], half)], MLA_QK_DIM) * scale
    tq2 = _place([(jnp.zeros((seq_len, MLA_NOPE), f32), MLA_NOPE),
                  (-sin * gq[None, MLA_NOPE + half:], half),
                  (sin * gq[None, MLA_NOPE:MLA_NOPE + half], half)], MLA_QK_DIM) * scale
    e1 = jnp.concatenate([cos * gk[None, MLA_NOPE:MLA_NOPE + half], cos * gk[None, MLA_NOPE + half:]], -1)
    e2 = jnp.concatenate([-sin * gk[None, MLA_NOPE + half:], sin * gk[None, MLA_NOPE:MLA_NOPE + half]], -1)
    tabs = {"tq1": _place([(tq1, LANES)], LANES), "tq2": _place([(tq2, LANES)], LANES),
            "e1": _place([(e1, LANES)], LANES), "e2": _place([(e2, LANES)], LANES)}
    gkf = _place([(gk[None, :MLA_NOPE], MLA_NOPE), (jnp.ones((1, MLA_ROPE), f32), LANES - MLA_NOPE)], LANES)

    na_scale = NA_HEAD_DIM ** -0.5 * LOG2E
    lp = {
        "ln_g": p["ln_g"][l][None, :],
        "w_in": w_in_packed,
        "gq2": jnp.tile(p["na_q_norm"][l], 2)[None, :] * na_scale,
        "gk2": jnp.tile(p["na_k_norm"][l], 2)[None, :],
        "gcq": p["mla_cq_norm"][l][None, :],
        "gckv": p["mla_ckv_norm"][l][None, :],
        "wq": wq, "wqs": wqs, "wk": wk, "wvt": wvt, "gkf": gkf,
        "w_o_na": p["w_o_na"][l].astype(bf16),
        "w_o_mla": p["w_o_mla"][l].astype(bf16),
        "w_out": p["w_out"][l].astype(bf16),
    }
    return lp, tabs


@jax.jit
def _forward(x, p):
    bsz, seq_len, _ = x.shape
    depth = p["w_in"].shape[0]
    cos, sin = _rope_tables(seq_len)
    onehot, col_mask, tbl_idx = _na_table_static(seq_len)
    x2d = x.reshape(bsz * seq_len, D_MODEL)
    for l in range(depth):
        lp, tabs = _layer_params(l, p, cos, sin)
        tbl = _na_table(p["na_rel_bias"][l], onehot, col_mask, tbl_idx)
        qa, ka, va, nag, qf, kf, mg, gna, gmla, vbt = _proj_call(x2d, lp, tabs, seq_len)

        def b3(a):
            return a.reshape(bsz, seq_len, a.shape[-1])

        a_na = _natt_call(b3(qa), b3(ka), b3(va), b3(nag), tbl)
        a_mla = _mla_call(b3(qf), b3(kf), vbt, b3(mg))
        x2d = _out_call(x2d, a_na.reshape(-1, NA_WIDTH), a_mla.reshape(-1, MLA_WIDTH), gna, gmla, lp)
    return x2d.reshape(bsz, seq_len, D_MODEL)


def kernel(x, ln_g, w_in, na_q_norm, na_k_norm, na_rel_bias, mla_cq_norm, mla_ckv_norm, w_uq, w_ukv,
           mla_q_norm, mla_k_norm, w_o_na, w_o_mla, w_out):
    params = dict(ln_g=ln_g, w_in=w_in, na_q_norm=na_q_norm, na_k_norm=na_k_norm,
                  na_rel_bias=na_rel_bias, mla_cq_norm=mla_cq_norm, mla_ckv_norm=mla_ckv_norm,
                  w_uq=w_uq, w_ukv=w_ukv, mla_q_norm=mla_q_norm, mla_k_norm=mla_k_norm,
                  w_o_na=w_o_na, w_o_mla=w_o_mla, w_out=w_out)
    return _forward(x, params)
```

```python
import functools

import numpy as np
import jax
import jax.numpy as jnp
from jax import lax
from jax.experimental import pallas as pl
from jax.experimental.pallas import tpu as pltpu

D_MODEL = 1024
GRID_W = 64
NA_HEADS = 8
NA_HEAD_DIM = 64
NA_WIDTH = NA_HEADS * NA_HEAD_DIM
NA_WIN_ROWS = 8
NA_WIN_COLS = 16
MLA_HEADS = 8
MLA_NOPE = 64
MLA_ROPE = 32
MLA_QK_DIM = MLA_NOPE + MLA_ROPE
MLA_V = 64
MLA_Q_LORA = 256
MLA_KV_LORA = 128
MLA_WIDTH = MLA_HEADS * MLA_V
ROPE_BASE = 10000.0
EPS = 1e-6

LANES = 128
HEAD_PAD = LANES
MLA_PAD_WIDTH = MLA_HEADS * HEAD_PAD
VMEM_LIMIT_BYTES = 56 * 1024 * 1024

_OFF_NAQ = 0
_OFF_NAK = _OFF_NAQ + NA_WIDTH
_OFF_NAG = _OFF_NAK + NA_WIDTH
_OFF_CQ = _OFF_NAG + NA_WIDTH
_OFF_CKV = _OFF_CQ + MLA_Q_LORA
_OFF_MG = _OFF_CKV + 2 * LANES
_OFF_GNA = _OFF_MG + MLA_WIDTH
_OFF_GMLA = _OFF_GNA + D_MODEL
_W_IN_COLS = _OFF_GMLA + D_MODEL

PROJ_TILE = 512
OUT_TILE = 1024
PROJ_BLK = 256
NA_QROWS = 4
NA_BAND = 12
NA_TQ = NA_QROWS * GRID_W
NA_TK = NA_BAND * GRID_W
NA_KC = 256
MLA_TQ = 256
MLA_KC = 1024
MLA_GROUP = 8
MASK_VALUE = -1e30
LOG2E = 1.4426950408889634

_NT_DIMS = (((1,), (1,)), ((), ()))


def _dot(a, b):
    return jnp.dot(a, b, preferred_element_type=jnp.float32)


def _dot_nt(a, b):
    return lax.dot_general(a, b, _NT_DIMS, preferred_element_type=jnp.float32)


def _lane_iota():
    return lax.broadcasted_iota(jnp.int32, (1, LANES), 1)


def _proj_kernel(x_ref, lng_ref, w_ref, gq_ref, gk_ref, gcq_ref, gckv_ref, wq_ref, wqs_ref,
                 wk_ref, wvt_ref, wnavt_ref, gkf_ref, tq1_ref, tq2_ref, e1_ref, e2_ref,
                 qa_ref, ka_ref, nag_ref, qf_ref, kf_ref, mg_ref,
                 gna_ref, gmla_ref, vat_ref, vbt_ref, hb_scr, cqn_scr, lhsk_scr, sspe_scr):
    bf16 = jnp.bfloat16
    f32 = jnp.float32
    x = x_ref[...]
    hb_scr[...] = (x * lax.rsqrt(jnp.mean(x * x, axis=-1, keepdims=True) + EPS) * lng_ref[...]).astype(bf16)
    lo_half = _lane_iota() < NA_HEAD_DIM
    rope_lanes = _lane_iota() < MLA_ROPE
    ones_row = lax.broadcasted_iota(jnp.int32, (HEAD_PAD, 1), 0) == MLA_V

    def tiles(y):
        return [y[:, i * LANES:(i + 1) * LANES] for i in range(y.shape[1] // LANES)]

    def in_dot(off):
        return lambda: _dot(hb_scr[...], w_ref[:, off:off + PROJ_BLK])

    def headnorm64(out_ref, g_ref, col0):
        def post(y):
            for i, pc in enumerate(tiles(y)):
                sq = pc * pc
                s_lo = jnp.sum(jnp.where(lo_half, sq, 0.0), axis=-1, keepdims=True)
                s_hi = jnp.sum(jnp.where(lo_half, 0.0, sq), axis=-1, keepdims=True)
                r = jnp.where(lo_half,
                              lax.rsqrt(s_lo * (1.0 / NA_HEAD_DIM) + EPS),
                              lax.rsqrt(s_hi * (1.0 / NA_HEAD_DIM) + EPS))
                out_ref[:, col0 + i * LANES:col0 + (i + 1) * LANES] = (pc * r * g_ref[...]).astype(bf16)
        return post

    def elementwise(out_ref, col0, fn):
        def post(y):
            out_ref[:, col0:col0 + PROJ_BLK] = fn(y).astype(bf16)
        return post

    def silu(g):
        return g * jax.nn.sigmoid(g)

    def post_cq(y):
        cqn_scr[...] = (y * lax.rsqrt(jnp.mean(y * y, axis=-1, keepdims=True) + EPS) * gcq_ref[...]).astype(bf16)

    def post_ckv(y):
        ckv, e = tiles(y)
        lhsk_scr[:, :LANES] = (ckv * lax.rsqrt(jnp.mean(ckv * ckv, axis=-1, keepdims=True) + EPS)
                               * gckv_ref[...]).astype(bf16)
        ss_pe = jnp.sum(jnp.where(rope_lanes, e * e, 0.0), axis=-1, keepdims=True)
        sspe_scr[...] = jnp.broadcast_to(ss_pe, sspe_scr.shape)
        rot = e * e1_ref[...] + pltpu.roll(e, LANES - MLA_ROPE, axis=1) * e2_ref[...]
        rot_hi = rot.astype(bf16).astype(f32)
        rot_lo = (rot - rot_hi).astype(bf16).astype(f32)
        lhsk_scr[:, LANES:] = (rot_hi + pltpu.roll(rot_lo, MLA_ROPE, axis=1)).astype(bf16)

    def q_dot(pair):
        cols = slice(pair * PROJ_BLK, (pair + 1) * PROJ_BLK)
        return lambda: (_dot(cqn_scr[...], wq_ref[:, cols]), _dot(cqn_scr[...], wqs_ref[:, cols]))

    def q_post(pair):
        def post(ys):
            for i, (a, b) in enumerate(zip(tiles(ys[0]), tiles(ys[1]))):
                r = lax.rsqrt(jnp.sum(a * a, axis=-1, keepdims=True) / MLA_QK_DIM + EPS)
                col0 = pair * PROJ_BLK + i * HEAD_PAD
                qf_ref[:, col0:col0 + HEAD_PAD] = ((a * tq1_ref[...] + b * tq2_ref[...]) * r).astype(bf16)
        return post

    def k_dot(pair):
        cols = slice(pair * PROJ_BLK, (pair + 1) * PROJ_BLK)
        return lambda: _dot(lhsk_scr[...], wk_ref[:, cols])

    def k_post(pair):
        def post(y):
            for i, a in enumerate(tiles(y)):
                ss = jnp.sum(jnp.where(lo_half, a * a, 0.0), axis=-1, keepdims=True) + sspe_scr[...]
                r = lax.rsqrt(ss / MLA_QK_DIM + EPS)
                col0 = pair * PROJ_BLK + i * HEAD_PAD
                kf_ref[:, col0:col0 + HEAD_PAD] = (a * gkf_ref[...] * r).astype(bf16)
        return post

    def v_dot(pair):
        rows = slice(pair * PROJ_BLK, (pair + 1) * PROJ_BLK)
        return lambda: _dot_nt(wvt_ref[rows, :], lhsk_scr[:, :LANES])

    def v_post(pair):
        def post(y):
            for i in range(PROJ_BLK // HEAD_PAD):
                row0 = pair * PROJ_BLK + i * HEAD_PAD
                a = y[i * HEAD_PAD:(i + 1) * HEAD_PAD, :]
                vbt_ref[0, row0:row0 + HEAD_PAD, :] = jnp.where(ones_row, 1.0, a).astype(bf16)
        return post

    def nav_dot(row0):
        return lambda: _dot_nt(wnavt_ref[row0:row0 + PROJ_BLK, :], hb_scr[...])

    def nav_post(row0):
        def post(y):
            vat_ref[0, row0:row0 + PROJ_BLK, :] = y.astype(bf16)
        return post

    def in_blocks(off, width, make_post):
        return [(in_dot(off + c), make_post(c)) for c in range(0, width, PROJ_BLK)]

    tasks = [(in_dot(_OFF_CQ), post_cq), (in_dot(_OFF_CKV), post_ckv)]
    tasks += in_blocks(_OFF_NAQ, NA_WIDTH, lambda c: headnorm64(qa_ref, gq_ref, c))
    tasks += in_blocks(_OFF_NAK, NA_WIDTH, lambda c: headnorm64(ka_ref, gk_ref, c))
    tasks += [(q_dot(p), q_post(p)) for p in range(MLA_PAD_WIDTH // PROJ_BLK)]
    tasks += [(nav_dot(r), nav_post(r)) for r in range(0, NA_WIDTH, PROJ_BLK)]
    tasks += [(k_dot(p), k_post(p)) for p in range(MLA_PAD_WIDTH // PROJ_BLK)]
    tasks += in_blocks(_OFF_NAG, NA_WIDTH, lambda c: elementwise(nag_ref, c, silu))
    tasks += [(v_dot(p), v_post(p)) for p in range(MLA_PAD_WIDTH // PROJ_BLK)]
    tasks += in_blocks(_OFF_MG, MLA_WIDTH, lambda c: elementwise(mg_ref, c, silu))
    tasks += in_blocks(_OFF_GNA, D_MODEL, lambda c: elementwise(gna_ref, c, jax.nn.sigmoid))
    tasks += in_blocks(_OFF_GMLA, D_MODEL, lambda c: elementwise(gmla_ref, c, jax.nn.sigmoid))

    _run_tasks([(dot_fn, post_fn, False) for dot_fn, post_fn in tasks])


def _const_spec(shape):
    nd = len(shape)
    return pl.BlockSpec(shape, lambda i: (0,) * nd, pipeline_mode=pl.Buffered(1))


def _proj_call(x2d, lp, tabs, seq_len):
    n_tok = x2d.shape[0]
    tm = PROJ_TILE
    assert n_tok % tm == 0 and seq_len % tm == 0
    tiles_per_seq = seq_len // tm
    bf16 = jnp.bfloat16

    def row_spec(width):
        return pl.BlockSpec((tm, width), lambda i: (i, 0))

    def tab_spec():
        return pl.BlockSpec((tm, LANES), lambda i: (i % tiles_per_seq, 0))

    consts = [lp["ln_g"], lp["w_in"], lp["gq2"], lp["gk2"], lp["gcq"], lp["gckv"],
              lp["wq"], lp["wqs"], lp["wk"], lp["wvt"], lp["wnavt"], lp["gkf"]]
    out_widths = [NA_WIDTH, NA_WIDTH, NA_WIDTH, MLA_PAD_WIDTH, MLA_PAD_WIDTH,
                  MLA_WIDTH, D_MODEL, D_MODEL]
    t_widths = [NA_WIDTH, MLA_PAD_WIDTH]
    t_specs = [pl.BlockSpec((1, w, tm), lambda i: (i // tiles_per_seq, 0, i % tiles_per_seq)) for w in t_widths]
    t_shapes = [jax.ShapeDtypeStruct((n_tok // seq_len, w, seq_len), bf16) for w in t_widths]
    return pl.pallas_call(
        _proj_kernel,
        grid=(n_tok // tm,),
        in_specs=[row_spec(D_MODEL)] + [_const_spec(c.shape) for c in consts] + [tab_spec()] * 4,
        out_specs=[row_spec(w) for w in out_widths] + t_specs,
        out_shape=[jax.ShapeDtypeStruct((n_tok, w), bf16) for w in out_widths] + t_shapes,
        scratch_shapes=[pltpu.VMEM((tm, D_MODEL), bf16),
                        pltpu.VMEM((tm, MLA_Q_LORA), bf16),
                        pltpu.VMEM((tm, 2 * LANES), bf16),
                        pltpu.VMEM((tm, LANES), jnp.float32)],
        compiler_params=pltpu.CompilerParams(
            dimension_semantics=("arbitrary",), vmem_limit_bytes=VMEM_LIMIT_BYTES),
        name="proj",
    )(x2d, *consts, tabs["tq1"], tabs["tq2"], tabs["e1"], tabs["e2"])


def _na_band_start(i, rows):
    return jnp.clip(NA_QROWS * i - NA_WIN_ROWS // 2, 0, rows - NA_BAND)


def _natt_kernel(q_ref, k_ref, vt_ref, gate_ref, tbl_ref, o_ref, s_scr, p_scr):
    bf16 = jnp.bfloat16
    i = pl.program_id(1)
    start = pl.multiple_of(_na_band_start(i, k_ref.shape[1] // GRID_W) * GRID_W, NA_QROWS * GRID_W)
    lo_half = _lane_iota() < NA_HEAD_DIM
    n_chunks = NA_TK // NA_KC
    sublanes = 8

    def pcols(head):
        return slice((head // 2) * LANES, (head // 2 + 1) * LANES)

    def kslice(c):
        return slice(c * NA_KC, (c + 1) * NA_KC)

    def fold(x, run, op):
        part = x.reshape(x.shape[0] // sublanes, sublanes, x.shape[1])
        part = jnp.max(part, axis=0) if op is jnp.maximum else jnp.sum(part, axis=0)
        return part if run is None else op(run, part)

    def masked_q(head):
        qp = q_ref[0, :, pcols(head)]
        sel = lo_half if head % 2 == 0 else jnp.logical_not(lo_half)
        return jnp.where(sel, qp, jnp.zeros_like(qp))

    def qk_chunk(head, qm, c, run_max):
        keys = k_ref[0, pl.ds(start + c * NA_KC, NA_KC), pcols(head)]
        s = _dot_nt(keys, qm) + tbl_ref[0, head, kslice(c), :]
        s_scr[head % 2, kslice(c), :] = s
        return fold(s, run_max, jnp.maximum)

    def exp_chunk(head, c, m, run_sum):
        p = jnp.exp2(s_scr[head % 2, kslice(c), :] - m)
        p_scr[head % 2, kslice(c), :] = p.astype(bf16)
        return fold(p, run_sum, jnp.add)

    def pv_chunk(head, c, acc):
        vals = vt_ref[0, pcols(head), pl.ds(start + c * NA_KC, NA_KC)]
        o = _dot(vals, p_scr[head % 2, kslice(c), :])
        return o if acc is None else acc + o

    def finish(head, acc, run_sum, prev):
        o = (acc / jnp.sum(run_sum, axis=0, keepdims=True)).T
        if head % 2 == 0:
            return o
        pair = jnp.where(lo_half, prev, o)
        o_ref[0, :, pcols(head)] = (pair * gate_ref[0, :, pcols(head)].astype(jnp.float32)).astype(bf16)
        return None

    qm = masked_q(0)
    run = None
    for c in range(n_chunks):
        run = qk_chunk(0, qm, c, run)
    m = jnp.max(run, axis=0, keepdims=True)
    prev_sum, even_out = None, None
    for head in range(NA_HEADS):
        run, acc, cur_sum = None, None, None
        if head + 1 < NA_HEADS:
            qm = masked_q(head + 1)
        for c in range(n_chunks):
            if head + 1 < NA_HEADS:
                run = qk_chunk(head + 1, qm, c, run)
            if head >= 1:
                acc = pv_chunk(head - 1, c, acc)
            cur_sum = exp_chunk(head, c, m, cur_sum)
        if head >= 1:
            even_out = finish(head - 1, acc, prev_sum, even_out)
        prev_sum = cur_sum
        if head + 1 < NA_HEADS:
            m = jnp.max(run, axis=0, keepdims=True)
    acc = None
    for c in range(n_chunks):
        acc = pv_chunk(NA_HEADS - 1, c, acc)
    finish(NA_HEADS - 1, acc, prev_sum, even_out)


def _natt_call(qa, ka, vat, nag, tbl):
    bsz, seq_len, _ = qa.shape
    n_blk = seq_len // NA_TQ

    def pattern(i):
        return jnp.minimum(i, 1) + (i == n_blk - 1).astype(jnp.int32)

    blk = pl.BlockSpec((1, NA_TQ, NA_WIDTH), lambda b, i: (b, i, 0))
    return pl.pallas_call(
        _natt_kernel,
        grid=(bsz, n_blk),
        in_specs=[blk,
                  pl.BlockSpec((1, seq_len, NA_WIDTH), lambda b, i: (b, 0, 0)),
                  pl.BlockSpec((1, NA_WIDTH, seq_len), lambda b, i: (b, 0, 0)),
                  blk,
                  pl.BlockSpec((1, NA_HEADS, NA_TK, NA_TQ), lambda b, i: (pattern(i), 0, 0, 0))],
        out_specs=blk,
        out_shape=jax.ShapeDtypeStruct((bsz, seq_len, NA_WIDTH), jnp.bfloat16),
        scratch_shapes=[pltpu.VMEM((2, NA_TK, NA_TQ), jnp.float32),
                        pltpu.VMEM((2, NA_TK, NA_TQ), jnp.bfloat16)],
        compiler_params=pltpu.CompilerParams(
            dimension_semantics=("arbitrary", "arbitrary"), vmem_limit_bytes=VMEM_LIMIT_BYTES),
        name="natt",
    )(qa, ka, vat, nag, tbl)


def _mla_kernel(q_ref, k_ref, vt_ref, gate_ref, o_ref, s_scr, p_scr):
    bf16 = jnp.bfloat16
    seq_len = k_ref.shape[1]
    n_chunks = seq_len // MLA_KC
    n_heads = q_ref.shape[2] // HEAD_PAD
    lo_half = _lane_iota() < MLA_V
    sublanes = 8

    def hcols(head):
        return slice(head * HEAD_PAD, (head + 1) * HEAD_PAD)

    def kslice(c):
        return slice(c * MLA_KC, (c + 1) * MLA_KC)

    def qk_chunk(head, c, run_max):
        s = _dot_nt(k_ref[0, kslice(c), hcols(head)], q_ref[0, :, hcols(head)])
        s_scr[head % 2, kslice(c), :] = s
        part = jnp.max(s.reshape(MLA_KC // sublanes, sublanes, MLA_TQ), axis=0)
        return part if run_max is None else jnp.maximum(run_max, part)

    def exp_chunk(head, c, m):
        p_scr[head % 2, kslice(c), :] = jnp.exp2(s_scr[head % 2, kslice(c), :] - m).astype(bf16)

    def pv_chunk(head, c, acc):
        o = _dot(vt_ref[0, hcols(head), kslice(c)], p_scr[head % 2, kslice(c), :])
        return o if acc is None else acc + o

    def finish(head, acc, prev):
        acc = acc.T
        o = acc / acc[:, MLA_V:MLA_V + 1]
        if head % 2 == 0:
            return o
        cols = slice((head // 2) * LANES, (head // 2 + 1) * LANES)
        pair = jnp.where(lo_half, prev, pltpu.roll(o, MLA_V, axis=1))
        o_ref[0, :, cols] = (pair * gate_ref[0, :, cols].astype(jnp.float32)).astype(bf16)
        return None

    run = None
    for c in range(n_chunks):
        run = qk_chunk(0, c, run)
    m = jnp.max(run, axis=0, keepdims=True)
    acc, even_out = None, None
    for head in range(n_heads):
        run, next_acc = None, None
        for c in range(n_chunks):
            if head + 1 < n_heads:
                run = qk_chunk(head + 1, c, run)
            if head >= 1:
                next_acc = pv_chunk(head - 1, c, next_acc)
            exp_chunk(head, c, m)
        if head >= 1:
            even_out = finish(head - 1, next_acc, even_out)
        if head + 1 < n_heads:
            m = jnp.max(run, axis=0, keepdims=True)
    for c in range(n_chunks):
        acc = pv_chunk(n_heads - 1, c, acc)
    finish(n_heads - 1, acc, even_out)


def _mla_call(qf, kf, vbt, mg):
    bsz, seq_len, _ = qf.shape
    n_groups = MLA_HEADS // MLA_GROUP
    assert seq_len % MLA_KC == 0 and seq_len % MLA_TQ == 0 and MLA_GROUP % 2 == 0
    return pl.pallas_call(
        _mla_kernel,
        grid=(bsz, n_groups, seq_len // MLA_TQ),
        in_specs=[pl.BlockSpec((1, MLA_TQ, MLA_GROUP * HEAD_PAD), lambda b, p, j: (b, j, p)),
                  pl.BlockSpec((1, seq_len, MLA_GROUP * HEAD_PAD), lambda b, p, j: (b, 0, p)),
                  pl.BlockSpec((1, MLA_GROUP * HEAD_PAD, seq_len), lambda b, p, j: (b, p, 0)),
                  pl.BlockSpec((1, MLA_TQ, MLA_GROUP * MLA_V), lambda b, p, j: (b, j, p))],
        out_specs=pl.BlockSpec((1, MLA_TQ, MLA_GROUP * MLA_V), lambda b, p, j: (b, j, p)),
        out_shape=jax.ShapeDtypeStruct((bsz, seq_len, MLA_WIDTH), jnp.bfloat16),
        scratch_shapes=[pltpu.VMEM((2, seq_len, MLA_TQ), jnp.float32),
                        pltpu.VMEM((2, seq_len, MLA_TQ), jnp.bfloat16)],
        compiler_params=pltpu.CompilerParams(
            dimension_semantics=("arbitrary", "arbitrary", "arbitrary"),
            vmem_limit_bytes=VMEM_LIMIT_BYTES),
        name="mla",
    )(qf, kf, vbt, mg)


def _run_tasks(tasks):
    pending = None
    for dot_fn, post_fn, barrier in tasks:
        if barrier and pending is not None:
            pending[0](pending[1])
            pending = None
        y = dot_fn()
        if pending is not None:
            pending[0](pending[1])
        pending = (post_fn, y)
    pending[0](pending[1])


def _out_kernel(x_ref, ana_ref, amla_ref, gna_ref, gmla_ref, wna_ref, wmla_ref, wout_ref, o_ref, y_scr):
    f32 = jnp.float32
    row_tiles = [slice(r, r + PROJ_TILE) for r in range(0, o_ref.shape[0], PROJ_TILE)]
    col_blocks = [slice(c, c + PROJ_BLK) for c in range(0, o_ref.shape[1], PROJ_BLK)]

    def branch_dots(rows, cols):
        return lambda: (_dot(ana_ref[rows, :], wna_ref[:, cols]), _dot(amla_ref[rows, :], wmla_ref[:, cols]))

    def merge(rows, cols):
        def post(us):
            y = gna_ref[rows, cols].astype(f32) * us[0] + gmla_ref[rows, cols].astype(f32) * us[1]
            y_scr[rows, cols] = y.astype(jnp.bfloat16)
        return post

    def out_dot(rows, cols):
        return lambda: _dot(y_scr[rows, :], wout_ref[:, cols])

    def residual(rows, cols):
        def post(y):
            o_ref[rows, cols] = x_ref[rows, cols] + y
        return post

    tasks = [(branch_dots(r, c), merge(r, c), False) for r in row_tiles for c in col_blocks]
    tasks += [(out_dot(r, c), residual(r, c), i == 0)
              for i, (r, c) in enumerate((r, c) for r in row_tiles for c in col_blocks)]
    _run_tasks(tasks)


def _out_call(x2d, a_na, a_mla, gna, gmla, lp):
    n_tok = x2d.shape[0]
    tm = OUT_TILE
    assert n_tok % tm == 0 and tm % PROJ_TILE == 0

    def row_spec(width):
        return pl.BlockSpec((tm, width), lambda i: (i, 0))

    consts = [lp["w_o_na"], lp["w_o_mla"], lp["w_out"]]
    return pl.pallas_call(
        _out_kernel,
        grid=(n_tok // tm,),
        in_specs=[row_spec(D_MODEL), row_spec(NA_WIDTH), row_spec(MLA_WIDTH), row_spec(D_MODEL),
                  row_spec(D_MODEL)] + [_const_spec(c.shape) for c in consts],
        out_specs=row_spec(D_MODEL),
        out_shape=jax.ShapeDtypeStruct((n_tok, D_MODEL), jnp.float32),
        scratch_shapes=[pltpu.VMEM((tm, D_MODEL), jnp.bfloat16)],
        compiler_params=pltpu.CompilerParams(
            dimension_semantics=("arbitrary",), vmem_limit_bytes=VMEM_LIMIT_BYTES),
        name="out",
    )(x2d, a_na, a_mla, gna, gmla, *consts)


_N_DROW = 2 * NA_WIN_ROWS - 1
_N_DCOL = 2 * NA_WIN_COLS - 1


def _na_table_static(seq_len):
    rows = seq_len // GRID_W
    n_blk = rows // NA_QROWS
    qc = np.arange(GRID_W)
    kj = np.arange(GRID_W)
    cs = np.clip(qc - NA_WIN_COLS // 2, 0, GRID_W - NA_WIN_COLS)
    col_valid = (kj[None, :] >= cs[:, None]) & (kj[None, :] < cs[:, None] + NA_WIN_COLS)
    dcol = kj[None, :] - qc[:, None] + (NA_WIN_COLS - 1)
    onehot = (dcol[None] == np.arange(_N_DCOL)[:, None, None]) & col_valid[None]
    col_mask = np.where(col_valid, 0.0, MASK_VALUE).astype(np.float32)
    qr = np.arange(NA_QROWS)
    kn = np.arange(NA_BAND)
    idx = []
    for i in (0, 1, n_blk - 1):
        band = int(np.clip(NA_QROWS * i - NA_WIN_ROWS // 2, 0, rows - NA_BAND))
        r = NA_QROWS * i + qr
        rs = np.clip(r - NA_WIN_ROWS // 2, 0, rows - NA_WIN_ROWS)
        krow = band + kn
        valid = (krow[None, :] >= rs[:, None]) & (krow[None, :] < rs[:, None] + NA_WIN_ROWS)
        drow = krow[None, :] - r[:, None] + (NA_WIN_ROWS - 1)
        idx.append(np.where(valid, drow, _N_DROW))
    return onehot.astype(np.float32), col_mask, np.stack(idx).astype(np.int32)


def _na_table(rel_bias, onehot, col_mask, idx):
    blocks = jnp.einsum("hdc,cqj->hdqj", rel_bias, onehot, precision=lax.Precision.HIGHEST) + col_mask
    blocks = jnp.concatenate(
        [blocks, jnp.full((NA_HEADS, 1, GRID_W, GRID_W), MASK_VALUE, jnp.float32)], axis=1) * LOG2E
    tbl = jnp.take(blocks, idx.reshape(-1), axis=1)
    tbl = tbl.reshape(NA_HEADS, 3, NA_QROWS, NA_BAND, GRID_W, GRID_W)
    return tbl.transpose(1, 0, 3, 5, 2, 4).reshape(3, NA_HEADS, NA_TK, NA_TQ)


def _rope_tables(seq_len):
    t = jnp.arange(seq_len)
    row = (t // GRID_W).astype(jnp.float32)
    col = (t % GRID_W).astype(jnp.float32)
    half = MLA_ROPE // 2
    n_freq = half // 2
    inv = jnp.power(jnp.float32(ROPE_BASE), -jnp.arange(n_freq, dtype=jnp.float32) / n_freq)
    ang = jnp.concatenate([row[:, None] * inv, col[:, None] * inv], axis=-1)
    return jnp.cos(ang), jnp.sin(ang)


def _place(cols_to_blocks, total):
    parts = []
    for arr, width in cols_to_blocks:
        pad = width - arr.shape[-1]
        parts.append(jnp.pad(arr, [(0, 0)] * (arr.ndim - 1) + [(0, pad)]) if pad else arr)
    out = jnp.concatenate(parts, axis=-1)
    assert out.shape[-1] == total
    return out


def _layer_params(l, p, cos, sin):
    f32, bf16 = jnp.float32, jnp.bfloat16
    half = MLA_ROPE // 2
    w_in = p["w_in"][l]
    o = 0
    pieces = {}
    for name, size in (("naq", NA_WIDTH), ("nak", NA_WIDTH), ("nav", NA_WIDTH), ("nag", NA_WIDTH),
                       ("cq", MLA_Q_LORA), ("ckv", MLA_KV_LORA), ("kpe", MLA_ROPE),
                       ("mg", MLA_WIDTH), ("gna", D_MODEL), ("gmla", D_MODEL)):
        pieces[name] = w_in[:, o:o + size]
        o += size
    kpe = pieces["kpe"]
    kpe_sw = jnp.concatenate([kpe[:, half:], kpe[:, :half]], axis=-1)
    ckvx = _place([(pieces["ckv"], LANES), (jnp.concatenate([kpe, kpe_sw], axis=-1), LANES)], 2 * LANES)
    w_in_packed = jnp.concatenate(
        [pieces["naq"], pieces["nak"], pieces["nag"], pieces["cq"], ckvx,
         pieces["mg"], pieces["gna"], pieces["gmla"]], axis=-1).astype(bf16)
    assert w_in_packed.shape[-1] == _W_IN_COLS
    wnavt = pieces["nav"].T.astype(bf16)

    w_uq = p["w_uq"][l].reshape(MLA_Q_LORA, MLA_HEADS, MLA_QK_DIM)
    nope, x1, x2 = w_uq[..., :MLA_NOPE], w_uq[..., MLA_NOPE:MLA_NOPE + half], w_uq[..., MLA_NOPE + half:]
    zq = jnp.zeros_like(nope)
    wq = _place([(jnp.concatenate([nope, x1, x2], -1), HEAD_PAD)], HEAD_PAD)
    wqs = _place([(jnp.concatenate([zq, x2, x1], -1), HEAD_PAD)], HEAD_PAD)
    wq = wq.reshape(MLA_Q_LORA, MLA_PAD_WIDTH).astype(bf16)
    wqs = wqs.reshape(MLA_Q_LORA, MLA_PAD_WIDTH).astype(bf16)

    w_ukv = p["w_ukv"][l].reshape(MLA_KV_LORA, MLA_HEADS, MLA_NOPE + MLA_V)
    k_nope = _place([(w_ukv[..., :MLA_NOPE], HEAD_PAD)], HEAD_PAD).reshape(MLA_KV_LORA, MLA_PAD_WIDTH)
    eye = jnp.eye(MLA_ROPE, dtype=f32)
    put = jnp.pad(eye, ((0, 0), (MLA_NOPE, HEAD_PAD - MLA_QK_DIM)))
    put = jnp.tile(put, (1, MLA_HEADS))
    wk = jnp.concatenate([k_nope, put, put, jnp.zeros((2 * LANES - MLA_KV_LORA - 2 * MLA_ROPE,
                                                        MLA_PAD_WIDTH), f32)], axis=0).astype(bf16)
    wvt = _place([(w_ukv[..., MLA_NOPE:], HEAD_PAD)], HEAD_PAD).reshape(MLA_KV_LORA, MLA_PAD_WIDTH).T.astype(bf16)

    gq = p["mla_q_norm"][l]
    gk = p["mla_k_norm"][l]
    scale = MLA_QK_DIM ** -0.5 * LOG2E
    seq_len = cos.shape[0]
    ones = jnp.ones((seq_len, 1), f32)
    tq1 = _place([(ones * gq[None, :MLA_NOPE], MLA_NOPE),
                  (cos * gq[None, MLA_NOPE:MLA_NOPE + half], half),
                  (cos * gq[None, MLA_NOPE + half:], half)], MLA_QK_DIM) * scale
    tq2 = _place([(jnp.zeros((seq_len, MLA_NOPE), f32), MLA_NOPE),
                  (-sin * gq[None, MLA_NOPE + half:], half),
                  (sin * gq[None, MLA_NOPE:MLA_NOPE + half], half)], MLA_QK_DIM) * scale
    e1 = jnp.concatenate([cos * gk[None, MLA_NOPE:MLA_NOPE + half], cos * gk[None, MLA_NOPE + half:]], -1)
    e2 = jnp.concatenate([-sin * gk[None, MLA_NOPE + half:], sin * gk[None, MLA_NOPE:MLA_NOPE + half]], -1)
    tabs = {"tq1": _place([(tq1, LANES)], LANES), "tq2": _place([(tq2, LANES)], LANES),
            "e1": _place([(e1, LANES)], LANES), "e2": _place([(e2, LANES)], LANES)}
    gkf = _place([(gk[None, :MLA_NOPE], MLA_NOPE), (jnp.ones((1, MLA_ROPE), f32), LANES - MLA_NOPE)], LANES)

    na_scale = NA_HEAD_DIM ** -0.5 * LOG2E
    lp = {
        "ln_g": p["ln_g"][l][None, :],
        "w_in": w_in_packed,
        "gq2": jnp.tile(p["na_q_norm"][l], 2)[None, :] * na_scale,
        "gk2": jnp.tile(p["na_k_norm"][l], 2)[None, :],
        "gcq": p["mla_cq_norm"][l][None, :],
        "gckv": p["mla_ckv_norm"][l][None, :],
        "wq": wq, "wqs": wqs, "wk": wk, "wvt": wvt, "wnavt": wnavt, "gkf": gkf,
        "w_o_na": p["w_o_na"][l].astype(bf16),
        "w_o_mla": p["w_o_mla"][l].astype(bf16),
        "w_out": p["w_out"][l].astype(bf16),
    }
    return lp, tabs


@jax.jit
def _forward(x, p):
    bsz, seq_len, _ = x.shape
    depth = p["w_in"].shape[0]
    cos, sin = _rope_tables(seq_len)
    onehot, col_mask, tbl_idx = _na_table_static(seq_len)
    x2d = x.reshape(bsz * seq_len, D_MODEL)
    for l in range(depth):
        lp, tabs = _layer_params(l, p, cos, sin)
        tbl = _na_table(p["na_rel_bias"][l], onehot, col_mask, tbl_idx)
        qa, ka, nag, qf, kf, mg, gna, gmla, vat, vbt = _proj_call(x2d, lp, tabs, seq_len)

        def b3(a):
            return a.reshape(bsz, seq_len, a.shape[-1])

        a_na = _natt_call(b3(qa), b3(ka), vat, b3(nag), tbl)
        a_mla = _mla_call(b3(qf), b3(kf), vbt, b3(mg))
        x2d = _out_call(x2d, a_na.reshape(-1, NA_WIDTH), a_mla.reshape(-1, MLA_WIDTH), gna, gmla, lp)
    return x2d.reshape(bsz, seq_len, D_MODEL)


def kernel(x, ln_g, w_in, na_q_norm, na_k_norm, na_rel_bias, mla_cq_norm, mla_ckv_norm, w_uq, w_ukv,
           mla_q_norm, mla_k_norm, w_o_na, w_o_mla, w_out):
    params = dict(ln_g=ln_g, w_in=w_in, na_q_norm=na_q_norm, na_k_norm=na_k_norm,
                  na_rel_bias=na_rel_bias, mla_cq_norm=mla_cq_norm, mla_ckv_norm=mla_ckv_norm,
                  w_uq=w_uq, w_ukv=w_ukv, mla_q_norm=mla_q_norm, mla_k_norm=mla_k_norm,
                  w_o_na=w_o_na, w_o_mla=w_o_mla, w_out=w_out)
    return _forward(x, params)
```

```python
import functools

import numpy as np
import jax
import jax.numpy as jnp
from jax import lax
from jax.experimental import pallas as pl
from jax.experimental.pallas import tpu as pltpu

D_MODEL = 1024
GRID_W = 64
NA_HEADS = 8
NA_HEAD_DIM = 64
NA_WIDTH = NA_HEADS * NA_HEAD_DIM
NA_WIN_ROWS = 8
NA_WIN_COLS = 16
MLA_HEADS = 8
MLA_NOPE = 64
MLA_ROPE = 32
MLA_QK_DIM = MLA_NOPE + MLA_ROPE
MLA_V = 64
MLA_Q_LORA = 256
MLA_KV_LORA = 128
MLA_WIDTH = MLA_HEADS * MLA_V
ROPE_BASE = 10000.0
EPS = 1e-6

LANES = 128
HEAD_PAD = LANES
MLA_PAD_WIDTH = MLA_HEADS * HEAD_PAD
MLA_VT_ROWS = 80
VMEM_LIMIT_BYTES = 56 * 1024 * 1024

_OFF_NAQ = 0
_OFF_NAK = _OFF_NAQ + NA_WIDTH
_OFF_NAV = _OFF_NAK + NA_WIDTH
_OFF_NAG = _OFF_NAV + NA_WIDTH
_OFF_CQ = _OFF_NAG + NA_WIDTH
_OFF_CKV = _OFF_CQ + MLA_Q_LORA
_OFF_MG = _OFF_CKV + 2 * LANES
_OFF_GNA = _OFF_MG + MLA_WIDTH
_OFF_GMLA = _OFF_GNA + D_MODEL
_W_IN_COLS = _OFF_GMLA + D_MODEL

PROJ_TILE = 512
OUT_TILE = 1024
PROJ_BLK = 256
NA_QROWS = 4
NA_BAND = 12
NA_TQ = NA_QROWS * GRID_W
NA_TK = NA_BAND * GRID_W
NA_KC = 768
MLA_TQ = 256
MLA_KC = 1024
MLA_GROUP = 8
MASK_VALUE = -1e30
LOG2E = 1.4426950408889634

_NT_DIMS = (((1,), (1,)), ((), ()))


def _dot(a, b):
    return jnp.dot(a, b, preferred_element_type=jnp.float32)


def _dot_nt(a, b):
    return lax.dot_general(a, b, _NT_DIMS, preferred_element_type=jnp.float32)


def _lane_iota():
    return lax.broadcasted_iota(jnp.int32, (1, LANES), 1)


def _proj_kernel(x_ref, lng_ref, w_ref, gq_ref, gk_ref, gcq_ref, gckv_ref, wq_ref, wqs_ref,
                 wk_ref, wvt_ref, gkf_ref, tq1_ref, tq2_ref, e1_ref, e2_ref,
                 qa_ref, ka_ref, va_ref, nag_ref, qf_ref, kf_ref, mg_ref,
                 gna_ref, gmla_ref, vbt_ref, hb_scr, cqn_scr, lhsk_scr, sspe_scr):
    bf16 = jnp.bfloat16
    f32 = jnp.float32
    x = x_ref[...]
    hb_scr[...] = (x * lax.rsqrt(jnp.mean(x * x, axis=-1, keepdims=True) + EPS) * lng_ref[...]).astype(bf16)
    lo_half = _lane_iota() < NA_HEAD_DIM
    rope_lanes = _lane_iota() < MLA_ROPE
    ones_row = lax.broadcasted_iota(jnp.int32, (MLA_VT_ROWS, 1), 0) == MLA_V

    def tiles(y):
        return [y[:, i * LANES:(i + 1) * LANES] for i in range(y.shape[1] // LANES)]

    def in_dot(off):
        return lambda: _dot(hb_scr[...], w_ref[:, off:off + PROJ_BLK])

    def headnorm64(out_ref, g_ref, col0):
        def post(y):
            for i, pc in enumerate(tiles(y)):
                sq = pc * pc
                s_lo = jnp.sum(jnp.where(lo_half, sq, 0.0), axis=-1, keepdims=True)
                s_hi = jnp.sum(jnp.where(lo_half, 0.0, sq), axis=-1, keepdims=True)
                r = jnp.where(lo_half,
                              lax.rsqrt(s_lo * (1.0 / NA_HEAD_DIM) + EPS),
                              lax.rsqrt(s_hi * (1.0 / NA_HEAD_DIM) + EPS))
                out_ref[:, col0 + i * LANES:col0 + (i + 1) * LANES] = (pc * r * g_ref[...]).astype(bf16)
        return post

    def elementwise(out_ref, col0, fn):
        def post(y):
            out_ref[:, col0:col0 + PROJ_BLK] = fn(y).astype(bf16)
        return post

    def silu(g):
        return g * jax.nn.sigmoid(g)

    def post_cq(y):
        cqn_scr[...] = (y * lax.rsqrt(jnp.mean(y * y, axis=-1, keepdims=True) + EPS) * gcq_ref[...]).astype(bf16)

    def post_ckv(y):
        ckv, e = tiles(y)
        lhsk_scr[:, :LANES] = (ckv * lax.rsqrt(jnp.mean(ckv * ckv, axis=-1, keepdims=True) + EPS)
                               * gckv_ref[...]).astype(bf16)
        ss_pe = jnp.sum(jnp.where(rope_lanes, e * e, 0.0), axis=-1, keepdims=True)
        sspe_scr[...] = jnp.broadcast_to(ss_pe, sspe_scr.shape)
        rot = e * e1_ref[...] + pltpu.roll(e, LANES - MLA_ROPE, axis=1) * e2_ref[...]
        rot_hi = rot.astype(bf16).astype(f32)
        rot_lo = (rot - rot_hi).astype(bf16).astype(f32)
        lhsk_scr[:, LANES:] = (rot_hi + pltpu.roll(rot_lo, MLA_ROPE, axis=1)).astype(bf16)

    def q_dot(pair):
        cols = slice(pair * PROJ_BLK, (pair + 1) * PROJ_BLK)
        return lambda: (_dot(cqn_scr[...], wq_ref[:, cols]), _dot(cqn_scr[...], wqs_ref[:, cols]))

    def q_post(pair):
        def post(ys):
            for i, (a, b) in enumerate(zip(tiles(ys[0]), tiles(ys[1]))):
                r = lax.rsqrt(jnp.sum(a * a, axis=-1, keepdims=True) / MLA_QK_DIM + EPS)
                col0 = pair * PROJ_BLK + i * HEAD_PAD
                qf_ref[:, col0:col0 + HEAD_PAD] = ((a * tq1_ref[...] + b * tq2_ref[...]) * r).astype(bf16)
        return post

    def k_dot(pair):
        cols = slice(pair * PROJ_BLK, (pair + 1) * PROJ_BLK)
        return lambda: _dot(lhsk_scr[...], wk_ref[:, cols])

    def k_post(pair):
        def post(y):
            for i, a in enumerate(tiles(y)):
                ss = jnp.sum(jnp.where(lo_half, a * a, 0.0), axis=-1, keepdims=True) + sspe_scr[...]
                r = lax.rsqrt(ss / MLA_QK_DIM + EPS)
                col0 = pair * PROJ_BLK + i * HEAD_PAD
                kf_ref[:, col0:col0 + HEAD_PAD] = (a * gkf_ref[...] * r).astype(bf16)
        return post

    def v_dot(pair):
        rows = slice(pair * 2 * MLA_VT_ROWS, (pair + 1) * 2 * MLA_VT_ROWS)
        return lambda: _dot_nt(wvt_ref[rows, :], lhsk_scr[:, :LANES])

    def v_post(pair):
        def post(y):
            for i in range(2):
                row0 = (2 * pair + i) * MLA_VT_ROWS
                a = y[i * MLA_VT_ROWS:(i + 1) * MLA_VT_ROWS, :]
                vbt_ref[0, row0:row0 + MLA_VT_ROWS, :] = jnp.where(ones_row, 1.0, a).astype(bf16)
        return post

    def in_blocks(off, width, make_post):
        return [(in_dot(off + c), make_post(c)) for c in range(0, width, PROJ_BLK)]

    tasks = [(in_dot(_OFF_CQ), post_cq), (in_dot(_OFF_CKV), post_ckv)]
    tasks += in_blocks(_OFF_NAQ, NA_WIDTH, lambda c: headnorm64(qa_ref, gq_ref, c))
    tasks += in_blocks(_OFF_NAK, NA_WIDTH, lambda c: headnorm64(ka_ref, gk_ref, c))
    tasks += [(q_dot(p), q_post(p)) for p in range(MLA_PAD_WIDTH // PROJ_BLK)]
    tasks += in_blocks(_OFF_NAV, NA_WIDTH, lambda c: elementwise(va_ref, c, lambda y: y))
    tasks += [(k_dot(p), k_post(p)) for p in range(MLA_PAD_WIDTH // PROJ_BLK)]
    tasks += in_blocks(_OFF_NAG, NA_WIDTH, lambda c: elementwise(nag_ref, c, silu))
    tasks += [(v_dot(p), v_post(p)) for p in range(MLA_PAD_WIDTH // PROJ_BLK)]
    tasks += in_blocks(_OFF_MG, MLA_WIDTH, lambda c: elementwise(mg_ref, c, silu))
    tasks += in_blocks(_OFF_GNA, D_MODEL, lambda c: elementwise(gna_ref, c, jax.nn.sigmoid))
    tasks += in_blocks(_OFF_GMLA, D_MODEL, lambda c: elementwise(gmla_ref, c, jax.nn.sigmoid))

    _run_tasks([(dot_fn, post_fn, False) for dot_fn, post_fn in tasks])


def _const_spec(shape):
    nd = len(shape)
    return pl.BlockSpec(shape, lambda i: (0,) * nd, pipeline_mode=pl.Buffered(1))


def _proj_call(x2d, lp, tabs, seq_len):
    n_tok = x2d.shape[0]
    tm = PROJ_TILE
    assert n_tok % tm == 0 and seq_len % tm == 0
    tiles_per_seq = seq_len // tm
    bf16 = jnp.bfloat16

    def row_spec(width):
        return pl.BlockSpec((tm, width), lambda i: (i, 0))

    def tab_spec():
        return pl.BlockSpec((tm, LANES), lambda i: (i % tiles_per_seq, 0))

    consts = [lp["ln_g"], lp["w_in"], lp["gq2"], lp["gk2"], lp["gcq"], lp["gckv"],
              lp["wq"], lp["wqs"], lp["wk"], lp["wvt"], lp["gkf"]]
    out_widths = [NA_WIDTH, NA_WIDTH, NA_WIDTH, NA_WIDTH, MLA_PAD_WIDTH, MLA_PAD_WIDTH,
                  MLA_WIDTH, D_MODEL, D_MODEL]
    t_widths = [MLA_HEADS * MLA_VT_ROWS]
    t_specs = [pl.BlockSpec((1, w, tm), lambda i: (i // tiles_per_seq, 0, i % tiles_per_seq)) for w in t_widths]
    t_shapes = [jax.ShapeDtypeStruct((n_tok // seq_len, w, seq_len), bf16) for w in t_widths]
    return pl.pallas_call(
        _proj_kernel,
        grid=(n_tok // tm,),
        in_specs=[row_spec(D_MODEL)] + [_const_spec(c.shape) for c in consts] + [tab_spec()] * 4,
        out_specs=[row_spec(w) for w in out_widths] + t_specs,
        out_shape=[jax.ShapeDtypeStruct((n_tok, w), bf16) for w in out_widths] + t_shapes,
        scratch_shapes=[pltpu.VMEM((tm, D_MODEL), bf16),
                        pltpu.VMEM((tm, MLA_Q_LORA), bf16),
                        pltpu.VMEM((tm, 2 * LANES), bf16),
                        pltpu.VMEM((tm, LANES), jnp.float32)],
        compiler_params=pltpu.CompilerParams(
            dimension_semantics=("arbitrary",), vmem_limit_bytes=VMEM_LIMIT_BYTES),
        name="proj",
    )(x2d, *consts, tabs["tq1"], tabs["tq2"], tabs["e1"], tabs["e2"])


def _na_band_start(i, rows):
    return jnp.clip(NA_QROWS * i - NA_WIN_ROWS // 2, 0, rows - NA_BAND)


def _natt_kernel(q_ref, k_ref, v_ref, gate_ref, tbl_ref, o_ref, s_scr, p_scr):
    bf16 = jnp.bfloat16
    i = pl.program_id(1)
    start = pl.multiple_of(_na_band_start(i, k_ref.shape[1] // GRID_W) * GRID_W, GRID_W)
    lo_half = _lane_iota() < NA_HEAD_DIM
    n_chunks = NA_TK // NA_KC

    def pcols(head):
        return slice((head // 2) * LANES, (head // 2 + 1) * LANES)

    def kslice(c):
        return slice(c * NA_KC, (c + 1) * NA_KC)

    def fold(x, run, op):
        for t in range(x.shape[1] // LANES):
            tile = x[:, t * LANES:(t + 1) * LANES]
            run = tile if run is None else op(run, tile)
        return run

    def masked_q(head):
        qp = q_ref[0, :, pcols(head)]
        sel = lo_half if head % 2 == 0 else jnp.logical_not(lo_half)
        return jnp.where(sel, qp, jnp.zeros_like(qp))

    def qk_chunk(head, qm, c, run_max):
        keys = k_ref[0, pl.ds(start + c * NA_KC, NA_KC), pcols(head)]
        s = _dot_nt(qm, keys) + tbl_ref[0, head, :, kslice(c)]
        s_scr[head % 2, :, kslice(c)] = s
        return fold(s, run_max, jnp.maximum)

    def exp_chunk(head, c, m, run_sum):
        p = jnp.exp2(s_scr[head % 2, :, kslice(c)] - m)
        p_scr[head % 2, :, kslice(c)] = p.astype(bf16)
        return fold(p, run_sum, jnp.add)

    def pv_chunk(head, c, acc):
        vals = v_ref[0, pl.ds(start + c * NA_KC, NA_KC), pcols(head)]
        o = _dot(p_scr[head % 2, :, kslice(c)], vals)
        return o if acc is None else acc + o

    def finish(head, acc, run_sum, prev):
        o = acc / jnp.sum(run_sum, axis=-1, keepdims=True)
        if head % 2 == 0:
            return o
        pair = jnp.where(lo_half, prev, o)
        o_ref[0, :, pcols(head)] = (pair * gate_ref[0, :, pcols(head)].astype(jnp.float32)).astype(bf16)
        return None

    qm = masked_q(0)
    run = None
    for c in range(n_chunks):
        run = qk_chunk(0, qm, c, run)
    m = jnp.max(run, axis=-1, keepdims=True)
    prev_sum, even_out = None, None
    for head in range(NA_HEADS):
        run, acc, cur_sum = None, None, None
        if head + 1 < NA_HEADS:
            qm = masked_q(head + 1)
        for c in range(n_chunks):
            if head + 1 < NA_HEADS:
                run = qk_chunk(head + 1, qm, c, run)
            if head >= 1:
                acc = pv_chunk(head - 1, c, acc)
            cur_sum = exp_chunk(head, c, m, cur_sum)
        if head >= 1:
            even_out = finish(head - 1, acc, prev_sum, even_out)
        prev_sum = cur_sum
        if head + 1 < NA_HEADS:
            m = jnp.max(run, axis=-1, keepdims=True)
    acc = None
    for c in range(n_chunks):
        acc = pv_chunk(NA_HEADS - 1, c, acc)
    finish(NA_HEADS - 1, acc, prev_sum, even_out)


def _natt_call(qa, ka, va, nag, tbl):
    bsz, seq_len, _ = qa.shape
    n_blk = seq_len // NA_TQ

    def pattern(i):
        return jnp.minimum(i, 1) + (i == n_blk - 1).astype(jnp.int32)

    blk = pl.BlockSpec((1, NA_TQ, NA_WIDTH), lambda b, i: (b, i, 0))
    full = pl.BlockSpec((1, seq_len, NA_WIDTH), lambda b, i: (b, 0, 0))
    return pl.pallas_call(
        _natt_kernel,
        grid=(bsz, n_blk),
        in_specs=[blk, full, full, blk,
                  pl.BlockSpec((1, NA_HEADS, NA_TQ, NA_TK), lambda b, i: (pattern(i), 0, 0, 0))],
        out_specs=blk,
        out_shape=jax.ShapeDtypeStruct((bsz, seq_len, NA_WIDTH), jnp.bfloat16),
        scratch_shapes=[pltpu.VMEM((2, NA_TQ, NA_TK), jnp.float32),
                        pltpu.VMEM((2, NA_TQ, NA_TK), jnp.bfloat16)],
        compiler_params=pltpu.CompilerParams(
            dimension_semantics=("arbitrary", "arbitrary"), vmem_limit_bytes=VMEM_LIMIT_BYTES),
        name="natt",
    )(qa, ka, va, nag, tbl)


def _mla_kernel(q_ref, k_ref, vt_ref, gate_ref, o_ref, s_scr, p_scr):
    bf16 = jnp.bfloat16
    seq_len = k_ref.shape[1]
    n_chunks = seq_len // MLA_KC
    n_heads = q_ref.shape[2] // HEAD_PAD
    lo_half = _lane_iota() < MLA_V
    sublanes = 8

    def hcols(head):
        return slice(head * HEAD_PAD, (head + 1) * HEAD_PAD)

    def kslice(c):
        return slice(c * MLA_KC, (c + 1) * MLA_KC)

    def qk_chunk(head, c, run_max):
        s = _dot_nt(k_ref[0, kslice(c), hcols(head)], q_ref[0, :, hcols(head)])
        s_scr[head % 2, kslice(c), :] = s
        part = jnp.max(s.reshape(MLA_KC // sublanes, sublanes, MLA_TQ), axis=0)
        return part if run_max is None else jnp.maximum(run_max, part)

    def exp_chunk(head, c, m):
        p_scr[head % 2, kslice(c), :] = jnp.exp2(s_scr[head % 2, kslice(c), :] - m).astype(bf16)

    def pv_chunk(head, c, acc):
        vrows = slice(head * MLA_VT_ROWS, (head + 1) * MLA_VT_ROWS)
        o = _dot(vt_ref[0, vrows, kslice(c)], p_scr[head % 2, kslice(c), :])
        return o if acc is None else acc + o

    def finish(head, acc, prev):
        pad = jnp.zeros((HEAD_PAD - MLA_VT_ROWS, acc.shape[1]), acc.dtype)
        acc = jnp.concatenate([acc, pad], axis=0).T
        o = acc / acc[:, MLA_V:MLA_V + 1]
        if head % 2 == 0:
            return o
        cols = slice((head // 2) * LANES, (head // 2 + 1) * LANES)
        pair = jnp.where(lo_half, prev, pltpu.roll(o, MLA_V, axis=1))
        o_ref[0, :, cols] = (pair * gate_ref[0, :, cols].astype(jnp.float32)).astype(bf16)
        return None

    run = None
    for c in range(n_chunks):
        run = qk_chunk(0, c, run)
    m = jnp.max(run, axis=0, keepdims=True)
    acc, even_out = None, None
    for head in range(n_heads):
        run, next_acc = None, None
        for c in range(n_chunks):
            if head + 1 < n_heads:
                run = qk_chunk(head + 1, c, run)
            if head >= 1:
                next_acc = pv_chunk(head - 1, c, next_acc)
            exp_chunk(head, c, m)
        if head >= 1:
            even_out = finish(head - 1, next_acc, even_out)
        if head + 1 < n_heads:
            m = jnp.max(run, axis=0, keepdims=True)
    for c in range(n_chunks):
        acc = pv_chunk(n_heads - 1, c, acc)
    finish(n_heads - 1, acc, even_out)


def _mla_call(qf, kf, vbt, mg):
    bsz, seq_len, _ = qf.shape
    n_groups = MLA_HEADS // MLA_GROUP
    assert seq_len % MLA_KC == 0 and seq_len % MLA_TQ == 0 and MLA_GROUP % 2 == 0
    return pl.pallas_call(
        _mla_kernel,
        grid=(bsz, n_groups, seq_len // MLA_TQ),
        in_specs=[pl.BlockSpec((1, MLA_TQ, MLA_GROUP * HEAD_PAD), lambda b, p, j: (b, j, p)),
                  pl.BlockSpec((1, seq_len, MLA_GROUP * HEAD_PAD), lambda b, p, j: (b, 0, p)),
                  pl.BlockSpec((1, MLA_GROUP * MLA_VT_ROWS, seq_len), lambda b, p, j: (b, p, 0)),
                  pl.BlockSpec((1, MLA_TQ, MLA_GROUP * MLA_V), lambda b, p, j: (b, j, p))],
        out_specs=pl.BlockSpec((1, MLA_TQ, MLA_GROUP * MLA_V), lambda b, p, j: (b, j, p)),
        out_shape=jax.ShapeDtypeStruct((bsz, seq_len, MLA_WIDTH), jnp.bfloat16),
        scratch_shapes=[pltpu.VMEM((2, seq_len, MLA_TQ), jnp.float32),
                        pltpu.VMEM((2, seq_len, MLA_TQ), jnp.bfloat16)],
        compiler_params=pltpu.CompilerParams(
            dimension_semantics=("arbitrary", "arbitrary", "arbitrary"),
            vmem_limit_bytes=VMEM_LIMIT_BYTES),
        name="mla",
    )(qf, kf, vbt, mg)


def _run_tasks(tasks):
    pending = None
    for dot_fn, post_fn, barrier in tasks:
        if barrier and pending is not None:
            pending[0](pending[1])
            pending = None
        y = dot_fn()
        if pending is not None:
            pending[0](pending[1])
        pending = (post_fn, y)
    pending[0](pending[1])


def _out_kernel(x_ref, ana_ref, amla_ref, gna_ref, gmla_ref, wna_ref, wmla_ref, wout_ref, o_ref, y_scr):
    f32 = jnp.float32
    row_tiles = [slice(r, r + PROJ_TILE) for r in range(0, o_ref.shape[0], PROJ_TILE)]
    col_blocks = [slice(c, c + PROJ_BLK) for c in range(0, o_ref.shape[1], PROJ_BLK)]

    def branch_dots(rows, cols):
        return lambda: (_dot(ana_ref[rows, :], wna_ref[:, cols]), _dot(amla_ref[rows, :], wmla_ref[:, cols]))

    def merge(rows, cols):
        def post(us):
            y = gna_ref[rows, cols].astype(f32) * us[0] + gmla_ref[rows, cols].astype(f32) * us[1]
            y_scr[rows, cols] = y.astype(jnp.bfloat16)
        return post

    def out_dot(rows, cols):
        return lambda: _dot(y_scr[rows, :], wout_ref[:, cols])

    def residual(rows, cols):
        def post(y):
            o_ref[rows, cols] = x_ref[rows, cols] + y
        return post

    tasks = [(branch_dots(r, c), merge(r, c), False) for r in row_tiles for c in col_blocks]
    tasks += [(out_dot(r, c), residual(r, c), i == 0)
              for i, (r, c) in enumerate((r, c) for r in row_tiles for c in col_blocks)]
    _run_tasks(tasks)


def _out_call(x2d, a_na, a_mla, gna, gmla, lp):
    n_tok = x2d.shape[0]
    tm = OUT_TILE
    assert n_tok % tm == 0 and tm % PROJ_TILE == 0

    def row_spec(width):
        return pl.BlockSpec((tm, width), lambda i: (i, 0))

    consts = [lp["w_o_na"], lp["w_o_mla"], lp["w_out"]]
    return pl.pallas_call(
        _out_kernel,
        grid=(n_tok // tm,),
        in_specs=[row_spec(D_MODEL), row_spec(NA_WIDTH), row_spec(MLA_WIDTH), row_spec(D_MODEL),
                  row_spec(D_MODEL)] + [_const_spec(c.shape) for c in consts],
        out_specs=row_spec(D_MODEL),
        out_shape=jax.ShapeDtypeStruct((n_tok, D_MODEL), jnp.float32),
        scratch_shapes=[pltpu.VMEM((tm, D_MODEL), jnp.bfloat16)],
        compiler_params=pltpu.CompilerParams(
            dimension_semantics=("arbitrary",), vmem_limit_bytes=VMEM_LIMIT_BYTES),
        name="out",
    )(x2d, a_na, a_mla, gna, gmla, *consts)


_N_DROW = 2 * NA_WIN_ROWS - 1
_N_DCOL = 2 * NA_WIN_COLS - 1


def _na_table_static(seq_len):
    rows = seq_len // GRID_W
    n_blk = rows // NA_QROWS
    qc = np.arange(GRID_W)
    kj = np.arange(GRID_W)
    cs = np.clip(qc - NA_WIN_COLS // 2, 0, GRID_W - NA_WIN_COLS)
    col_valid = (kj[None, :] >= cs[:, None]) & (kj[None, :] < cs[:, None] + NA_WIN_COLS)
    dcol = kj[None, :] - qc[:, None] + (NA_WIN_COLS - 1)
    onehot = (dcol[None] == np.arange(_N_DCOL)[:, None, None]) & col_valid[None]
    col_mask = np.where(col_valid, 0.0, MASK_VALUE).astype(np.float32)
    qr = np.arange(NA_QROWS)
    kn = np.arange(NA_BAND)
    idx = []
    for i in (0, 1, n_blk - 1):
        band = int(np.clip(NA_QROWS * i - NA_WIN_ROWS // 2, 0, rows - NA_BAND))
        r = NA_QROWS * i + qr
        rs = np.clip(r - NA_WIN_ROWS // 2, 0, rows - NA_WIN_ROWS)
        krow = band + kn
        valid = (krow[None, :] >= rs[:, None]) & (krow[None, :] < rs[:, None] + NA_WIN_ROWS)
        drow = krow[None, :] - r[:, None] + (NA_WIN_ROWS - 1)
        idx.append(np.where(valid, drow, _N_DROW))
    return onehot.astype(np.float32), col_mask, np.stack(idx).astype(np.int32)


def _na_table(rel_bias, onehot, col_mask, idx):
    blocks = jnp.einsum("hdc,cqj->hdqj", rel_bias, onehot, precision=lax.Precision.HIGHEST) + col_mask
    blocks = jnp.concatenate(
        [blocks, jnp.full((NA_HEADS, 1, GRID_W, GRID_W), MASK_VALUE, jnp.float32)], axis=1) * LOG2E
    tbl = jnp.take(blocks, idx.reshape(-1), axis=1)
    tbl = tbl.reshape(NA_HEADS, 3, NA_QROWS, NA_BAND, GRID_W, GRID_W)
    return tbl.transpose(1, 0, 2, 4, 3, 5).reshape(3, NA_HEADS, NA_TQ, NA_TK)


def _rope_tables(seq_len):
    t = jnp.arange(seq_len)
    row = (t // GRID_W).astype(jnp.float32)
    col = (t % GRID_W).astype(jnp.float32)
    half = MLA_ROPE // 2
    n_freq = half // 2
    inv = jnp.power(jnp.float32(ROPE_BASE), -jnp.arange(n_freq, dtype=jnp.float32) / n_freq)
    ang = jnp.concatenate([row[:, None] * inv, col[:, None] * inv], axis=-1)
    return jnp.cos(ang), jnp.sin(ang)


def _place(cols_to_blocks, total):
    parts = []
    for arr, width in cols_to_blocks:
        pad = width - arr.shape[-1]
        parts.append(jnp.pad(arr, [(0, 0)] * (arr.ndim - 1) + [(0, pad)]) if pad else arr)
    out = jnp.concatenate(parts, axis=-1)
    assert out.shape[-1] == total
    return out


def _layer_params(l, p, cos, sin):
    f32, bf16 = jnp.float32, jnp.bfloat16
    half = MLA_ROPE // 2
    w_in = p["w_in"][l]
    o = 0
    pieces = {}
    for name, size in (("naq", NA_WIDTH), ("nak", NA_WIDTH), ("nav", NA_WIDTH), ("nag", NA_WIDTH),
                       ("cq", MLA_Q_LORA), ("ckv", MLA_KV_LORA), ("kpe", MLA_ROPE),
                       ("mg", MLA_WIDTH), ("gna", D_MODEL), ("gmla", D_MODEL)):
        pieces[name] = w_in[:, o:o + size]
        o += size
    kpe = pieces["kpe"]
    kpe_sw = jnp.concatenate([kpe[:, half:], kpe[:, :half]], axis=-1)
    ckvx = _place([(pieces["ckv"], LANES), (jnp.concatenate([kpe, kpe_sw], axis=-1), LANES)], 2 * LANES)
    w_in_packed = jnp.concatenate(
        [pieces["naq"], pieces["nak"], pieces["nav"], pieces["nag"], pieces["cq"], ckvx,
         pieces["mg"], pieces["gna"], pieces["gmla"]], axis=-1).astype(bf16)
    assert w_in_packed.shape[-1] == _W_IN_COLS

    w_uq = p["w_uq"][l].reshape(MLA_Q_LORA, MLA_HEADS, MLA_QK_DIM)
    nope, x1, x2 = w_uq[..., :MLA_NOPE], w_uq[..., MLA_NOPE:MLA_NOPE + half], w_uq[..., MLA_NOPE + half:]
    zq = jnp.zeros_like(nope)
    wq = _place([(jnp.concatenate([nope, x1, x2], -1), HEAD_PAD)], HEAD_PAD)
    wqs = _place([(jnp.concatenate([zq, x2, x1], -1), HEAD_PAD)], HEAD_PAD)
    wq = wq.reshape(MLA_Q_LORA, MLA_PAD_WIDTH).astype(bf16)
    wqs = wqs.reshape(MLA_Q_LORA, MLA_PAD_WIDTH).astype(bf16)

    w_ukv = p["w_ukv"][l].reshape(MLA_KV_LORA, MLA_HEADS, MLA_NOPE + MLA_V)
    k_nope = _place([(w_ukv[..., :MLA_NOPE], HEAD_PAD)], HEAD_PAD).reshape(MLA_KV_LORA, MLA_PAD_WIDTH)
    eye = jnp.eye(MLA_ROPE, dtype=f32)
    put = jnp.pad(eye, ((0, 0), (MLA_NOPE, HEAD_PAD - MLA_QK_DIM)))
    put = jnp.tile(put, (1, MLA_HEADS))
    wk = jnp.concatenate([k_nope, put, put, jnp.zeros((2 * LANES - MLA_KV_LORA - 2 * MLA_ROPE,
                                                        MLA_PAD_WIDTH), f32)], axis=0).astype(bf16)
    wvt = _place([(w_ukv[..., MLA_NOPE:], MLA_VT_ROWS)], MLA_VT_ROWS)
    wvt = wvt.reshape(MLA_KV_LORA, MLA_HEADS * MLA_VT_ROWS).T.astype(bf16)

    gq = p["mla_q_norm"][l]
    gk = p["mla_k_norm"][l]
    scale = MLA_QK_DIM ** -0.5 * LOG2E
    seq_len = cos.shape[0]
    ones = jnp.ones((seq_len, 1), f32)
    tq1 = _place([(ones * gq[None, :MLA_NOPE], MLA_NOPE),
                  (cos * gq[None, MLA_NOPE:MLA_NOPE + half], half),
                  (cos * gq[None, MLA_NOPE + half:], half)], MLA_QK_DIM) * scale
    tq2 = _place([(jnp.zeros((seq_len, MLA_NOPE), f32), MLA_NOPE),
                  (-sin * gq[None, MLA_NOPE + half:], half),
                  (sin * gq[None, MLA_NOPE:MLA_NOPE + half], half)], MLA_QK_DIM) * scale
    e1 = jnp.concatenate([cos * gk[None, MLA_NOPE:MLA_NOPE + half], cos * gk[None, MLA_NOPE + half:]], -1)
    e2 = jnp.concatenate([-sin * gk[None, MLA_NOPE + half:], sin * gk[None, MLA_NOPE:MLA_NOPE + half]], -1)
    tabs = {"tq1": _place([(tq1, LANES)], LANES), "tq2": _place([(tq2, LANES)], LANES),
            "e1": _place([(e1, LANES)], LANES), "e2": _place([(e2, LANES)], LANES)}
    gkf = _place([(gk[None, :MLA_NOPE], MLA_NOPE), (jnp.ones((1, MLA_ROPE), f32), LANES - MLA_NOPE)], LANES)

    na_scale = NA_HEAD_DIM ** -0.5 * LOG2E
    lp = {
        "ln_g": p["ln_g"][l][None, :],
        "w_in": w_in_packed,
        "gq2": jnp.tile(p["na_q_norm"][l], 2)[None, :] * na_scale,
        "gk2": jnp.tile(p["na_k_norm"][l], 2)[None, :],
        "gcq": p["mla_cq_norm"][l][None, :],
        "gckv": p["mla_ckv_norm"][l][None, :],
        "wq": wq, "wqs": wqs, "wk": wk, "wvt": wvt, "gkf": gkf,
        "w_o_na": p["w_o_na"][l].astype(bf16),
        "w_o_mla": p["w_o_mla"][l].astype(bf16),
        "w_out": p["w_out"][l].astype(bf16),
    }
    return lp, tabs


@jax.jit
def _forward(x, p):
    bsz, seq_len, _ = x.shape
    depth = p["w_in"].shape[0]
    cos, sin = _rope_tables(seq_len)
    onehot, col_mask, tbl_idx = _na_table_static(seq_len)
    x2d = x.reshape(bsz * seq_len, D_MODEL)
    for l in range(depth):
        lp, tabs = _layer_params(l, p, cos, sin)
        tbl = _na_table(p["na_rel_bias"][l], onehot, col_mask, tbl_idx)
        qa, ka, va, nag, qf, kf, mg, gna, gmla, vbt = _proj_call(x2d, lp, tabs, seq_len)

        def b3(a):
            return a.reshape(bsz, seq_len, a.shape[-1])

        a_na = _natt_call(b3(qa), b3(ka), b3(va), b3(nag), tbl)
        a_mla = _mla_call(b3(qf), b3(kf), vbt, b3(mg))
        x2d = _out_call(x2d, a_na.reshape(-1, NA_WIDTH), a_mla.reshape(-1, MLA_WIDTH), gna, gmla, lp)
    return x2d.reshape(bsz, seq_len, D_MODEL)


def kernel(x, ln_g, w_in, na_q_norm, na_k_norm, na_rel_bias, mla_cq_norm, mla_ckv_norm, w_uq, w_ukv,
           mla_q_norm, mla_k_norm, w_o_na, w_o_mla, w_out):
    params = dict(ln_g=ln_g, w_in=w_in, na_q_norm=na_q_norm, na_k_norm=na_k_norm,
                  na_rel_bias=na_rel_bias, mla_cq_norm=mla_cq_norm, mla_ckv_norm=mla_ckv_norm,
                  w_uq=w_uq, w_ukv=w_ukv, mla_q_norm=mla_q_norm, mla_k_norm=mla_k_norm,
                  w_o_na=w_o_na, w_o_mla=w_o_mla, w_out=w_out)
    return _forward(x, params)
```

```python
import functools

import numpy as np
import jax
import jax.numpy as jnp
from jax import lax
from jax.experimental import pallas as pl
from jax.experimental.pallas import tpu as pltpu

D_MODEL = 1024
GRID_W = 64
NA_HEADS = 8
NA_HEAD_DIM = 64
NA_WIDTH = NA_HEADS * NA_HEAD_DIM
NA_WIN_ROWS = 8
NA_WIN_COLS = 16
MLA_HEADS = 8
MLA_NOPE = 64
MLA_ROPE = 32
MLA_QK_DIM = MLA_NOPE + MLA_ROPE
MLA_V = 64
MLA_Q_LORA = 256
MLA_KV_LORA = 128
MLA_WIDTH = MLA_HEADS * MLA_V
ROPE_BASE = 10000.0
EPS = 1e-6

LANES = 128
HEAD_PAD = LANES
MLA_PAD_WIDTH = MLA_HEADS * HEAD_PAD
MLA_VT_ROWS = 128
VMEM_LIMIT_BYTES = 56 * 1024 * 1024

_OFF_NAQ = 0
_OFF_NAK = _OFF_NAQ + NA_WIDTH
_OFF_NAV = _OFF_NAK + NA_WIDTH
_OFF_NAG = _OFF_NAV + NA_WIDTH
_OFF_CQ = _OFF_NAG + NA_WIDTH
_OFF_CKV = _OFF_CQ + MLA_Q_LORA
_OFF_MG = _OFF_CKV + 2 * LANES
_OFF_GNA = _OFF_MG + MLA_WIDTH
_OFF_GMLA = _OFF_GNA + D_MODEL
_W_IN_COLS = _OFF_GMLA + D_MODEL

PROJ_TILE = 512
OUT_TILE = 1024
PROJ_BLK = 256
NA_QROWS = 4
NA_BAND = 12
NA_TQ = NA_QROWS * GRID_W
NA_TK = NA_BAND * GRID_W
NA_KC = 768
MLA_TQ = 256
MLA_KC = 1024
MLA_GROUP = 8
MASK_VALUE = -1e30
LOG2E = 1.4426950408889634

_NT_DIMS = (((1,), (1,)), ((), ()))


def _dot(a, b):
    return jnp.dot(a, b, preferred_element_type=jnp.float32)


def _dot_nt(a, b):
    return lax.dot_general(a, b, _NT_DIMS, preferred_element_type=jnp.float32)


def _lane_iota():
    return lax.broadcasted_iota(jnp.int32, (1, LANES), 1)


def _proj_kernel(x_ref, lng_ref, w_ref, gq_ref, gk_ref, gcq_ref, gckv_ref, wq_ref, wqs_ref,
                 wk_ref, wvt_ref, gkf_ref, tq1_ref, tq2_ref, e1_ref, e2_ref,
                 qa_ref, ka_ref, va_ref, nag_ref, qf_ref, kf_ref, mg_ref,
                 gna_ref, gmla_ref, vbt_ref, hb_scr, cqn_scr, lhsk_scr, sspe_scr):
    bf16 = jnp.bfloat16
    f32 = jnp.float32
    x = x_ref[...]
    hb_scr[...] = (x * lax.rsqrt(jnp.mean(x * x, axis=-1, keepdims=True) + EPS) * lng_ref[...]).astype(bf16)
    lo_half = _lane_iota() < NA_HEAD_DIM
    rope_lanes = _lane_iota() < MLA_ROPE
    ones_row = lax.broadcasted_iota(jnp.int32, (MLA_VT_ROWS, 1), 0) == MLA_V

    def tiles(y):
        return [y[:, i * LANES:(i + 1) * LANES] for i in range(y.shape[1] // LANES)]

    def in_dot(off):
        return lambda: _dot(hb_scr[...], w_ref[:, off:off + PROJ_BLK])

    def headnorm64(out_ref, g_ref, col0):
        def post(y):
            for i, pc in enumerate(tiles(y)):
                sq = pc * pc
                s_lo = jnp.sum(jnp.where(lo_half, sq, 0.0), axis=-1, keepdims=True)
                s_hi = jnp.sum(jnp.where(lo_half, 0.0, sq), axis=-1, keepdims=True)
                r = jnp.where(lo_half,
                              lax.rsqrt(s_lo * (1.0 / NA_HEAD_DIM) + EPS),
                              lax.rsqrt(s_hi * (1.0 / NA_HEAD_DIM) + EPS))
                out_ref[:, col0 + i * LANES:col0 + (i + 1) * LANES] = (pc * r * g_ref[...]).astype(bf16)
        return post

    def elementwise(out_ref, col0, fn):
        def post(y):
            out_ref[:, col0:col0 + PROJ_BLK] = fn(y).astype(bf16)
        return post

    def silu(g):
        return g * jax.nn.sigmoid(g)

    def post_cq(y):
        cqn_scr[...] = (y * lax.rsqrt(jnp.mean(y * y, axis=-1, keepdims=True) + EPS) * gcq_ref[...]).astype(bf16)

    def post_ckv(y):
        ckv, e = tiles(y)
        lhsk_scr[:, :LANES] = (ckv * lax.rsqrt(jnp.mean(ckv * ckv, axis=-1, keepdims=True) + EPS)
                               * gckv_ref[...]).astype(bf16)
        ss_pe = jnp.sum(jnp.where(rope_lanes, e * e, 0.0), axis=-1, keepdims=True)
        sspe_scr[...] = jnp.broadcast_to(ss_pe, sspe_scr.shape)
        rot = e * e1_ref[...] + pltpu.roll(e, LANES - MLA_ROPE, axis=1) * e2_ref[...]
        rot_hi = rot.astype(bf16).astype(f32)
        rot_lo = (rot - rot_hi).astype(bf16).astype(f32)
        lhsk_scr[:, LANES:] = (rot_hi + pltpu.roll(rot_lo, MLA_ROPE, axis=1)).astype(bf16)

    def q_dot(pair):
        cols = slice(pair * PROJ_BLK, (pair + 1) * PROJ_BLK)
        return lambda: (_dot(cqn_scr[...], wq_ref[:, cols]), _dot(cqn_scr[...], wqs_ref[:, cols]))

    def q_post(pair):
        def post(ys):
            for i, (a, b) in enumerate(zip(tiles(ys[0]), tiles(ys[1]))):
                r = lax.rsqrt(jnp.sum(a * a, axis=-1, keepdims=True) / MLA_QK_DIM + EPS)
                col0 = pair * PROJ_BLK + i * HEAD_PAD
                qf_ref[:, col0:col0 + HEAD_PAD] = ((a * tq1_ref[...] + b * tq2_ref[...]) * r).astype(bf16)
        return post

    def k_dot(pair):
        cols = slice(pair * PROJ_BLK, (pair + 1) * PROJ_BLK)
        return lambda: _dot(lhsk_scr[...], wk_ref[:, cols])

    def k_post(pair):
        def post(y):
            for i, a in enumerate(tiles(y)):
                ss = jnp.sum(jnp.where(lo_half, a * a, 0.0), axis=-1, keepdims=True) + sspe_scr[...]
                r = lax.rsqrt(ss / MLA_QK_DIM + EPS)
                col0 = pair * PROJ_BLK + i * HEAD_PAD
                kf_ref[:, col0:col0 + HEAD_PAD] = (a * gkf_ref[...] * r).astype(bf16)
        return post

    def v_dot(pair):
        rows = slice(pair * 2 * MLA_VT_ROWS, (pair + 1) * 2 * MLA_VT_ROWS)
        return lambda: _dot_nt(wvt_ref[rows, :], lhsk_scr[:, :LANES])

    def v_post(pair):
        def post(y):
            for i in range(2):
                row0 = (2 * pair + i) * MLA_VT_ROWS
                a = y[i * MLA_VT_ROWS:(i + 1) * MLA_VT_ROWS, :]
                vbt_ref[0, row0:row0 + MLA_VT_ROWS, :] = jnp.where(ones_row, 1.0, a).astype(bf16)
        return post

    def in_blocks(off, width, make_post):
        return [(in_dot(off + c), make_post(c)) for c in range(0, width, PROJ_BLK)]

    tasks = [(in_dot(_OFF_CQ), post_cq), (in_dot(_OFF_CKV), post_ckv)]
    tasks += in_blocks(_OFF_NAQ, NA_WIDTH, lambda c: headnorm64(qa_ref, gq_ref, c))
    tasks += in_blocks(_OFF_NAK, NA_WIDTH, lambda c: headnorm64(ka_ref, gk_ref, c))
    tasks += [(q_dot(p), q_post(p)) for p in range(MLA_PAD_WIDTH // PROJ_BLK)]
    tasks += in_blocks(_OFF_GNA, D_MODEL, lambda c: elementwise(gna_ref, c, jax.nn.sigmoid))
    tasks += [(k_dot(p), k_post(p)) for p in range(MLA_PAD_WIDTH // PROJ_BLK)]
    tasks += in_blocks(_OFF_NAG, NA_WIDTH, lambda c: elementwise(nag_ref, c, silu))
    tasks += [(v_dot(p), v_post(p)) for p in range(MLA_PAD_WIDTH // PROJ_BLK)]
    tasks += in_blocks(_OFF_MG, MLA_WIDTH, lambda c: elementwise(mg_ref, c, silu))
    tasks += in_blocks(_OFF_GMLA, D_MODEL, lambda c: elementwise(gmla_ref, c, jax.nn.sigmoid))
    tasks += in_blocks(_OFF_NAV, NA_WIDTH, lambda c: elementwise(va_ref, c, lambda y: y))

    _run_tasks([(dot_fn, post_fn, False) for dot_fn, post_fn in tasks])


def _const_spec(stacked, layer):
    rest = stacked.shape[1:]
    return pl.BlockSpec((None,) + rest, lambda i: (layer,) + (0,) * len(rest), pipeline_mode=pl.Buffered(1))


def _proj_call(x2d, lp, tabs, layer, seq_len):
    n_tok = x2d.shape[0]
    tm = PROJ_TILE
    assert n_tok % tm == 0 and seq_len % tm == 0
    tiles_per_seq = seq_len // tm
    bf16 = jnp.bfloat16

    def row_spec(width):
        return pl.BlockSpec((tm, width), lambda i: (i, 0))

    def tab_spec():
        return pl.BlockSpec((None, tm, LANES), lambda i: (layer, i % tiles_per_seq, 0))

    consts = [lp["ln_g"], lp["w_in"], lp["gq2"], lp["gk2"], lp["gcq"], lp["gckv"],
              lp["wq"], lp["wqs"], lp["wk"], lp["wvt"], lp["gkf"]]
    out_widths = [NA_WIDTH, NA_WIDTH, NA_WIDTH, NA_WIDTH, MLA_PAD_WIDTH, MLA_PAD_WIDTH,
                  MLA_WIDTH, D_MODEL, D_MODEL]
    t_widths = [MLA_HEADS * MLA_VT_ROWS]
    t_specs = [pl.BlockSpec((1, w, tm), lambda i: (i // tiles_per_seq, 0, i % tiles_per_seq)) for w in t_widths]
    t_shapes = [jax.ShapeDtypeStruct((n_tok // seq_len, w, seq_len), bf16) for w in t_widths]
    return pl.pallas_call(
        _proj_kernel,
        grid=(n_tok // tm,),
        in_specs=[row_spec(D_MODEL)] + [_const_spec(c, layer) for c in consts] + [tab_spec()] * 4,
        out_specs=[row_spec(w) for w in out_widths] + t_specs,
        out_shape=[jax.ShapeDtypeStruct((n_tok, w), bf16) for w in out_widths] + t_shapes,
        scratch_shapes=[pltpu.VMEM((tm, D_MODEL), bf16),
                        pltpu.VMEM((tm, MLA_Q_LORA), bf16),
                        pltpu.VMEM((tm, 2 * LANES), bf16),
                        pltpu.VMEM((tm, LANES), jnp.float32)],
        compiler_params=pltpu.CompilerParams(
            dimension_semantics=("arbitrary",), vmem_limit_bytes=VMEM_LIMIT_BYTES),
        name="proj",
    )(x2d, *consts, tabs["tq1"], tabs["tq2"], tabs["e1"], tabs["e2"])


def _na_band_start(i, rows):
    return jnp.clip(NA_QROWS * i - NA_WIN_ROWS // 2, 0, rows - NA_BAND)


def _natt_kernel(q_ref, k_ref, v_ref, gate_ref, tbl_ref, o_ref, s_scr, p_scr):
    bf16 = jnp.bfloat16
    i = pl.program_id(1)
    start = pl.multiple_of(_na_band_start(i, k_ref.shape[1] // GRID_W) * GRID_W, GRID_W)
    lo_half = _lane_iota() < NA_HEAD_DIM
    n_chunks = NA_TK // NA_KC

    def pcols(head):
        return slice((head // 2) * LANES, (head // 2 + 1) * LANES)

    def kslice(c):
        return slice(c * NA_KC, (c + 1) * NA_KC)

    def fold(x, run, op):
        for t in range(x.shape[1] // LANES):
            tile = x[:, t * LANES:(t + 1) * LANES]
            run = tile if run is None else op(run, tile)
        return run

    def masked_q(head):
        qp = q_ref[0, :, pcols(head)]
        sel = lo_half if head % 2 == 0 else jnp.logical_not(lo_half)
        return jnp.where(sel, qp, jnp.zeros_like(qp))

    def qk_chunk(head, qm, c, run_max):
        keys = k_ref[0, pl.ds(start + c * NA_KC, NA_KC), pcols(head)]
        s = _dot_nt(qm, keys) + tbl_ref[0, head, :, kslice(c)]
        s_scr[head % 2, :, kslice(c)] = s
        return fold(s, run_max, jnp.maximum)

    def exp_chunk(head, c, m, run_sum):
        p = jnp.exp2(s_scr[head % 2, :, kslice(c)] - m)
        p_scr[head % 2, :, kslice(c)] = p.astype(bf16)
        return fold(p, run_sum, jnp.add)

    def pv_chunk(head, c, acc):
        vals = v_ref[0, pl.ds(start + c * NA_KC, NA_KC), pcols(head)]
        o = _dot(p_scr[head % 2, :, kslice(c)], vals)
        return o if acc is None else acc + o

    def finish(head, acc, run_sum, prev):
        o = acc / jnp.sum(run_sum, axis=-1, keepdims=True)
        if head % 2 == 0:
            return o
        pair = jnp.where(lo_half, prev, o)
        o_ref[0, :, pcols(head)] = (pair * gate_ref[0, :, pcols(head)].astype(jnp.float32)).astype(bf16)
        return None

    qm = masked_q(0)
    run = None
    for c in range(n_chunks):
        run = qk_chunk(0, qm, c, run)
    m = jnp.max(run, axis=-1, keepdims=True)
    prev_sum, even_out = None, None
    for head in range(NA_HEADS):
        run, acc, cur_sum = None, None, None
        if head + 1 < NA_HEADS:
            qm = masked_q(head + 1)
        for c in range(n_chunks):
            if head + 1 < NA_HEADS:
                run = qk_chunk(head + 1, qm, c, run)
            if head >= 1:
                acc = pv_chunk(head - 1, c, acc)
            cur_sum = exp_chunk(head, c, m, cur_sum)
        if head >= 1:
            even_out = finish(head - 1, acc, prev_sum, even_out)
        prev_sum = cur_sum
        if head + 1 < NA_HEADS:
            m = jnp.max(run, axis=-1, keepdims=True)
    acc = None
    for c in range(n_chunks):
        acc = pv_chunk(NA_HEADS - 1, c, acc)
    finish(NA_HEADS - 1, acc, prev_sum, even_out)


def _natt_call(qa, ka, va, nag, tbl, layer):
    bsz, seq_len, _ = qa.shape
    n_blk = seq_len // NA_TQ

    def pattern(i):
        return jnp.minimum(i, 1) + (i == n_blk - 1).astype(jnp.int32)

    blk = pl.BlockSpec((1, NA_TQ, NA_WIDTH), lambda b, i: (b, i, 0))
    full = pl.BlockSpec((1, seq_len, NA_WIDTH), lambda b, i: (b, 0, 0))
    return pl.pallas_call(
        _natt_kernel,
        grid=(bsz, n_blk),
        in_specs=[blk, full, full, blk,
                  pl.BlockSpec((None, 1, NA_HEADS, NA_TQ, NA_TK),
                               lambda b, i: (layer, pattern(i), 0, 0, 0))],
        out_specs=blk,
        out_shape=jax.ShapeDtypeStruct((bsz, seq_len, NA_WIDTH), jnp.bfloat16),
        scratch_shapes=[pltpu.VMEM((2, NA_TQ, NA_TK), jnp.float32),
                        pltpu.VMEM((2, NA_TQ, NA_TK), jnp.bfloat16)],
        compiler_params=pltpu.CompilerParams(
            dimension_semantics=("arbitrary", "arbitrary"), vmem_limit_bytes=VMEM_LIMIT_BYTES),
        name="natt",
    )(qa, ka, va, nag, tbl)


def _mla_kernel(q_ref, k_ref, vt_ref, gate_ref, o_ref, s_scr, p_scr):
    bf16 = jnp.bfloat16
    seq_len = k_ref.shape[1]
    n_chunks = seq_len // MLA_KC
    n_heads = q_ref.shape[2] // HEAD_PAD
    lo_half = _lane_iota() < MLA_V
    sublanes = 8

    def hcols(head):
        return slice(head * HEAD_PAD, (head + 1) * HEAD_PAD)

    def kslice(c):
        return slice(c * MLA_KC, (c + 1) * MLA_KC)

    def qk_chunk(head, c, run_max):
        s = _dot_nt(k_ref[0, kslice(c), hcols(head)], q_ref[0, :, hcols(head)])
        s_scr[head % 2, kslice(c), :] = s
        part = jnp.max(s.reshape(MLA_KC // sublanes, sublanes, MLA_TQ), axis=0)
        return part if run_max is None else jnp.maximum(run_max, part)

    def exp_chunk(head, c, m):
        p_scr[head % 2, kslice(c), :] = jnp.exp2(s_scr[head % 2, kslice(c), :] - m).astype(bf16)

    def pv_chunk(head, c, acc):
        vrows = slice(head * MLA_VT_ROWS, (head + 1) * MLA_VT_ROWS)
        o = _dot(vt_ref[0, vrows, kslice(c)], p_scr[head % 2, kslice(c), :])
        return o if acc is None else acc + o

    def finish(head, acc, prev):
        if MLA_VT_ROWS < HEAD_PAD:
            pad = jnp.zeros((HEAD_PAD - MLA_VT_ROWS, acc.shape[1]), acc.dtype)
            acc = jnp.concatenate([acc, pad], axis=0)
        acc = acc.T
        o = acc / acc[:, MLA_V:MLA_V + 1]
        if head % 2 == 0:
            return o
        cols = slice((head // 2) * LANES, (head // 2 + 1) * LANES)
        pair = jnp.where(lo_half, prev, pltpu.roll(o, MLA_V, axis=1))
        o_ref[0, :, cols] = (pair * gate_ref[0, :, cols].astype(jnp.float32)).astype(bf16)
        return None

    run = None
    for c in range(n_chunks):
        run = qk_chunk(0, c, run)
    m = jnp.max(run, axis=0, keepdims=True)
    acc, even_out = None, None
    for head in range(n_heads):
        run, next_acc = None, None
        for c in range(n_chunks):
            if head + 1 < n_heads:
                run = qk_chunk(head + 1, c, run)
            if head >= 1:
                next_acc = pv_chunk(head - 1, c, next_acc)
            exp_chunk(head, c, m)
        if head >= 1:
            even_out = finish(head - 1, next_acc, even_out)
        if head + 1 < n_heads:
            m = jnp.max(run, axis=0, keepdims=True)
    for c in range(n_chunks):
        acc = pv_chunk(n_heads - 1, c, acc)
    finish(n_heads - 1, acc, even_out)


def _mla_call(qf, kf, vbt, mg):
    bsz, seq_len, _ = qf.shape
    n_groups = MLA_HEADS // MLA_GROUP
    assert seq_len % MLA_KC == 0 and seq_len % MLA_TQ == 0 and MLA_GROUP % 2 == 0
    return pl.pallas_call(
        _mla_kernel,
        grid=(bsz, n_groups, seq_len // MLA_TQ),
        in_specs=[pl.BlockSpec((1, MLA_TQ, MLA_GROUP * HEAD_PAD), lambda b, p, j: (b, j, p)),
                  pl.BlockSpec((1, seq_len, MLA_GROUP * HEAD_PAD), lambda b, p, j: (b, 0, p)),
                  pl.BlockSpec((1, MLA_GROUP * MLA_VT_ROWS, seq_len), lambda b, p, j: (b, p, 0)),
                  pl.BlockSpec((1, MLA_TQ, MLA_GROUP * MLA_V), lambda b, p, j: (b, j, p))],
        out_specs=pl.BlockSpec((1, MLA_TQ, MLA_GROUP * MLA_V), lambda b, p, j: (b, j, p)),
        out_shape=jax.ShapeDtypeStruct((bsz, seq_len, MLA_WIDTH), jnp.bfloat16),
        scratch_shapes=[pltpu.VMEM((2, seq_len, MLA_TQ), jnp.float32),
                        pltpu.VMEM((2, seq_len, MLA_TQ), jnp.bfloat16)],
        compiler_params=pltpu.CompilerParams(
            dimension_semantics=("arbitrary", "arbitrary", "arbitrary"),
            vmem_limit_bytes=VMEM_LIMIT_BYTES),
        name="mla",
    )(qf, kf, vbt, mg)


def _run_tasks(tasks):
    pending = None
    for dot_fn, post_fn, barrier in tasks:
        if barrier and pending is not None:
            pending[0](pending[1])
            pending = None
        y = dot_fn()
        if pending is not None:
            pending[0](pending[1])
        pending = (post_fn, y)
    pending[0](pending[1])


def _out_kernel(x_ref, ana_ref, amla_ref, gna_ref, gmla_ref, wna_ref, wmla_ref, wout_ref, o_ref, y_scr):
    f32 = jnp.float32
    row_tiles = [slice(r, r + PROJ_TILE) for r in range(0, o_ref.shape[0], PROJ_TILE)]
    col_blocks = [slice(c, c + PROJ_BLK) for c in range(0, o_ref.shape[1], PROJ_BLK)]

    def branch_dots(rows, cols):
        return lambda: (_dot(ana_ref[rows, :], wna_ref[:, cols]), _dot(amla_ref[rows, :], wmla_ref[:, cols]))

    def merge(rows, cols):
        def post(us):
            y = gna_ref[rows, cols].astype(f32) * us[0] + gmla_ref[rows, cols].astype(f32) * us[1]
            y_scr[rows, cols] = y.astype(jnp.bfloat16)
        return post

    def out_dot(rows, cols):
        return lambda: _dot(y_scr[rows, :], wout_ref[:, cols])

    def residual(rows, cols):
        def post(y):
            o_ref[rows, cols] = x_ref[rows, cols] + y
        return post

    tasks = [(branch_dots(r, c), merge(r, c), False) for r in row_tiles for c in col_blocks]
    tasks += [(out_dot(r, c), residual(r, c), i == 0)
              for i, (r, c) in enumerate((r, c) for r in row_tiles for c in col_blocks)]
    _run_tasks(tasks)


def _out_call(x2d, a_na, a_mla, gna, gmla, lp, layer):
    n_tok = x2d.shape[0]
    tm = OUT_TILE
    assert n_tok % tm == 0 and tm % PROJ_TILE == 0

    def row_spec(width):
        return pl.BlockSpec((tm, width), lambda i: (i, 0))

    consts = [lp["w_o_na"], lp["w_o_mla"], lp["w_out"]]
    return pl.pallas_call(
        _out_kernel,
        grid=(n_tok // tm,),
        in_specs=[row_spec(D_MODEL), row_spec(NA_WIDTH), row_spec(MLA_WIDTH), row_spec(D_MODEL),
                  row_spec(D_MODEL)] + [_const_spec(c, layer) for c in consts],
        out_specs=row_spec(D_MODEL),
        out_shape=jax.ShapeDtypeStruct((n_tok, D_MODEL), jnp.float32),
        scratch_shapes=[pltpu.VMEM((tm, D_MODEL), jnp.bfloat16)],
        compiler_params=pltpu.CompilerParams(
            dimension_semantics=("arbitrary",), vmem_limit_bytes=VMEM_LIMIT_BYTES),
        name="out",
    )(x2d, a_na, a_mla, gna, gmla, *consts)


_N_DROW = 2 * NA_WIN_ROWS - 1
_N_DCOL = 2 * NA_WIN_COLS - 1


def _na_table_static(seq_len):
    rows = seq_len // GRID_W
    n_blk = rows // NA_QROWS
    qc = np.arange(GRID_W)
    kj = np.arange(GRID_W)
    cs = np.clip(qc - NA_WIN_COLS // 2, 0, GRID_W - NA_WIN_COLS)
    col_valid = (kj[None, :] >= cs[:, None]) & (kj[None, :] < cs[:, None] + NA_WIN_COLS)
    dcol = kj[None, :] - qc[:, None] + (NA_WIN_COLS - 1)
    onehot = (dcol[None] == np.arange(_N_DCOL)[:, None, None]) & col_valid[None]
    col_mask = np.where(col_valid, 0.0, MASK_VALUE).astype(np.float32)
    qr = np.arange(NA_QROWS)
    kn = np.arange(NA_BAND)
    idx = []
    for i in (0, 1, n_blk - 1):
        band = int(np.clip(NA_QROWS * i - NA_WIN_ROWS // 2, 0, rows - NA_BAND))
        r = NA_QROWS * i + qr
        rs = np.clip(r - NA_WIN_ROWS // 2, 0, rows - NA_WIN_ROWS)
        krow = band + kn
        valid = (krow[None, :] >= rs[:, None]) & (krow[None, :] < rs[:, None] + NA_WIN_ROWS)
        drow = krow[None, :] - r[:, None] + (NA_WIN_ROWS - 1)
        idx.append(np.where(valid, drow, _N_DROW))
    return onehot.astype(np.float32), col_mask, np.stack(idx).astype(np.int32)


def _na_table(rel_bias, onehot, col_mask, idx):
    blocks = jnp.einsum("hdc,cqj->hdqj", rel_bias, onehot, precision=lax.Precision.HIGHEST) + col_mask
    blocks = jnp.concatenate(
        [blocks, jnp.full((NA_HEADS, 1, GRID_W, GRID_W), MASK_VALUE, jnp.float32)], axis=1) * LOG2E
    tbl = jnp.take(blocks, idx.reshape(-1), axis=1)
    tbl = tbl.reshape(NA_HEADS, 3, NA_QROWS, NA_BAND, GRID_W, GRID_W)
    return tbl.transpose(1, 0, 2, 4, 3, 5).reshape(3, NA_HEADS, NA_TQ, NA_TK)


def _rope_tables(seq_len):
    t = jnp.arange(seq_len)
    row = (t // GRID_W).astype(jnp.float32)
    col = (t % GRID_W).astype(jnp.float32)
    half = MLA_ROPE // 2
    n_freq = half // 2
    inv = jnp.power(jnp.float32(ROPE_BASE), -jnp.arange(n_freq, dtype=jnp.float32) / n_freq)
    ang = jnp.concatenate([row[:, None] * inv, col[:, None] * inv], axis=-1)
    return jnp.cos(ang), jnp.sin(ang)


def _place(cols_to_blocks, total):
    parts = []
    for arr, width in cols_to_blocks:
        pad = width - arr.shape[-1]
        parts.append(jnp.pad(arr, [(0, 0)] * (arr.ndim - 1) + [(0, pad)]) if pad else arr)
    out = jnp.concatenate(parts, axis=-1)
    assert out.shape[-1] == total
    return out


def _layer_params(p, cos, sin):
    f32, bf16 = jnp.float32, jnp.bfloat16
    half = MLA_ROPE // 2
    w_in = p["w_in"]
    o = 0
    pieces = {}
    for name, size in (("naq", NA_WIDTH), ("nak", NA_WIDTH), ("nav", NA_WIDTH), ("nag", NA_WIDTH),
                       ("cq", MLA_Q_LORA), ("ckv", MLA_KV_LORA), ("kpe", MLA_ROPE),
                       ("mg", MLA_WIDTH), ("gna", D_MODEL), ("gmla", D_MODEL)):
        pieces[name] = w_in[:, o:o + size]
        o += size
    kpe = pieces["kpe"]
    kpe_sw = jnp.concatenate([kpe[:, half:], kpe[:, :half]], axis=-1)
    ckvx = _place([(pieces["ckv"], LANES), (jnp.concatenate([kpe, kpe_sw], axis=-1), LANES)], 2 * LANES)
    w_in_packed = jnp.concatenate(
        [pieces["naq"], pieces["nak"], pieces["nav"], pieces["nag"], pieces["cq"], ckvx,
         pieces["mg"], pieces["gna"], pieces["gmla"]], axis=-1).astype(bf16)
    assert w_in_packed.shape[-1] == _W_IN_COLS

    w_uq = p["w_uq"].reshape(MLA_Q_LORA, MLA_HEADS, MLA_QK_DIM)
    nope, x1, x2 = w_uq[..., :MLA_NOPE], w_uq[..., MLA_NOPE:MLA_NOPE + half], w_uq[..., MLA_NOPE + half:]
    zq = jnp.zeros_like(nope)
    wq = _place([(jnp.concatenate([nope, x1, x2], -1), HEAD_PAD)], HEAD_PAD)
    wqs = _place([(jnp.concatenate([zq, x2, x1], -1), HEAD_PAD)], HEAD_PAD)
    wq = wq.reshape(MLA_Q_LORA, MLA_PAD_WIDTH).astype(bf16)
    wqs = wqs.reshape(MLA_Q_LORA, MLA_PAD_WIDTH).astype(bf16)

    w_ukv = p["w_ukv"].reshape(MLA_KV_LORA, MLA_HEADS, MLA_NOPE + MLA_V)
    k_nope = _place([(w_ukv[..., :MLA_NOPE], HEAD_PAD)], HEAD_PAD).reshape(MLA_KV_LORA, MLA_PAD_WIDTH)
    eye = jnp.eye(MLA_ROPE, dtype=f32)
    put = jnp.pad(eye, ((0, 0), (MLA_NOPE, HEAD_PAD - MLA_QK_DIM)))
    put = jnp.tile(put, (1, MLA_HEADS))
    wk = jnp.concatenate([k_nope, put, put, jnp.zeros((2 * LANES - MLA_KV_LORA - 2 * MLA_ROPE,
                                                        MLA_PAD_WIDTH), f32)], axis=0).astype(bf16)
    wvt = _place([(w_ukv[..., MLA_NOPE:], MLA_VT_ROWS)], MLA_VT_ROWS)
    wvt = wvt.reshape(MLA_KV_LORA, MLA_HEADS * MLA_VT_ROWS).T.astype(bf16)

    gq = p["mla_q_norm"]
    gk = p["mla_k_norm"]
    scale = MLA_QK_DIM ** -0.5 * LOG2E
    seq_len = cos.shape[0]
    ones = jnp.ones((seq_len, 1), f32)
    tq1 = _place([(ones * gq[None, :MLA_NOPE], MLA_NOPE),
                  (cos * gq[None, MLA_NOPE:MLA_NOPE + half], half),
                  (cos * gq[None, MLA_NOPE + half:], half)], MLA_QK_DIM) * scale
    tq2 = _place([(jnp.zeros((seq_len, MLA_NOPE), f32), MLA_NOPE),
                  (-sin * gq[None, MLA_NOPE + half:], half),
                  (sin * gq[None, MLA_NOPE:MLA_NOPE + half], half)], MLA_QK_DIM) * scale
    e1 = jnp.concatenate([cos * gk[None, MLA_NOPE:MLA_NOPE + half], cos * gk[None, MLA_NOPE + half:]], -1)
    e2 = jnp.concatenate([-sin * gk[None, MLA_NOPE + half:], sin * gk[None, MLA_NOPE:MLA_NOPE + half]], -1)
    tabs = {"tq1": _place([(tq1, LANES)], LANES), "tq2": _place([(tq2, LANES)], LANES),
            "e1": _place([(e1, LANES)], LANES), "e2": _place([(e2, LANES)], LANES)}
    gkf = _place([(gk[None, :MLA_NOPE], MLA_NOPE), (jnp.ones((1, MLA_ROPE), f32), LANES - MLA_NOPE)], LANES)

    na_scale = NA_HEAD_DIM ** -0.5 * LOG2E
    lp = {
        "ln_g": p["ln_g"][None, :],
        "w_in": w_in_packed,
        "gq2": jnp.tile(p["na_q_norm"], 2)[None, :] * na_scale,
        "gk2": jnp.tile(p["na_k_norm"], 2)[None, :],
        "gcq": p["mla_cq_norm"][None, :],
        "gckv": p["mla_ckv_norm"][None, :],
        "wq": wq, "wqs": wqs, "wk": wk, "wvt": wvt, "gkf": gkf,
        "w_o_na": p["w_o_na"].astype(bf16),
        "w_o_mla": p["w_o_mla"].astype(bf16),
        "w_out": p["w_out"].astype(bf16),
    }
    return lp, tabs


@jax.jit
def _forward(x, p):
    bsz, seq_len, _ = x.shape
    depth = p["w_in"].shape[0]
    cos, sin = _rope_tables(seq_len)
    onehot, col_mask, tbl_idx = _na_table_static(seq_len)
    x2d = x.reshape(bsz * seq_len, D_MODEL)
    lp, tabs = jax.vmap(lambda pl_: _layer_params(pl_, cos, sin))(p)
    tbl = jax.vmap(lambda rb: _na_table(rb, onehot, col_mask, tbl_idx))(p["na_rel_bias"])
    for layer in range(depth):
        qa, ka, va, nag, qf, kf, mg, gna, gmla, vbt = _proj_call(x2d, lp, tabs, layer, seq_len)

        def b3(a):
            return a.reshape(bsz, seq_len, a.shape[-1])

        a_na = _natt_call(b3(qa), b3(ka), b3(va), b3(nag), tbl, layer)
        a_mla = _mla_call(b3(qf), b3(kf), vbt, b3(mg))
        x2d = _out_call(x2d, a_na.reshape(-1, NA_WIDTH), a_mla.reshape(-1, MLA_WIDTH), gna, gmla, lp, layer)
    return x2d.reshape(bsz, seq_len, D_MODEL)


def kernel(x, ln_g, w_in, na_q_norm, na_k_norm, na_rel_bias, mla_cq_norm, mla_ckv_norm, w_uq, w_ukv,
           mla_q_norm, mla_k_norm, w_o_na, w_o_mla, w_out):
    params = dict(ln_g=ln_g, w_in=w_in, na_q_norm=na_q_norm, na_k_norm=na_k_norm,
                  na_rel_bias=na_rel_bias, mla_cq_norm=mla_cq_norm, mla_ckv_norm=mla_ckv_norm,
                  w_uq=w_uq, w_ukv=w_ukv, mla_q_norm=mla_q_norm, mla_k_norm=mla_k_norm,
                  w_o_na=w_o_na, w_o_mla=w_o_mla, w_out=w_out)
    return _forward(x, params)
```

```python
import functools

import numpy as np
import jax
import jax.numpy as jnp
from jax import lax
from jax.experimental import pallas as pl
from jax.experimental.pallas import tpu as pltpu

D_MODEL = 1024
GRID_W = 64
NA_HEADS = 8
NA_HEAD_DIM = 64
NA_WIDTH = NA_HEADS * NA_HEAD_DIM
NA_WIN_ROWS = 8
NA_WIN_COLS = 16
MLA_HEADS = 8
MLA_NOPE = 64
MLA_ROPE = 32
MLA_QK_DIM = MLA_NOPE + MLA_ROPE
MLA_V = 64
MLA_Q_LORA = 256
MLA_KV_LORA = 128
MLA_WIDTH = MLA_HEADS * MLA_V
ROPE_BASE = 10000.0
EPS = 1e-6

LANES = 128
HEAD_PAD = LANES
MLA_PAD_WIDTH = MLA_HEADS * HEAD_PAD
MLA_VT_ROWS = 128
VMEM_LIMIT_BYTES = 56 * 1024 * 1024

_OFF_NAQ = 0
_OFF_NAK = _OFF_NAQ + NA_WIDTH
_OFF_NAV = _OFF_NAK + NA_WIDTH
_OFF_NAG = _OFF_NAV + NA_WIDTH
_OFF_CQ = _OFF_NAG + NA_WIDTH
_OFF_CKV = _OFF_CQ + MLA_Q_LORA
_OFF_MG = _OFF_CKV + 2 * LANES
_OFF_GNA = _OFF_MG + MLA_WIDTH
_OFF_GMLA = _OFF_GNA + D_MODEL
_W_IN_COLS = _OFF_GMLA + D_MODEL

PROJ_TILE = 512
OUT_TILE = 1024
PROJ_BLK = 256
NA_QROWS = 4
NA_BAND = 12
NA_TQ = NA_QROWS * GRID_W
NA_TK = NA_BAND * GRID_W
NA_KC = 768
MLA_TQ = 256
MLA_KC = 1024
MLA_GROUP = 8
MASK_VALUE = -1e30
LOG2E = 1.4426950408889634

_NT_DIMS = (((1,), (1,)), ((), ()))


def _dot(a, b):
    return jnp.dot(a, b, preferred_element_type=jnp.float32)


def _dot_nt(a, b):
    return lax.dot_general(a, b, _NT_DIMS, preferred_element_type=jnp.float32)


def _lane_iota():
    return lax.broadcasted_iota(jnp.int32, (1, LANES), 1)


def _proj_kernel(x_ref, lng_ref, w_ref, gq_ref, gk_ref, gcq_ref, gckv_ref, wq_ref, wqs_ref,
                 wk_ref, wvt_ref, gkf_ref, tq1_ref, tq2_ref, e1_ref, e2_ref,
                 qa_ref, ka_ref, va_ref, nag_ref, qf_ref, kf_ref, mg_ref,
                 gna_ref, gmla_ref, vbt_ref, hb_scr, cqn_scr, lhsk_scr, sspe_scr):
    bf16 = jnp.bfloat16
    f32 = jnp.float32
    x = x_ref[...]
    hb_scr[...] = (x * lax.rsqrt(jnp.mean(x * x, axis=-1, keepdims=True) + EPS) * lng_ref[...]).astype(bf16)
    lo_half = _lane_iota() < NA_HEAD_DIM
    rope_lanes = _lane_iota() < MLA_ROPE
    ones_row = lax.broadcasted_iota(jnp.int32, (MLA_VT_ROWS, 1), 0) == MLA_V

    def tiles(y):
        return [y[:, i * LANES:(i + 1) * LANES] for i in range(y.shape[1] // LANES)]

    def in_dot(off):
        return lambda: _dot(hb_scr[...], w_ref[:, off:off + PROJ_BLK])

    def headnorm64(out_ref, g_ref, col0):
        def post(y):
            for i, pc in enumerate(tiles(y)):
                sq = pc * pc
                s_lo = jnp.sum(jnp.where(lo_half, sq, 0.0), axis=-1, keepdims=True)
                s_hi = jnp.sum(jnp.where(lo_half, 0.0, sq), axis=-1, keepdims=True)
                r = jnp.where(lo_half,
                              lax.rsqrt(s_lo * (1.0 / NA_HEAD_DIM) + EPS),
                              lax.rsqrt(s_hi * (1.0 / NA_HEAD_DIM) + EPS))
                out_ref[:, col0 + i * LANES:col0 + (i + 1) * LANES] = (pc * r * g_ref[...]).astype(bf16)
        return post

    def elementwise(out_ref, col0, fn):
        def post(y):
            out_ref[:, col0:col0 + PROJ_BLK] = fn(y).astype(bf16)
        return post

    def silu(g):
        return g * jax.nn.sigmoid(g)

    def post_cq(y):
        cqn_scr[...] = (y * lax.rsqrt(jnp.mean(y * y, axis=-1, keepdims=True) + EPS) * gcq_ref[...]).astype(bf16)

    def post_ckv(y):
        ckv, e = tiles(y)
        lhsk_scr[:, :LANES] = (ckv * lax.rsqrt(jnp.mean(ckv * ckv, axis=-1, keepdims=True) + EPS)
                               * gckv_ref[...]).astype(bf16)
        ss_pe = jnp.sum(jnp.where(rope_lanes, e * e, 0.0), axis=-1, keepdims=True)
        sspe_scr[...] = jnp.broadcast_to(ss_pe, sspe_scr.shape)
        rot = e * e1_ref[...] + pltpu.roll(e, LANES - MLA_ROPE, axis=1) * e2_ref[...]
        rot_hi = rot.astype(bf16).astype(f32)
        rot_lo = (rot - rot_hi).astype(bf16).astype(f32)
        lhsk_scr[:, LANES:] = (rot_hi + pltpu.roll(rot_lo, MLA_ROPE, axis=1)).astype(bf16)

    def q_dot(pair):
        cols = slice(pair * PROJ_BLK, (pair + 1) * PROJ_BLK)
        return lambda: (_dot(cqn_scr[...], wq_ref[:, cols]), _dot(cqn_scr[...], wqs_ref[:, cols]))

    def q_post(pair):
        def post(ys):
            for i, (a, b) in enumerate(zip(tiles(ys[0]), tiles(ys[1]))):
                r = lax.rsqrt(jnp.sum(a * a, axis=-1, keepdims=True) / MLA_QK_DIM + EPS)
                col0 = pair * PROJ_BLK + i * HEAD_PAD
                qf_ref[:, col0:col0 + HEAD_PAD] = ((a * tq1_ref[...] + b * tq2_ref[...]) * r).astype(bf16)
        return post

    def k_dot(pair):
        cols = slice(pair * PROJ_BLK, (pair + 1) * PROJ_BLK)
        return lambda: _dot(lhsk_scr[...], wk_ref[:, cols])

    def k_post(pair):
        def post(y):
            for i, a in enumerate(tiles(y)):
                ss = jnp.sum(jnp.where(lo_half, a * a, 0.0), axis=-1, keepdims=True) + sspe_scr[...]
                r = lax.rsqrt(ss / MLA_QK_DIM + EPS)
                col0 = pair * PROJ_BLK + i * HEAD_PAD
                kf_ref[:, col0:col0 + HEAD_PAD] = (a * gkf_ref[...] * r).astype(bf16)
        return post

    def v_dot(pair):
        rows = slice(pair * 2 * MLA_VT_ROWS, (pair + 1) * 2 * MLA_VT_ROWS)
        return lambda: _dot_nt(wvt_ref[rows, :], lhsk_scr[:, :LANES])

    def v_post(pair):
        def post(y):
            for i in range(2):
                row0 = (2 * pair + i) * MLA_VT_ROWS
                a = y[i * MLA_VT_ROWS:(i + 1) * MLA_VT_ROWS, :]
                vbt_ref[0, row0:row0 + MLA_VT_ROWS, :] = jnp.where(ones_row, 1.0, a).astype(bf16)
        return post

    def in_blocks(off, width, make_post):
        return [(in_dot(off + c), make_post(c)) for c in range(0, width, PROJ_BLK)]

    def wide_blocks(off, width, out_ref, fn):
        tasks = []
        half = hb_scr.shape[0] // 2
        for r0 in range(0, hb_scr.shape[0], half):
            for c in range(0, width, 2 * PROJ_BLK):
                def dot_fn(r0=r0, c=c):
                    return _dot(hb_scr[r0:r0 + half, :], w_ref[:, off + c:off + c + 2 * PROJ_BLK])

                def post(y, r0=r0, c=c):
                    out_ref[r0:r0 + half, c:c + 2 * PROJ_BLK] = fn(y).astype(bf16)
                tasks.append((dot_fn, post))
        return tasks

    tasks = [(in_dot(_OFF_CQ), post_cq), (in_dot(_OFF_CKV), post_ckv)]
    tasks += in_blocks(_OFF_NAQ, NA_WIDTH, lambda c: headnorm64(qa_ref, gq_ref, c))
    tasks += in_blocks(_OFF_NAK, NA_WIDTH, lambda c: headnorm64(ka_ref, gk_ref, c))
    tasks += [(q_dot(p), q_post(p)) for p in range(MLA_PAD_WIDTH // PROJ_BLK)]
    tasks += wide_blocks(_OFF_GNA, D_MODEL, gna_ref, jax.nn.sigmoid)
    tasks += [(k_dot(p), k_post(p)) for p in range(MLA_PAD_WIDTH // PROJ_BLK)]
    tasks += wide_blocks(_OFF_NAG, NA_WIDTH, nag_ref, silu)
    tasks += [(v_dot(p), v_post(p)) for p in range(MLA_PAD_WIDTH // PROJ_BLK)]
    tasks += wide_blocks(_OFF_MG, MLA_WIDTH, mg_ref, silu)
    tasks += wide_blocks(_OFF_GMLA, D_MODEL, gmla_ref, jax.nn.sigmoid)
    tasks += wide_blocks(_OFF_NAV, NA_WIDTH, va_ref, lambda y: y)

    _run_tasks([(dot_fn, post_fn, False) for dot_fn, post_fn in tasks])


def _const_spec(stacked, layer):
    rest = stacked.shape[1:]
    return pl.BlockSpec((None,) + rest, lambda i: (layer,) + (0,) * len(rest), pipeline_mode=pl.Buffered(1))


def _proj_call(x2d, lp, tabs, layer, seq_len):
    n_tok = x2d.shape[0]
    tm = PROJ_TILE
    assert n_tok % tm == 0 and seq_len % tm == 0
    tiles_per_seq = seq_len // tm
    bf16 = jnp.bfloat16

    def row_spec(width):
        return pl.BlockSpec((tm, width), lambda i: (i, 0))

    def tab_spec():
        return pl.BlockSpec((None, tm, LANES), lambda i: (layer, i % tiles_per_seq, 0))

    consts = [lp["ln_g"], lp["w_in"], lp["gq2"], lp["gk2"], lp["gcq"], lp["gckv"],
              lp["wq"], lp["wqs"], lp["wk"], lp["wvt"], lp["gkf"]]
    out_widths = [NA_WIDTH, NA_WIDTH, NA_WIDTH, NA_WIDTH, MLA_PAD_WIDTH, MLA_PAD_WIDTH,
                  MLA_WIDTH, D_MODEL, D_MODEL]
    t_widths = [MLA_HEADS * MLA_VT_ROWS]
    t_specs = [pl.BlockSpec((1, w, tm), lambda i: (i // tiles_per_seq, 0, i % tiles_per_seq)) for w in t_widths]
    t_shapes = [jax.ShapeDtypeStruct((n_tok // seq_len, w, seq_len), bf16) for w in t_widths]
    return pl.pallas_call(
        _proj_kernel,
        grid=(n_tok // tm,),
        in_specs=[row_spec(D_MODEL)] + [_const_spec(c, layer) for c in consts] + [tab_spec()] * 4,
        out_specs=[row_spec(w) for w in out_widths] + t_specs,
        out_shape=[jax.ShapeDtypeStruct((n_tok, w), bf16) for w in out_widths] + t_shapes,
        scratch_shapes=[pltpu.VMEM((tm, D_MODEL), bf16),
                        pltpu.VMEM((tm, MLA_Q_LORA), bf16),
                        pltpu.VMEM((tm, 2 * LANES), bf16),
                        pltpu.VMEM((tm, LANES), jnp.float32)],
        compiler_params=pltpu.CompilerParams(
            dimension_semantics=("arbitrary",), vmem_limit_bytes=VMEM_LIMIT_BYTES),
        name="proj",
    )(x2d, *consts, tabs["tq1"], tabs["tq2"], tabs["e1"], tabs["e2"])


def _na_band_start(i, rows):
    return jnp.clip(NA_QROWS * i - NA_WIN_ROWS // 2, 0, rows - NA_BAND)


def _natt_kernel(q_ref, k_ref, v_ref, gate_ref, tbl_ref, o_ref, s_scr, p_scr):
    bf16 = jnp.bfloat16
    i = pl.program_id(1)
    start = pl.multiple_of(_na_band_start(i, k_ref.shape[1] // GRID_W) * GRID_W, GRID_W)
    pat = jnp.minimum(i, 1) + (i == pl.num_programs(1) - 1).astype(jnp.int32)
    lo_half = _lane_iota() < NA_HEAD_DIM
    n_chunks = NA_TK // NA_KC

    def pcols(head):
        return slice((head // 2) * LANES, (head // 2 + 1) * LANES)

    def kslice(c):
        return slice(c * NA_KC, (c + 1) * NA_KC)

    def fold(x, run, op):
        for t in range(x.shape[1] // LANES):
            tile = x[:, t * LANES:(t + 1) * LANES]
            run = tile if run is None else op(run, tile)
        return run

    def masked_q(head):
        qp = q_ref[0, :, pcols(head)]
        sel = lo_half if head % 2 == 0 else jnp.logical_not(lo_half)
        return jnp.where(sel, qp, jnp.zeros_like(qp))

    def qk_chunk(head, qm, c, run_max):
        keys = k_ref[0, pl.ds(start + c * NA_KC, NA_KC), pcols(head)]
        s = _dot_nt(qm, keys) + tbl_ref[pat, head, :, kslice(c)]
        s_scr[head % 2, :, kslice(c)] = s
        return fold(s, run_max, jnp.maximum)

    def exp_chunk(head, c, m, run_sum):
        p = jnp.exp2(s_scr[head % 2, :, kslice(c)] - m)
        p_scr[head % 2, :, kslice(c)] = p.astype(bf16)
        return fold(p, run_sum, jnp.add)

    def pv_chunk(head, c, acc):
        vals = v_ref[0, pl.ds(start + c * NA_KC, NA_KC), pcols(head)]
        o = _dot(p_scr[head % 2, :, kslice(c)], vals)
        return o if acc is None else acc + o

    def finish(head, acc, run_sum, prev):
        o = acc / jnp.sum(run_sum, axis=-1, keepdims=True)
        if head % 2 == 0:
            return o
        pair = jnp.where(lo_half, prev, o)
        o_ref[0, :, pcols(head)] = (pair * gate_ref[0, :, pcols(head)].astype(jnp.float32)).astype(bf16)
        return None

    qm = masked_q(0)
    run = None
    for c in range(n_chunks):
        run = qk_chunk(0, qm, c, run)
    m = jnp.max(run, axis=-1, keepdims=True)
    prev_sum, even_out = None, None
    for head in range(NA_HEADS):
        run, acc, cur_sum = None, None, None
        if head + 1 < NA_HEADS:
            qm = masked_q(head + 1)
        for c in range(n_chunks):
            if head + 1 < NA_HEADS:
                run = qk_chunk(head + 1, qm, c, run)
            if head >= 1:
                acc = pv_chunk(head - 1, c, acc)
            cur_sum = exp_chunk(head, c, m, cur_sum)
        if head >= 1:
            even_out = finish(head - 1, acc, prev_sum, even_out)
        prev_sum = cur_sum
        if head + 1 < NA_HEADS:
            m = jnp.max(run, axis=-1, keepdims=True)
    acc = None
    for c in range(n_chunks):
        acc = pv_chunk(NA_HEADS - 1, c, acc)
    finish(NA_HEADS - 1, acc, prev_sum, even_out)


def _natt_call(qa, ka, va, nag, tbl, layer):
    bsz, seq_len, _ = qa.shape
    n_blk = seq_len // NA_TQ
    blk = pl.BlockSpec((1, NA_TQ, NA_WIDTH), lambda b, i: (b, i, 0))
    full = pl.BlockSpec((1, seq_len, NA_WIDTH), lambda b, i: (b, 0, 0))
    return pl.pallas_call(
        _natt_kernel,
        grid=(bsz, n_blk),
        in_specs=[blk, full, full, blk,
                  pl.BlockSpec((None,) + tbl.shape[1:], lambda b, i: (layer, 0, 0, 0, 0),
                               pipeline_mode=pl.Buffered(1))],
        out_specs=blk,
        out_shape=jax.ShapeDtypeStruct((bsz, seq_len, NA_WIDTH), jnp.bfloat16),
        scratch_shapes=[pltpu.VMEM((2, NA_TQ, NA_TK), jnp.float32),
                        pltpu.VMEM((2, NA_TQ, NA_TK), jnp.bfloat16)],
        compiler_params=pltpu.CompilerParams(
            dimension_semantics=("arbitrary", "arbitrary"), vmem_limit_bytes=VMEM_LIMIT_BYTES),
        name="natt",
    )(qa, ka, va, nag, tbl)


def _mla_kernel(q_ref, k_ref, vt_ref, gate_ref, o_ref, s_scr, p_scr):
    bf16 = jnp.bfloat16
    seq_len = k_ref.shape[1]
    n_chunks = seq_len // MLA_KC
    n_heads = q_ref.shape[2] // HEAD_PAD
    lo_half = _lane_iota() < MLA_V
    sublanes = 8

    def hcols(head):
        return slice(head * HEAD_PAD, (head + 1) * HEAD_PAD)

    def kslice(c):
        return slice(c * MLA_KC, (c + 1) * MLA_KC)

    def qk_chunk(head, c, run_max):
        s = _dot_nt(k_ref[0, kslice(c), hcols(head)], q_ref[0, :, hcols(head)])
        s_scr[head % 2, kslice(c), :] = s
        part = jnp.max(s.reshape(MLA_KC // sublanes, sublanes, MLA_TQ), axis=0)
        return part if run_max is None else jnp.maximum(run_max, part)

    def exp_chunk(head, c, m):
        p_scr[head % 2, kslice(c), :] = jnp.exp2(s_scr[head % 2, kslice(c), :] - m).astype(bf16)

    def pv_chunk(head, c, acc):
        vrows = slice(head * MLA_VT_ROWS, (head + 1) * MLA_VT_ROWS)
        o = _dot(vt_ref[0, vrows, kslice(c)], p_scr[head % 2, kslice(c), :])
        return o if acc is None else acc + o

    def finish(head, acc, prev):
        if MLA_VT_ROWS < HEAD_PAD:
            pad = jnp.zeros((HEAD_PAD - MLA_VT_ROWS, acc.shape[1]), acc.dtype)
            acc = jnp.concatenate([acc, pad], axis=0)
        acc = acc.T
        o = acc / acc[:, MLA_V:MLA_V + 1]
        if head % 2 == 0:
            return o
        cols = slice((head // 2) * LANES, (head // 2 + 1) * LANES)
        pair = jnp.where(lo_half, prev, pltpu.roll(o, MLA_V, axis=1))
        o_ref[0, :, cols] = (pair * gate_ref[0, :, cols].astype(jnp.float32)).astype(bf16)
        return None

    run = None
    for c in range(n_chunks):
        run = qk_chunk(0, c, run)
    m = jnp.max(run, axis=0, keepdims=True)
    acc, even_out = None, None
    for head in range(n_heads):
        run, next_acc = None, None
        for c in range(n_chunks):
            if head + 1 < n_heads:
                run = qk_chunk(head + 1, c, run)
            if head >= 1:
                next_acc = pv_chunk(head - 1, c, next_acc)
            exp_chunk(head, c, m)
        if head >= 1:
            even_out = finish(head - 1, next_acc, even_out)
        if head + 1 < n_heads:
            m = jnp.max(run, axis=0, keepdims=True)
    for c in range(n_chunks):
        acc = pv_chunk(n_heads - 1, c, acc)
    finish(n_heads - 1, acc, even_out)


def _mla_call(qf, kf, vbt, mg):
    bsz, seq_len, _ = qf.shape
    n_groups = MLA_HEADS // MLA_GROUP
    assert seq_len % MLA_KC == 0 and seq_len % MLA_TQ == 0 and MLA_GROUP % 2 == 0
    return pl.pallas_call(
        _mla_kernel,
        grid=(bsz, n_groups, seq_len // MLA_TQ),
        in_specs=[pl.BlockSpec((1, MLA_TQ, MLA_GROUP * HEAD_PAD), lambda b, p, j: (b, j, p)),
                  pl.BlockSpec((1, seq_len, MLA_GROUP * HEAD_PAD), lambda b, p, j: (b, 0, p)),
                  pl.BlockSpec((1, MLA_GROUP * MLA_VT_ROWS, seq_len), lambda b, p, j: (b, p, 0)),
                  pl.BlockSpec((1, MLA_TQ, MLA_GROUP * MLA_V), lambda b, p, j: (b, j, p))],
        out_specs=pl.BlockSpec((1, MLA_TQ, MLA_GROUP * MLA_V), lambda b, p, j: (b, j, p)),
        out_shape=jax.ShapeDtypeStruct((bsz, seq_len, MLA_WIDTH), jnp.bfloat16),
        scratch_shapes=[pltpu.VMEM((2, seq_len, MLA_TQ), jnp.float32),
                        pltpu.VMEM((2, seq_len, MLA_TQ), jnp.bfloat16)],
        compiler_params=pltpu.CompilerParams(
            dimension_semantics=("arbitrary", "arbitrary", "arbitrary"),
            vmem_limit_bytes=VMEM_LIMIT_BYTES),
        name="mla",
    )(qf, kf, vbt, mg)


def _run_tasks(tasks):
    pending = None
    for dot_fn, post_fn, barrier in tasks:
        if barrier and pending is not None:
            pending[0](pending[1])
            pending = None
        y = dot_fn()
        if pending is not None:
            pending[0](pending[1])
        pending = (post_fn, y)
    pending[0](pending[1])


def _out_kernel(x_ref, ana_ref, amla_ref, gna_ref, gmla_ref, wna_ref, wmla_ref, wout_ref, o_ref, y_scr):
    f32 = jnp.float32
    row_tiles = [slice(r, r + PROJ_TILE) for r in range(0, o_ref.shape[0], PROJ_TILE)]
    col_blocks = [slice(c, c + PROJ_BLK) for c in range(0, o_ref.shape[1], PROJ_BLK)]

    def branch_dots(rows, cols):
        return lambda: (_dot(ana_ref[rows, :], wna_ref[:, cols]), _dot(amla_ref[rows, :], wmla_ref[:, cols]))

    def merge(rows, cols):
        def post(us):
            y = gna_ref[rows, cols].astype(f32) * us[0] + gmla_ref[rows, cols].astype(f32) * us[1]
            y_scr[rows, cols] = y.astype(jnp.bfloat16)
        return post

    def out_dot(rows, cols):
        return lambda: _dot(y_scr[rows, :], wout_ref[:, cols])

    def residual(rows, cols):
        def post(y):
            o_ref[rows, cols] = x_ref[rows, cols] + y
        return post

    tasks = [(branch_dots(r, c), merge(r, c), False) for r in row_tiles for c in col_blocks]
    tasks += [(out_dot(r, c), residual(r, c), i == 0)
              for i, (r, c) in enumerate((r, c) for r in row_tiles for c in col_blocks)]
    _run_tasks(tasks)


def _out_call(x2d, a_na, a_mla, gna, gmla, lp, layer):
    n_tok = x2d.shape[0]
    tm = OUT_TILE
    assert n_tok % tm == 0 and tm % PROJ_TILE == 0

    def row_spec(width):
        return pl.BlockSpec((tm, width), lambda i: (i, 0))

    consts = [lp["w_o_na"], lp["w_o_mla"], lp["w_out"]]
    return pl.pallas_call(
        _out_kernel,
        grid=(n_tok // tm,),
        in_specs=[row_spec(D_MODEL), row_spec(NA_WIDTH), row_spec(MLA_WIDTH), row_spec(D_MODEL),
                  row_spec(D_MODEL)] + [_const_spec(c, layer) for c in consts],
        out_specs=row_spec(D_MODEL),
        out_shape=jax.ShapeDtypeStruct((n_tok, D_MODEL), jnp.float32),
        scratch_shapes=[pltpu.VMEM((tm, D_MODEL), jnp.bfloat16)],
        compiler_params=pltpu.CompilerParams(
            dimension_semantics=("arbitrary",), vmem_limit_bytes=VMEM_LIMIT_BYTES),
        name="out",
    )(x2d, a_na, a_mla, gna, gmla, *consts)


_N_DROW = 2 * NA_WIN_ROWS - 1
_N_DCOL = 2 * NA_WIN_COLS - 1


def _na_table_static(seq_len):
    rows = seq_len // GRID_W
    n_blk = rows // NA_QROWS
    qc = np.arange(GRID_W)
    kj = np.arange(GRID_W)
    cs = np.clip(qc - NA_WIN_COLS // 2, 0, GRID_W - NA_WIN_COLS)
    col_valid = (kj[None, :] >= cs[:, None]) & (kj[None, :] < cs[:, None] + NA_WIN_COLS)
    dcol = kj[None, :] - qc[:, None] + (NA_WIN_COLS - 1)
    onehot = (dcol[None] == np.arange(_N_DCOL)[:, None, None]) & col_valid[None]
    col_mask = np.where(col_valid, 0.0, MASK_VALUE).astype(np.float32)
    qr = np.arange(NA_QROWS)
    kn = np.arange(NA_BAND)
    idx = []
    for i in (0, 1, n_blk - 1):
        band = int(np.clip(NA_QROWS * i - NA_WIN_ROWS // 2, 0, rows - NA_BAND))
        r = NA_QROWS * i + qr
        rs = np.clip(r - NA_WIN_ROWS // 2, 0, rows - NA_WIN_ROWS)
        krow = band + kn
        valid = (krow[None, :] >= rs[:, None]) & (krow[None, :] < rs[:, None] + NA_WIN_ROWS)
        drow = krow[None, :] - r[:, None] + (NA_WIN_ROWS - 1)
        idx.append(np.where(valid, drow, _N_DROW))
    return onehot.astype(np.float32), col_mask, np.stack(idx).astype(np.int32)


def _na_table(rel_bias, onehot, col_mask, idx):
    blocks = jnp.einsum("hdc,cqj->hdqj", rel_bias, onehot, precision=lax.Precision.HIGHEST) + col_mask
    blocks = jnp.concatenate(
        [blocks, jnp.full((NA_HEADS, 1, GRID_W, GRID_W), MASK_VALUE, jnp.float32)], axis=1) * LOG2E
    tbl = jnp.take(blocks, idx.reshape(-1), axis=1)
    tbl = tbl.reshape(NA_HEADS, 3, NA_QROWS, NA_BAND, GRID_W, GRID_W)
    return tbl.transpose(1, 0, 2, 4, 3, 5).reshape(3, NA_HEADS, NA_TQ, NA_TK)


def _rope_tables(seq_len):
    t = jnp.arange(seq_len)
    row = (t // GRID_W).astype(jnp.float32)
    col = (t % GRID_W).astype(jnp.float32)
    half = MLA_ROPE // 2
    n_freq = half // 2
    inv = jnp.power(jnp.float32(ROPE_BASE), -jnp.arange(n_freq, dtype=jnp.float32) / n_freq)
    ang = jnp.concatenate([row[:, None] * inv, col[:, None] * inv], axis=-1)
    return jnp.cos(ang), jnp.sin(ang)


def _place(cols_to_blocks, total):
    parts = []
    for arr, width in cols_to_blocks:
        pad = width - arr.shape[-1]
        parts.append(jnp.pad(arr, [(0, 0)] * (arr.ndim - 1) + [(0, pad)]) if pad else arr)
    out = jnp.concatenate(parts, axis=-1)
    assert out.shape[-1] == total
    return out


def _layer_params(p, cos, sin):
    f32, bf16 = jnp.float32, jnp.bfloat16
    half = MLA_ROPE // 2
    w_in = p["w_in"]
    o = 0
    pieces = {}
    for name, size in (("naq", NA_WIDTH), ("nak", NA_WIDTH), ("nav", NA_WIDTH), ("nag", NA_WIDTH),
                       ("cq", MLA_Q_LORA), ("ckv", MLA_KV_LORA), ("kpe", MLA_ROPE),
                       ("mg", MLA_WIDTH), ("gna", D_MODEL), ("gmla", D_MODEL)):
        pieces[name] = w_in[:, o:o + size]
        o += size
    kpe = pieces["kpe"]
    kpe_sw = jnp.concatenate([kpe[:, half:], kpe[:, :half]], axis=-1)
    ckvx = _place([(pieces["ckv"], LANES), (jnp.concatenate([kpe, kpe_sw], axis=-1), LANES)], 2 * LANES)
    w_in_packed = jnp.concatenate(
        [pieces["naq"], pieces["nak"], pieces["nav"], pieces["nag"], pieces["cq"], ckvx,
         pieces["mg"], pieces["gna"], pieces["gmla"]], axis=-1).astype(bf16)
    assert w_in_packed.shape[-1] == _W_IN_COLS

    w_uq = p["w_uq"].reshape(MLA_Q_LORA, MLA_HEADS, MLA_QK_DIM)
    nope, x1, x2 = w_uq[..., :MLA_NOPE], w_uq[..., MLA_NOPE:MLA_NOPE + half], w_uq[..., MLA_NOPE + half:]
    zq = jnp.zeros_like(nope)
    wq = _place([(jnp.concatenate([nope, x1, x2], -1), HEAD_PAD)], HEAD_PAD)
    wqs = _place([(jnp.concatenate([zq, x2, x1], -1), HEAD_PAD)], HEAD_PAD)
    wq = wq.reshape(MLA_Q_LORA, MLA_PAD_WIDTH).astype(bf16)
    wqs = wqs.reshape(MLA_Q_LORA, MLA_PAD_WIDTH).astype(bf16)

    w_ukv = p["w_ukv"].reshape(MLA_KV_LORA, MLA_HEADS, MLA_NOPE + MLA_V)
    k_nope = _place([(w_ukv[..., :MLA_NOPE], HEAD_PAD)], HEAD_PAD).reshape(MLA_KV_LORA, MLA_PAD_WIDTH)
    eye = jnp.eye(MLA_ROPE, dtype=f32)
    put = jnp.pad(eye, ((0, 0), (MLA_NOPE, HEAD_PAD - MLA_QK_DIM)))
    put = jnp.tile(put, (1, MLA_HEADS))
    wk = jnp.concatenate([k_nope, put, put, jnp.zeros((2 * LANES - MLA_KV_LORA - 2 * MLA_ROPE,
                                                        MLA_PAD_WIDTH), f32)], axis=0).astype(bf16)
    wvt = _place([(w_ukv[..., MLA_NOPE:], MLA_VT_ROWS)], MLA_VT_ROWS)
    wvt = wvt.reshape(MLA_KV_LORA, MLA_HEADS * MLA_VT_ROWS).T.astype(bf16)

    gq = p["mla_q_norm"]
    gk = p["mla_k_norm"]
    scale = MLA_QK_DIM ** -0.5 * LOG2E
    seq_len = cos.shape[0]
    ones = jnp.ones((seq_len, 1), f32)
    tq1 = _place([(ones * gq[None, :MLA_NOPE], MLA_NOPE),
                  (cos * gq[None, MLA_NOPE:MLA_NOPE + half], half),
                  (cos * gq[None, MLA_NOPE + half:], half)], MLA_QK_DIM) * scale
    tq2 = _place([(jnp.zeros((seq_len, MLA_NOPE), f32), MLA_NOPE),
                  (-sin * gq[None, MLA_NOPE + half:], half),
                  (sin * gq[None, MLA_NOPE:MLA_NOPE + half], half)], MLA_QK_DIM) * scale
    e1 = jnp.concatenate([cos * gk[None, MLA_NOPE:MLA_NOPE + half], cos * gk[None, MLA_NOPE + half:]], -1)
    e2 = jnp.concatenate([-sin * gk[None, MLA_NOPE + half:], sin * gk[None, MLA_NOPE:MLA_NOPE + half]], -1)
    tabs = {"tq1": _place([(tq1, LANES)], LANES), "tq2": _place([(tq2, LANES)], LANES),
            "e1": _place([(e1, LANES)], LANES), "e2": _place([(e2, LANES)], LANES)}
    gkf = _place([(gk[None, :MLA_NOPE], MLA_NOPE), (jnp.ones((1, MLA_ROPE), f32), LANES - MLA_NOPE)], LANES)

    na_scale = NA_HEAD_DIM ** -0.5 * LOG2E
    lp = {
        "ln_g": p["ln_g"][None, :],
        "w_in": w_in_packed,
        "gq2": jnp.tile(p["na_q_norm"], 2)[None, :] * na_scale,
        "gk2": jnp.tile(p["na_k_norm"], 2)[None, :],
        "gcq": p["mla_cq_norm"][None, :],
        "gckv": p["mla_ckv_norm"][None, :],
        "wq": wq, "wqs": wqs, "wk": wk, "wvt": wvt, "gkf": gkf,
        "w_o_na": p["w_o_na"].astype(bf16),
        "w_o_mla": p["w_o_mla"].astype(bf16),
        "w_out": p["w_out"].astype(bf16),
    }
    return lp, tabs


@jax.jit
def _forward(x, p):
    bsz, seq_len, _ = x.shape
    depth = p["w_in"].shape[0]
    cos, sin = _rope_tables(seq_len)
    onehot, col_mask, tbl_idx = _na_table_static(seq_len)
    x2d = x.reshape(bsz * seq_len, D_MODEL)
    lp, tabs = jax.vmap(lambda pl_: _layer_params(pl_, cos, sin))(p)
    tbl = jax.vmap(lambda rb: _na_table(rb, onehot, col_mask, tbl_idx))(p["na_rel_bias"])
    for layer in range(depth):
        qa, ka, va, nag, qf, kf, mg, gna, gmla, vbt = _proj_call(x2d, lp, tabs, layer, seq_len)

        def b3(a):
            return a.reshape(bsz, seq_len, a.shape[-1])

        a_na = _natt_call(b3(qa), b3(ka), b3(va), b3(nag), tbl, layer)
        a_mla = _mla_call(b3(qf), b3(kf), vbt, b3(mg))
        x2d = _out_call(x2d, a_na.reshape(-1, NA_WIDTH), a_mla.reshape(-1, MLA_WIDTH), gna, gmla, lp, layer)
    return x2d.reshape(bsz, seq_len, D_MODEL)


def kernel(x, ln_g, w_in, na_q_norm, na_k_norm, na_rel_bias, mla_cq_norm, mla_ckv_norm, w_uq, w_ukv,
           mla_q_norm, mla_k_norm, w_o_na, w_o_mla, w_out):
    params = dict(ln_g=ln_g, w_in=w_in, na_q_norm=na_q_norm, na_k_norm=na_k_norm,
                  na_rel_bias=na_rel_bias, mla_cq_norm=mla_cq_norm, mla_ckv_norm=mla_ckv_norm,
                  w_uq=w_uq, w_ukv=w_ukv, mla_q_norm=mla_q_norm, mla_k_norm=mla_k_norm,
                  w_o_na=w_o_na, w_o_mla=w_o_mla, w_out=w_out)
    return _forward(x, params)
```

```python
import numpy as np
import jax
import jax.numpy as jnp
from jax import lax
from jax.experimental import pallas as pl
from jax.experimental.pallas import tpu as pltpu

D_MODEL = 1024
GRID_W = 64
NA_HEADS = 8
NA_HEAD_DIM = 64
NA_WIDTH = NA_HEADS * NA_HEAD_DIM
NA_WIN_ROWS = 8
NA_WIN_COLS = 16
MLA_HEADS = 8
MLA_NOPE = 64
MLA_ROPE = 32
MLA_QK_DIM = MLA_NOPE + MLA_ROPE
MLA_V = 64
MLA_Q_LORA = 256
MLA_KV_LORA = 128
MLA_WIDTH = MLA_HEADS * MLA_V
ROPE_BASE = 10000.0
EPS = 1e-6

LANES = 128
HEAD_PAD = LANES
MLA_PAD_WIDTH = MLA_HEADS * HEAD_PAD
MLA_VT_ROWS = 128
VMEM_LIMIT_BYTES = 56 * 1024 * 1024

_OFF_NAQ = 0
_OFF_NAK = _OFF_NAQ + NA_WIDTH
_OFF_NAV = _OFF_NAK + NA_WIDTH
_OFF_NAG = _OFF_NAV + NA_WIDTH
_OFF_CQ = _OFF_NAG + NA_WIDTH
_OFF_CKV = _OFF_CQ + MLA_Q_LORA
_OFF_MG = _OFF_CKV + 2 * LANES
_OFF_GNA = _OFF_MG + MLA_WIDTH
_OFF_GMLA = _OFF_GNA + D_MODEL
_W_IN_COLS = _OFF_GMLA + D_MODEL

PROJ_TILE = 512
OUT_TILE = 1024
PROJ_BLK = 256
NA_QROWS = 4
NA_BAND = 12
NA_TQ = NA_QROWS * GRID_W
NA_TK = NA_BAND * GRID_W
NA_SUB = 2
MLA_TQ = 256
MLA_KC = 1024
MLA_GROUP = 8
MASK_VALUE = -1e30
LOG2E = 1.4426950408889634

_NT_DIMS = (((1,), (1,)), ((), ()))


def _dot(a, b):
    return jnp.dot(a, b, preferred_element_type=jnp.float32)


def _dot_nt(a, b):
    return lax.dot_general(a, b, _NT_DIMS, preferred_element_type=jnp.float32)


def _lane_iota():
    return lax.broadcasted_iota(jnp.int32, (1, LANES), 1)


def _mla_ones_row(parity):
    return MLA_V if parity == 0 else 0


def _proj_kernel(x_ref, lng_ref, w_ref, gq_ref, gk_ref, gcq_ref, gckv_ref, wq_ref, wqs_ref,
                 wk_ref, wvt_ref, gkf_ref, tq1_ref, tq2_ref, e1_ref, e2_ref,
                 qa_ref, ka_ref, va_ref, nag_ref, qf_ref, kf_ref, mg_ref,
                 gna_ref, gmla_ref, vbt_ref, hb_scr, cqn_scr, lhsk_scr, sspe_scr):
    bf16 = jnp.bfloat16
    f32 = jnp.float32
    half_rows = hb_scr.shape[0] // 2
    x = x_ref[...]
    hb_scr[...] = (x * lax.rsqrt(jnp.mean(x * x, axis=-1, keepdims=True) + EPS) * lng_ref[...]).astype(bf16)
    lo_half = _lane_iota() < NA_HEAD_DIM
    rope_lanes = _lane_iota() < MLA_ROPE
    row_iota = lax.broadcasted_iota(jnp.int32, (MLA_VT_ROWS, 1), 0)

    def tiles(y):
        return [y[:, i * LANES:(i + 1) * LANES] for i in range(y.shape[1] // LANES)]

    def in_dot(off):
        return lambda: _dot(hb_scr[...], w_ref[:, off:off + PROJ_BLK])

    def headnorm64(g_ref):
        def fn(y):
            out = []
            for pc in tiles(y):
                sq = pc * pc
                s_lo = jnp.sum(jnp.where(lo_half, sq, 0.0), axis=-1, keepdims=True)
                s_hi = jnp.sum(jnp.where(lo_half, 0.0, sq), axis=-1, keepdims=True)
                r = jnp.where(lo_half,
                              lax.rsqrt(s_lo * (1.0 / NA_HEAD_DIM) + EPS),
                              lax.rsqrt(s_hi * (1.0 / NA_HEAD_DIM) + EPS))
                out.append(pc * r * g_ref[...])
            return jnp.concatenate(out, axis=-1)
        return fn

    def silu(g):
        return g * jax.nn.sigmoid(g)

    def post_cq(y):
        cqn_scr[...] = (y * lax.rsqrt(jnp.mean(y * y, axis=-1, keepdims=True) + EPS) * gcq_ref[...]).astype(bf16)

    def post_ckv(y):
        ckv, e = tiles(y)
        lhsk_scr[:, :LANES] = (ckv * lax.rsqrt(jnp.mean(ckv * ckv, axis=-1, keepdims=True) + EPS)
                               * gckv_ref[...]).astype(bf16)
        ss_pe = jnp.sum(jnp.where(rope_lanes, e * e, 0.0), axis=-1, keepdims=True)
        sspe_scr[...] = jnp.broadcast_to(ss_pe, sspe_scr.shape)
        rot = e * e1_ref[...] + pltpu.roll(e, LANES - MLA_ROPE, axis=1) * e2_ref[...]
        rot_hi = rot.astype(bf16).astype(f32)
        rot_lo = (rot - rot_hi).astype(bf16).astype(f32)
        lhsk_scr[:, LANES:] = (rot_hi + pltpu.roll(rot_lo, MLA_ROPE, axis=1)).astype(bf16)

    def q_dot(pair):
        cols = slice(pair * PROJ_BLK, (pair + 1) * PROJ_BLK)
        return lambda: (_dot(cqn_scr[...], wq_ref[:, cols]), _dot(cqn_scr[...], wqs_ref[:, cols]))

    def q_post(pair):
        def post(ys):
            for i, (a, b) in enumerate(zip(tiles(ys[0]), tiles(ys[1]))):
                r = lax.rsqrt(jnp.sum(a * a, axis=-1, keepdims=True) / MLA_QK_DIM + EPS)
                col0 = pair * PROJ_BLK + i * HEAD_PAD
                qf_ref[:, col0:col0 + HEAD_PAD] = ((a * tq1_ref[...] + b * tq2_ref[...]) * r).astype(bf16)
        return post

    def k_dot(pair):
        cols = slice(pair * PROJ_BLK, (pair + 1) * PROJ_BLK)
        return lambda: _dot(lhsk_scr[...], wk_ref[:, cols])

    def k_post(pair):
        def post(y):
            for i, a in enumerate(tiles(y)):
                ss = jnp.sum(jnp.where(lo_half, a * a, 0.0), axis=-1, keepdims=True) + sspe_scr[...]
                r = lax.rsqrt(ss / MLA_QK_DIM + EPS)
                col0 = pair * PROJ_BLK + i * HEAD_PAD
                kf_ref[:, col0:col0 + HEAD_PAD] = (a * gkf_ref[...] * r).astype(bf16)
        return post

    def v_dot(pair):
        rows = slice(pair * 2 * MLA_VT_ROWS, (pair + 1) * 2 * MLA_VT_ROWS)
        return lambda: _dot_nt(wvt_ref[rows, :], lhsk_scr[:, :LANES])

    def v_post(pair):
        def post(y):
            for i in range(2):
                row0 = (2 * pair + i) * MLA_VT_ROWS
                a = y[i * MLA_VT_ROWS:(i + 1) * MLA_VT_ROWS, :]
                vbt_ref[0, row0:row0 + MLA_VT_ROWS, :] = jnp.where(row_iota == _mla_ones_row(i), 1.0, a).astype(bf16)
        return post

    def wide_blocks(off, width, out_ref, fn):
        tasks = []
        for r0 in (0, half_rows):
            for c in range(0, width, 2 * PROJ_BLK):
                def dot_fn(r0=r0, c=c):
                    return _dot(hb_scr[r0:r0 + half_rows, :], w_ref[:, off + c:off + c + 2 * PROJ_BLK])

                def post(y, r0=r0, c=c):
                    out_ref[r0:r0 + half_rows, c:c + 2 * PROJ_BLK] = fn(y).astype(bf16)
                tasks.append((dot_fn, post))
        return tasks

    tasks = [(in_dot(_OFF_CQ), post_cq), (in_dot(_OFF_CKV), post_ckv)]
    tasks += wide_blocks(_OFF_NAQ, NA_WIDTH, qa_ref, headnorm64(gq_ref))
    tasks += wide_blocks(_OFF_NAK, NA_WIDTH, ka_ref, headnorm64(gk_ref))
    tasks += [(q_dot(p), q_post(p)) for p in range(MLA_PAD_WIDTH // PROJ_BLK)]
    tasks += wide_blocks(_OFF_GNA, D_MODEL, gna_ref, jax.nn.sigmoid)
    tasks += [(k_dot(p), k_post(p)) for p in range(MLA_PAD_WIDTH // PROJ_BLK)]
    tasks += wide_blocks(_OFF_NAG, NA_WIDTH, nag_ref, silu)
    tasks += [(v_dot(p), v_post(p)) for p in range(MLA_PAD_WIDTH // PROJ_BLK)]
    tasks += wide_blocks(_OFF_MG, MLA_WIDTH, mg_ref, silu)
    tasks += wide_blocks(_OFF_GMLA, D_MODEL, gmla_ref, jax.nn.sigmoid)
    tasks += wide_blocks(_OFF_NAV, NA_WIDTH, va_ref, lambda y: y)

    _run_tasks([(dot_fn, post_fn, False) for dot_fn, post_fn in tasks])


def _const_spec(stacked, layer):
    rest = stacked.shape[1:]
    return pl.BlockSpec((None,) + rest, lambda i: (layer,) + (0,) * len(rest), pipeline_mode=pl.Buffered(1))


def _proj_call(x2d, lp, tabs, layer, seq_len):
    n_tok = x2d.shape[0]
    tm = PROJ_TILE
    assert n_tok % tm == 0 and seq_len % tm == 0
    tiles_per_seq = seq_len // tm
    bf16 = jnp.bfloat16

    def row_spec(width):
        return pl.BlockSpec((tm, width), lambda i: (i, 0))

    def tab_spec():
        return pl.BlockSpec((None, tm, LANES), lambda i: (layer, i % tiles_per_seq, 0))

    consts = [lp["ln_g"], lp["w_in"], lp["gq2"], lp["gk2"], lp["gcq"], lp["gckv"],
              lp["wq"], lp["wqs"], lp["wk"], lp["wvt"], lp["gkf"]]
    out_widths = [NA_WIDTH, NA_WIDTH, NA_WIDTH, NA_WIDTH, MLA_PAD_WIDTH, MLA_PAD_WIDTH,
                  MLA_WIDTH, D_MODEL, D_MODEL]
    t_widths = [MLA_HEADS * MLA_VT_ROWS]
    t_specs = [pl.BlockSpec((1, w, tm), lambda i: (i // tiles_per_seq, 0, i % tiles_per_seq)) for w in t_widths]
    t_shapes = [jax.ShapeDtypeStruct((n_tok // seq_len, w, seq_len), bf16) for w in t_widths]
    return pl.pallas_call(
        _proj_kernel,
        grid=(n_tok // tm,),
        in_specs=[row_spec(D_MODEL)] + [_const_spec(c, layer) for c in consts] + [tab_spec()] * 4,
        out_specs=[row_spec(w) for w in out_widths] + t_specs,
        out_shape=[jax.ShapeDtypeStruct((n_tok, w), bf16) for w in out_widths] + t_shapes,
        scratch_shapes=[pltpu.VMEM((tm, D_MODEL), bf16),
                        pltpu.VMEM((tm, MLA_Q_LORA), bf16),
                        pltpu.VMEM((tm, 2 * LANES), bf16),
                        pltpu.VMEM((tm, LANES), jnp.float32)],
        compiler_params=pltpu.CompilerParams(
            dimension_semantics=("arbitrary",), vmem_limit_bytes=VMEM_LIMIT_BYTES),
        name="proj",
    )(x2d, *consts, tabs["tq1"], tabs["tq2"], tabs["e1"], tabs["e2"])


def _na_band_start(i, rows):
    return jnp.clip(NA_QROWS * i - NA_WIN_ROWS // 2, 0, rows - NA_BAND)


def _natt_kernel(q_ref, k_ref, v_ref, gate_ref, tbl_ref, o_ref, s_scr, p_scr):
    bf16 = jnp.bfloat16
    i = pl.program_id(1)
    rows = k_ref.shape[1] // GRID_W
    n_blk = pl.num_programs(1) * NA_SUB
    subs = range(NA_SUB)
    blks = [NA_SUB * i + sb for sb in subs]
    starts = [pl.multiple_of(_na_band_start(blk, rows) * GRID_W, GRID_W) for blk in blks]
    pats = [jnp.minimum(blk, 1) + (blk == n_blk - 1).astype(jnp.int32) for blk in blks]
    lo_half = _lane_iota() < NA_HEAD_DIM

    def pcols(head):
        return slice((head // 2) * LANES, (head // 2 + 1) * LANES)

    def qrows(sb):
        return slice(sb * NA_TQ, (sb + 1) * NA_TQ)

    def fold(x, op):
        run = None
        for t in range(x.shape[1] // LANES):
            tile = x[:, t * LANES:(t + 1) * LANES]
            run = tile if run is None else op(run, tile)
        return run

    def scores(head, sb):
        qp = q_ref[0, qrows(sb), pcols(head)]
        sel = lo_half if head % 2 == 0 else jnp.logical_not(lo_half)
        qm = jnp.where(sel, qp, jnp.zeros_like(qp))
        keys = k_ref[0, pl.ds(starts[sb], NA_TK), pcols(head)]
        s = _dot_nt(qm, keys) + tbl_ref[pats[sb], head]
        s_scr[sb, head % 2] = s
        return jnp.max(fold(s, jnp.maximum), axis=-1, keepdims=True)

    def probs(head, sb, m):
        p = jnp.exp2(s_scr[sb, head % 2] - m)
        p_scr[sb, head % 2] = p.astype(bf16)
        return fold(p, jnp.add)

    def pv(head, sb):
        return _dot(p_scr[sb, head % 2], v_ref[0, pl.ds(starts[sb], NA_TK), pcols(head)])

    def finish(head, sb, acc, run_sum, prev):
        o = acc / jnp.sum(run_sum, axis=-1, keepdims=True)
        if head % 2 == 0:
            return o
        pair = jnp.where(lo_half, prev, o)
        gate = gate_ref[0, qrows(sb), pcols(head)].astype(jnp.float32)
        o_ref[0, qrows(sb), pcols(head)] = (pair * gate).astype(bf16)
        return None

    m = [scores(0, sb) for sb in subs]
    prev_sum, even_out = [None] * NA_SUB, [None] * NA_SUB
    for head in range(NA_HEADS):
        m_next = [scores(head + 1, sb) for sb in subs] if head + 1 < NA_HEADS else None
        acc = [pv(head - 1, sb) for sb in subs] if head >= 1 else None
        cur_sum = [probs(head, sb, m[sb]) for sb in subs]
        if head >= 1:
            even_out = [finish(head - 1, sb, acc[sb], prev_sum[sb], even_out[sb]) for sb in subs]
        prev_sum, m = cur_sum, m_next
    for sb in subs:
        finish(NA_HEADS - 1, sb, pv(NA_HEADS - 1, sb), prev_sum[sb], even_out[sb])


def _natt_call(qa, ka, va, nag, tbl, layer):
    bsz, seq_len, _ = qa.shape
    n_blk = seq_len // NA_TQ
    assert n_blk % NA_SUB == 0
    blk = pl.BlockSpec((1, NA_SUB * NA_TQ, NA_WIDTH), lambda b, i: (b, i, 0))
    full = pl.BlockSpec((1, seq_len, NA_WIDTH), lambda b, i: (b, 0, 0))
    return pl.pallas_call(
        _natt_kernel,
        grid=(bsz, n_blk // NA_SUB),
        in_specs=[blk, full, full, blk,
                  pl.BlockSpec((None,) + tbl.shape[1:], lambda b, i: (layer, 0, 0, 0, 0),
                               pipeline_mode=pl.Buffered(1))],
        out_specs=blk,
        out_shape=jax.ShapeDtypeStruct((bsz, seq_len, NA_WIDTH), jnp.bfloat16),
        scratch_shapes=[pltpu.VMEM((NA_SUB, 2, NA_TQ, NA_TK), jnp.float32),
                        pltpu.VMEM((NA_SUB, 2, NA_TQ, NA_TK), jnp.bfloat16)],
        compiler_params=pltpu.CompilerParams(
            dimension_semantics=("arbitrary", "arbitrary"), vmem_limit_bytes=VMEM_LIMIT_BYTES),
        name="natt",
    )(qa, ka, va, nag, tbl)


def _mla_kernel(q_ref, k_ref, vt_ref, gate_ref, o_ref, s_scr, p_scr):
    bf16 = jnp.bfloat16
    seq_len = k_ref.shape[1]
    n_chunks = seq_len // MLA_KC
    n_heads = q_ref.shape[2] // HEAD_PAD
    lo_half = _lane_iota() < MLA_V
    sublanes = 8

    def hcols(head):
        return slice(head * HEAD_PAD, (head + 1) * HEAD_PAD)

    def kslice(c):
        return slice(c * MLA_KC, (c + 1) * MLA_KC)

    def qk_chunk(head, c, run_max):
        s = _dot_nt(k_ref[0, kslice(c), hcols(head)], q_ref[0, :, hcols(head)])
        s_scr[head % 2, kslice(c), :] = s
        part = jnp.max(s.reshape(MLA_KC // sublanes, sublanes, MLA_TQ), axis=0)
        return part if run_max is None else jnp.maximum(run_max, part)

    def exp_chunk(head, c, m):
        p_scr[head % 2, kslice(c), :] = jnp.exp2(s_scr[head % 2, kslice(c), :] - m).astype(bf16)

    def pv_chunk(head, c, acc):
        vrows = slice(head * MLA_VT_ROWS, (head + 1) * MLA_VT_ROWS)
        o = _dot(vt_ref[0, vrows, kslice(c)], p_scr[head % 2, kslice(c), :])
        return o if acc is None else acc + o

    def finish(head, acc, prev):
        ones_row = _mla_ones_row(head % 2)
        o = (acc / acc[ones_row:ones_row + 1, :]).T
        if head % 2 == 0:
            return o
        cols = slice((head // 2) * LANES, (head // 2 + 1) * LANES)
        pair = jnp.where(lo_half, prev, o)
        o_ref[0, :, cols] = (pair * gate_ref[0, :, cols].astype(jnp.float32)).astype(bf16)
        return None

    run = None
    for c in range(n_chunks):
        run = qk_chunk(0, c, run)
    m = jnp.max(run, axis=0, keepdims=True)
    acc, even_out = None, None
    for head in range(n_heads):
        run, next_acc = None, None
        for c in range(n_chunks):
            if head + 1 < n_heads:
                run = qk_chunk(head + 1, c, run)
            if head >= 1:
                next_acc = pv_chunk(head - 1, c, next_acc)
            exp_chunk(head, c, m)
        if head >= 1:
            even_out = finish(head - 1, next_acc, even_out)
        if head + 1 < n_heads:
            m = jnp.max(run, axis=0, keepdims=True)
    for c in range(n_chunks):
        acc = pv_chunk(n_heads - 1, c, acc)
    finish(n_heads - 1, acc, even_out)


def _mla_call(qf, kf, vbt, mg):
    bsz, seq_len, _ = qf.shape
    n_groups = MLA_HEADS // MLA_GROUP
    assert seq_len % MLA_KC == 0 and seq_len % MLA_TQ == 0 and MLA_GROUP % 2 == 0
    return pl.pallas_call(
        _mla_kernel,
        grid=(bsz, n_groups, seq_len // MLA_TQ),
        in_specs=[pl.BlockSpec((1, MLA_TQ, MLA_GROUP * HEAD_PAD), lambda b, p, j: (b, j, p)),
                  pl.BlockSpec((1, seq_len, MLA_GROUP * HEAD_PAD), lambda b, p, j: (b, 0, p)),
                  pl.BlockSpec((1, MLA_GROUP * MLA_VT_ROWS, seq_len), lambda b, p, j: (b, p, 0)),
                  pl.BlockSpec((1, MLA_TQ, MLA_GROUP * MLA_V), lambda b, p, j: (b, j, p))],
        out_specs=pl.BlockSpec((1, MLA_TQ, MLA_GROUP * MLA_V), lambda b, p, j: (b, j, p)),
        out_shape=jax.ShapeDtypeStruct((bsz, seq_len, MLA_WIDTH), jnp.bfloat16),
        scratch_shapes=[pltpu.VMEM((2, seq_len, MLA_TQ), jnp.float32),
                        pltpu.VMEM((2, seq_len, MLA_TQ), jnp.bfloat16)],
        compiler_params=pltpu.CompilerParams(
            dimension_semantics=("arbitrary", "arbitrary", "arbitrary"),
            vmem_limit_bytes=VMEM_LIMIT_BYTES),
        name="mla",
    )(qf, kf, vbt, mg)


def _run_tasks(tasks):
    pending = None
    for dot_fn, post_fn, barrier in tasks:
        if barrier and pending is not None:
            pending[0](pending[1])
            pending = None
        y = dot_fn()
        if pending is not None:
            pending[0](pending[1])
        pending = (post_fn, y)
    pending[0](pending[1])


def _out_kernel(x_ref, ana_ref, amla_ref, gna_ref, gmla_ref, wna_ref, wmla_ref, wout_ref, o_ref, y_scr):
    f32 = jnp.float32
    row_tiles = [slice(r, r + PROJ_TILE) for r in range(0, o_ref.shape[0], PROJ_TILE)]
    col_blocks = [slice(c, c + PROJ_BLK) for c in range(0, o_ref.shape[1], PROJ_BLK)]

    def branch_dots(rows, cols):
        return lambda: (_dot(ana_ref[rows, :], wna_ref[:, cols]), _dot(amla_ref[rows, :], wmla_ref[:, cols]))

    def merge(rows, cols):
        def post(us):
            y = gna_ref[rows, cols].astype(f32) * us[0] + gmla_ref[rows, cols].astype(f32) * us[1]
            y_scr[rows, cols] = y.astype(jnp.bfloat16)
        return post

    def out_dot(rows, cols):
        return lambda: _dot(y_scr[rows, :], wout_ref[:, cols])

    def residual(rows, cols):
        def post(y):
            o_ref[rows, cols] = x_ref[rows, cols] + y
        return post

    tasks = [(branch_dots(r, c), merge(r, c), False) for r in row_tiles for c in col_blocks]
    tasks += [(out_dot(r, c), residual(r, c), i == 0)
              for i, (r, c) in enumerate((r, c) for r in row_tiles for c in col_blocks)]
    _run_tasks(tasks)


def _out_call(x2d, a_na, a_mla, gna, gmla, lp, layer):
    n_tok = x2d.shape[0]
    tm = OUT_TILE
    assert n_tok % tm == 0 and tm % PROJ_TILE == 0

    def row_spec(width):
        return pl.BlockSpec((tm, width), lambda i: (i, 0))

    consts = [lp["w_o_na"], lp["w_o_mla"], lp["w_out"]]
    return pl.pallas_call(
        _out_kernel,
        grid=(n_tok // tm,),
        in_specs=[row_spec(D_MODEL), row_spec(NA_WIDTH), row_spec(MLA_WIDTH), row_spec(D_MODEL),
                  row_spec(D_MODEL)] + [_const_spec(c, layer) for c in consts],
        out_specs=row_spec(D_MODEL),
        out_shape=jax.ShapeDtypeStruct((n_tok, D_MODEL), jnp.float32),
        scratch_shapes=[pltpu.VMEM((tm, D_MODEL), jnp.bfloat16)],
        compiler_params=pltpu.CompilerParams(
            dimension_semantics=("arbitrary",), vmem_limit_bytes=VMEM_LIMIT_BYTES),
        name="out",
    )(x2d, a_na, a_mla, gna, gmla, *consts)


_N_DROW = 2 * NA_WIN_ROWS - 1
_N_DCOL = 2 * NA_WIN_COLS - 1


def _na_table_static(seq_len):
    rows = seq_len // GRID_W
    n_blk = rows // NA_QROWS
    qc = np.arange(GRID_W)
    kj = np.arange(GRID_W)
    cs = np.clip(qc - NA_WIN_COLS // 2, 0, GRID_W - NA_WIN_COLS)
    col_valid = (kj[None, :] >= cs[:, None]) & (kj[None, :] < cs[:, None] + NA_WIN_COLS)
    dcol = kj[None, :] - qc[:, None] + (NA_WIN_COLS - 1)
    onehot = (dcol[None] == np.arange(_N_DCOL)[:, None, None]) & col_valid[None]
    col_mask = np.where(col_valid, 0.0, MASK_VALUE).astype(np.float32)
    qr = np.arange(NA_QROWS)
    kn = np.arange(NA_BAND)
    idx = []
    for i in (0, 1, n_blk - 1):
        band = int(np.clip(NA_QROWS * i - NA_WIN_ROWS // 2, 0, rows - NA_BAND))
        r = NA_QROWS * i + qr
        rs = np.clip(r - NA_WIN_ROWS // 2, 0, rows - NA_WIN_ROWS)
        krow = band + kn
        valid = (krow[None, :] >= rs[:, None]) & (krow[None, :] < rs[:, None] + NA_WIN_ROWS)
        drow = krow[None, :] - r[:, None] + (NA_WIN_ROWS - 1)
        idx.append(np.where(valid, drow, _N_DROW))
    return onehot.astype(np.float32), col_mask, np.stack(idx).astype(np.int32)


def _na_table(rel_bias, onehot, col_mask, idx):
    blocks = jnp.einsum("hdc,cqj->hdqj", rel_bias, onehot, precision=lax.Precision.HIGHEST) + col_mask
    blocks = jnp.concatenate(
        [blocks, jnp.full((NA_HEADS, 1, GRID_W, GRID_W), MASK_VALUE, jnp.float32)], axis=1) * LOG2E
    tbl = jnp.take(blocks, idx.reshape(-1), axis=1)
    tbl = tbl.reshape(NA_HEADS, 3, NA_QROWS, NA_BAND, GRID_W, GRID_W)
    return tbl.transpose(1, 0, 2, 4, 3, 5).reshape(3, NA_HEADS, NA_TQ, NA_TK)


def _rope_tables(seq_len):
    t = jnp.arange(seq_len)
    row = (t // GRID_W).astype(jnp.float32)
    col = (t % GRID_W).astype(jnp.float32)
    half = MLA_ROPE // 2
    n_freq = half // 2
    inv = jnp.power(jnp.float32(ROPE_BASE), -jnp.arange(n_freq, dtype=jnp.float32) / n_freq)
    ang = jnp.concatenate([row[:, None] * inv, col[:, None] * inv], axis=-1)
    return jnp.cos(ang), jnp.sin(ang)


def _place(cols_to_blocks, total):
    parts = []
    for arr, width in cols_to_blocks:
        pad = width - arr.shape[-1]
        parts.append(jnp.pad(arr, [(0, 0)] * (arr.ndim - 1) + [(0, pad)]) if pad else arr)
    out = jnp.concatenate(parts, axis=-1)
    assert out.shape[-1] == total
    return out


def _layer_params(p, cos, sin):
    f32, bf16 = jnp.float32, jnp.bfloat16
    half = MLA_ROPE // 2
    w_in = p["w_in"]
    o = 0
    pieces = {}
    for name, size in (("naq", NA_WIDTH), ("nak", NA_WIDTH), ("nav", NA_WIDTH), ("nag", NA_WIDTH),
                       ("cq", MLA_Q_LORA), ("ckv", MLA_KV_LORA), ("kpe", MLA_ROPE),
                       ("mg", MLA_WIDTH), ("gna", D_MODEL), ("gmla", D_MODEL)):
        pieces[name] = w_in[:, o:o + size]
        o += size
    kpe = pieces["kpe"]
    kpe_sw = jnp.concatenate([kpe[:, half:], kpe[:, :half]], axis=-1)
    ckvx = _place([(pieces["ckv"], LANES), (jnp.concatenate([kpe, kpe_sw], axis=-1), LANES)], 2 * LANES)
    w_in_packed = jnp.concatenate(
        [pieces["naq"], pieces["nak"], pieces["nav"], pieces["nag"], pieces["cq"], ckvx,
         pieces["mg"], pieces["gna"], pieces["gmla"]], axis=-1).astype(bf16)
    assert w_in_packed.shape[-1] == _W_IN_COLS

    w_uq = p["w_uq"].reshape(MLA_Q_LORA, MLA_HEADS, MLA_QK_DIM)
    nope, x1, x2 = w_uq[..., :MLA_NOPE], w_uq[..., MLA_NOPE:MLA_NOPE + half], w_uq[..., MLA_NOPE + half:]
    zq = jnp.zeros_like(nope)
    wq = _place([(jnp.concatenate([nope, x1, x2], -1), HEAD_PAD)], HEAD_PAD)
    wqs = _place([(jnp.concatenate([zq, x2, x1], -1), HEAD_PAD)], HEAD_PAD)
    wq = wq.reshape(MLA_Q_LORA, MLA_PAD_WIDTH).astype(bf16)
    wqs = wqs.reshape(MLA_Q_LORA, MLA_PAD_WIDTH).astype(bf16)

    w_ukv = p["w_ukv"].reshape(MLA_KV_LORA, MLA_HEADS, MLA_NOPE + MLA_V)
    k_nope = _place([(w_ukv[..., :MLA_NOPE], HEAD_PAD)], HEAD_PAD).reshape(MLA_KV_LORA, MLA_PAD_WIDTH)
    eye = jnp.eye(MLA_ROPE, dtype=f32)
    put = jnp.pad(eye, ((0, 0), (MLA_NOPE, HEAD_PAD - MLA_QK_DIM)))
    put = jnp.tile(put, (1, MLA_HEADS))
    wk = jnp.concatenate([k_nope, put, put, jnp.zeros((2 * LANES - MLA_KV_LORA - 2 * MLA_ROPE,
                                                        MLA_PAD_WIDTH), f32)], axis=0).astype(bf16)
    w_uv = w_ukv[..., MLA_NOPE:].reshape(MLA_KV_LORA, MLA_HEADS // 2, 2, MLA_V)
    zv = jnp.zeros_like(w_uv[:, :, 0])
    wvt = jnp.stack([jnp.concatenate([w_uv[:, :, 0], zv], -1), jnp.concatenate([zv, w_uv[:, :, 1]], -1)], axis=2)
    wvt = wvt.reshape(MLA_KV_LORA, MLA_HEADS * MLA_VT_ROWS).T.astype(bf16)

    gq = p["mla_q_norm"]
    gk = p["mla_k_norm"]
    scale = MLA_QK_DIM ** -0.5 * LOG2E
    seq_len = cos.shape[0]
    ones = jnp.ones((seq_len, 1), f32)
    tq1 = _place([(ones * gq[None, :MLA_NOPE], MLA_NOPE),
                  (cos * gq[None, MLA_NOPE:MLA_NOPE + half], half),
                  (cos * gq[None, MLA_NOPE + half:], half)], MLA_QK_DIM) * scale
    tq2 = _place([(jnp.zeros((seq_len, MLA_NOPE), f32), MLA_NOPE),
                  (-sin * gq[None, MLA_NOPE + half:], half),
                  (sin * gq[None, MLA_NOPE:MLA_NOPE + half], half)], MLA_QK_DIM) * scale
    e1 = jnp.concatenate([cos * gk[None, MLA_NOPE:MLA_NOPE + half], cos * gk[None, MLA_NOPE + half:]], -1)
    e2 = jnp.concatenate([-sin * gk[None, MLA_NOPE + half:], sin * gk[None, MLA_NOPE:MLA_NOPE + half]], -1)
    tabs = {"tq1": _place([(tq1, LANES)], LANES), "tq2": _place([(tq2, LANES)], LANES),
            "e1": _place([(e1, LANES)], LANES), "e2": _place([(e2, LANES)], LANES)}
    gkf = _place([(gk[None, :MLA_NOPE], MLA_NOPE), (jnp.ones((1, MLA_ROPE), f32), LANES - MLA_NOPE)], LANES)

    na_scale = NA_HEAD_DIM ** -0.5 * LOG2E
    lp = {
        "ln_g": p["ln_g"][None, :],
        "w_in": w_in_packed,
        "gq2": jnp.tile(p["na_q_norm"], 2)[None, :] * na_scale,
        "gk2": jnp.tile(p["na_k_norm"], 2)[None, :],
        "gcq": p["mla_cq_norm"][None, :],
        "gckv": p["mla_ckv_norm"][None, :],
        "wq": wq, "wqs": wqs, "wk": wk, "wvt": wvt, "gkf": gkf,
        "w_o_na": p["w_o_na"].astype(bf16),
        "w_o_mla": p["w_o_mla"].astype(bf16),
        "w_out": p["w_out"].astype(bf16),
    }
    return lp, tabs


@jax.jit
def _forward(x, p):
    bsz, seq_len, _ = x.shape
    depth = p["w_in"].shape[0]
    cos, sin = _rope_tables(seq_len)
    onehot, col_mask, tbl_idx = _na_table_static(seq_len)
    x2d = x.reshape(bsz * seq_len, D_MODEL)
    lp, tabs = jax.vmap(lambda pl_: _layer_params(pl_, cos, sin))(p)
    tbl = jax.vmap(lambda rb: _na_table(rb, onehot, col_mask, tbl_idx))(p["na_rel_bias"])
    for layer in range(depth):
        qa, ka, va, nag, qf, kf, mg, gna, gmla, vbt = _proj_call(x2d, lp, tabs, layer, seq_len)

        def b3(a):
            return a.reshape(bsz, seq_len, a.shape[-1])

        a_na = _natt_call(b3(qa), b3(ka), b3(va), b3(nag), tbl, layer)
        a_mla = _mla_call(b3(qf), b3(kf), vbt, b3(mg))
        x2d = _out_call(x2d, a_na.reshape(-1, NA_WIDTH), a_mla.reshape(-1, MLA_WIDTH), gna, gmla, lp, layer)
    return x2d.reshape(bsz, seq_len, D_MODEL)


def kernel(x, ln_g, w_in, na_q_norm, na_k_norm, na_rel_bias, mla_cq_norm, mla_ckv_norm, w_uq, w_ukv,
           mla_q_norm, mla_k_norm, w_o_na, w_o_mla, w_out):
    params = dict(ln_g=ln_g, w_in=w_in, na_q_norm=na_q_norm, na_k_norm=na_k_norm,
                  na_rel_bias=na_rel_bias, mla_cq_norm=mla_cq_norm, mla_ckv_norm=mla_ckv_norm,
                  w_uq=w_uq, w_ukv=w_ukv, mla_q_norm=mla_q_norm, mla_k_norm=mla_k_norm,
                  w_o_na=w_o_na, w_o_mla=w_o_mla, w_out=w_out)
    return _forward(x, params)
```

```python
import numpy as np
import jax
import jax.numpy as jnp
from jax import lax
from jax.experimental import pallas as pl
from jax.experimental.pallas import tpu as pltpu

D_MODEL = 1024
GRID_W = 64
NA_HEADS = 8
NA_HEAD_DIM = 64
NA_WIDTH = NA_HEADS * NA_HEAD_DIM
NA_WIN_ROWS = 8
NA_WIN_COLS = 16
MLA_HEADS = 8
MLA_NOPE = 64
MLA_ROPE = 32
MLA_QK_DIM = MLA_NOPE + MLA_ROPE
MLA_V = 64
MLA_Q_LORA = 256
MLA_KV_LORA = 128
MLA_WIDTH = MLA_HEADS * MLA_V
ROPE_BASE = 10000.0
EPS = 1e-6

LANES = 128
HEAD_PAD = LANES
MLA_PAD_WIDTH = MLA_HEADS * HEAD_PAD
MLA_VT_ROWS = 128
VMEM_LIMIT_BYTES = 56 * 1024 * 1024

_OFF_NAQ = 0
_OFF_NAK = _OFF_NAQ + NA_WIDTH
_OFF_NAV = _OFF_NAK + NA_WIDTH
_OFF_NAG = _OFF_NAV + NA_WIDTH
_OFF_CQ = _OFF_NAG + NA_WIDTH
_OFF_CKV = _OFF_CQ + MLA_Q_LORA
_OFF_MG = _OFF_CKV + 2 * LANES
_OFF_GNA = _OFF_MG + MLA_WIDTH
_OFF_GMLA = _OFF_GNA + D_MODEL
_W_IN_COLS = _OFF_GMLA + D_MODEL

PROJ_TILE = 512
OUT_TILE = 1024
PROJ_BLK = 256
NA_QROWS = 4
NA_BAND = 12
NA_TQ = NA_QROWS * GRID_W
NA_TK = NA_BAND * GRID_W
NA_SUB = 2
MLA_TQ = 256
MLA_KC = 1024
MLA_GROUP = 8
MASK_VALUE = -1e30
LOG2E = 1.4426950408889634

_NT_DIMS = (((1,), (1,)), ((), ()))


def _dot(a, b):
    return jnp.dot(a, b, preferred_element_type=jnp.float32)


def _dot_nt(a, b):
    return lax.dot_general(a, b, _NT_DIMS, preferred_element_type=jnp.float32)


def _lane_iota():
    return lax.broadcasted_iota(jnp.int32, (1, LANES), 1)


def _mla_ones_row(parity):
    return MLA_V if parity == 0 else 0


def _proj_kernel(x_ref, lng_ref, w_ref, gq_ref, gk_ref, gcq_ref, gckv_ref, wq_ref,
                 wk_ref, wvt_ref, gkf_ref, tq1_ref, tq2_ref, e1_ref, e2_ref,
                 qa_ref, ka_ref, va_ref, nag_ref, qf_ref, kf_ref, mg_ref,
                 gna_ref, gmla_ref, vbt_ref, hb_scr, cqn_scr, lhsk_scr, sspe_scr):
    bf16 = jnp.bfloat16
    f32 = jnp.float32
    half_rows = hb_scr.shape[0] // 2
    x = x_ref[...]
    hb_scr[...] = (x * lax.rsqrt(jnp.mean(x * x, axis=-1, keepdims=True) + EPS) * lng_ref[...]).astype(bf16)
    lo_half = _lane_iota() < NA_HEAD_DIM
    rope_lanes = _lane_iota() < MLA_ROPE
    row_iota = lax.broadcasted_iota(jnp.int32, (MLA_VT_ROWS, 1), 0)

    def tiles(y):
        return [y[:, i * LANES:(i + 1) * LANES] for i in range(y.shape[1] // LANES)]

    def in_dot(off):
        return lambda: _dot(hb_scr[...], w_ref[:, off:off + PROJ_BLK])

    def headnorm64(g_ref):
        def fn(y):
            out = []
            for pc in tiles(y):
                sq = pc * pc
                s_lo = jnp.sum(jnp.where(lo_half, sq, 0.0), axis=-1, keepdims=True)
                s_hi = jnp.sum(jnp.where(lo_half, 0.0, sq), axis=-1, keepdims=True)
                r = jnp.where(lo_half,
                              lax.rsqrt(s_lo * (1.0 / NA_HEAD_DIM) + EPS),
                              lax.rsqrt(s_hi * (1.0 / NA_HEAD_DIM) + EPS))
                out.append(pc * r * g_ref[...])
            return jnp.concatenate(out, axis=-1)
        return fn

    def silu(g):
        return g * jax.nn.sigmoid(g)

    def post_cq(y):
        cqn_scr[...] = (y * lax.rsqrt(jnp.mean(y * y, axis=-1, keepdims=True) + EPS) * gcq_ref[...]).astype(bf16)

    def post_ckv(y):
        ckv, e = tiles(y)
        lhsk_scr[:, :LANES] = (ckv * lax.rsqrt(jnp.mean(ckv * ckv, axis=-1, keepdims=True) + EPS)
                               * gckv_ref[...]).astype(bf16)
        ss_pe = jnp.sum(jnp.where(rope_lanes, e * e, 0.0), axis=-1, keepdims=True)
        sspe_scr[...] = jnp.broadcast_to(ss_pe, sspe_scr.shape)
        rot = e * e1_ref[...] + pltpu.roll(e, LANES - MLA_ROPE, axis=1) * e2_ref[...]
        rot_hi = rot.astype(bf16).astype(f32)
        rot_lo = (rot - rot_hi).astype(bf16).astype(f32)
        lhsk_scr[:, LANES:] = (rot_hi + pltpu.roll(rot_lo, MLA_ROPE, axis=1)).astype(bf16)

    def q_dot(pair):
        cols = slice(pair * PROJ_BLK, (pair + 1) * PROJ_BLK)
        return lambda: _dot(cqn_scr[...], wq_ref[:, cols])

    def q_post(pair):
        first_rope_half = _lane_iota() < MLA_NOPE + MLA_ROPE // 2

        def post(y):
            for i, a in enumerate(tiles(y)):
                b = jnp.where(first_rope_half, pltpu.roll(a, LANES - MLA_ROPE // 2, axis=1),
                              pltpu.roll(a, MLA_ROPE // 2, axis=1))
                r = lax.rsqrt(jnp.sum(a * a, axis=-1, keepdims=True) / MLA_QK_DIM + EPS)
                col0 = pair * PROJ_BLK + i * HEAD_PAD
                qf_ref[:, col0:col0 + HEAD_PAD] = ((a * tq1_ref[...] + b * tq2_ref[...]) * r).astype(bf16)
        return post

    def k_dot(pair):
        cols = slice(pair * PROJ_BLK, (pair + 1) * PROJ_BLK)
        return lambda: _dot(lhsk_scr[...], wk_ref[:, cols])

    def k_post(pair):
        def post(y):
            for i, a in enumerate(tiles(y)):
                ss = jnp.sum(jnp.where(lo_half, a * a, 0.0), axis=-1, keepdims=True) + sspe_scr[...]
                r = lax.rsqrt(ss / MLA_QK_DIM + EPS)
                col0 = pair * PROJ_BLK + i * HEAD_PAD
                kf_ref[:, col0:col0 + HEAD_PAD] = (a * gkf_ref[...] * r).astype(bf16)
        return post

    def v_dot(pair):
        rows = slice(pair * 2 * MLA_VT_ROWS, (pair + 1) * 2 * MLA_VT_ROWS)
        return lambda: _dot_nt(wvt_ref[rows, :], lhsk_scr[:, :LANES])

    def v_post(pair):
        def post(y):
            for i in range(2):
                row0 = (2 * pair + i) * MLA_VT_ROWS
                a = y[i * MLA_VT_ROWS:(i + 1) * MLA_VT_ROWS, :]
                vbt_ref[0, row0:row0 + MLA_VT_ROWS, :] = jnp.where(row_iota == _mla_ones_row(i), 1.0, a).astype(bf16)
        return post

    def wide_blocks(off, width, out_ref, fn):
        tasks = []
        for r0 in (0, half_rows):
            for c in range(0, width, 2 * PROJ_BLK):
                def dot_fn(r0=r0, c=c):
                    return _dot(hb_scr[r0:r0 + half_rows, :], w_ref[:, off + c:off + c + 2 * PROJ_BLK])

                def post(y, r0=r0, c=c):
                    out_ref[r0:r0 + half_rows, c:c + 2 * PROJ_BLK] = fn(y).astype(bf16)
                tasks.append((dot_fn, post))
        return tasks

    tasks = [(in_dot(_OFF_CQ), post_cq), (in_dot(_OFF_CKV), post_ckv)]
    tasks += wide_blocks(_OFF_NAQ, NA_WIDTH, qa_ref, headnorm64(gq_ref))
    tasks += wide_blocks(_OFF_NAK, NA_WIDTH, ka_ref, headnorm64(gk_ref))
    tasks += [(q_dot(p), q_post(p)) for p in range(MLA_PAD_WIDTH // PROJ_BLK)]
    tasks += wide_blocks(_OFF_GNA, D_MODEL, gna_ref, jax.nn.sigmoid)
    tasks += [(k_dot(p), k_post(p)) for p in range(MLA_PAD_WIDTH // PROJ_BLK)]
    tasks += wide_blocks(_OFF_NAG, NA_WIDTH, nag_ref, silu)
    tasks += [(v_dot(p), v_post(p)) for p in range(MLA_PAD_WIDTH // PROJ_BLK)]
    tasks += wide_blocks(_OFF_MG, MLA_WIDTH, mg_ref, silu)
    tasks += wide_blocks(_OFF_GMLA, D_MODEL, gmla_ref, jax.nn.sigmoid)
    tasks += wide_blocks(_OFF_NAV, NA_WIDTH, va_ref, lambda y: y)

    _run_tasks([(dot_fn, post_fn, False) for dot_fn, post_fn in tasks])


def _const_spec(stacked, layer):
    rest = stacked.shape[1:]
    return pl.BlockSpec((None,) + rest, lambda i: (layer,) + (0,) * len(rest), pipeline_mode=pl.Buffered(1))


def _proj_call(x2d, lp, tabs, layer, seq_len):
    n_tok = x2d.shape[0]
    tm = PROJ_TILE
    assert n_tok % tm == 0 and seq_len % tm == 0
    tiles_per_seq = seq_len // tm
    bf16 = jnp.bfloat16

    def row_spec(width):
        return pl.BlockSpec((tm, width), lambda i: (i, 0))

    def tab_spec():
        return pl.BlockSpec((None, tm, LANES), lambda i: (layer, i % tiles_per_seq, 0))

    consts = [lp["ln_g"], lp["w_in"], lp["gq2"], lp["gk2"], lp["gcq"], lp["gckv"],
              lp["wq"], lp["wk"], lp["wvt"], lp["gkf"]]
    out_widths = [NA_WIDTH, NA_WIDTH, NA_WIDTH, NA_WIDTH, MLA_PAD_WIDTH, MLA_PAD_WIDTH,
                  MLA_WIDTH, D_MODEL, D_MODEL]
    t_widths = [MLA_HEADS * MLA_VT_ROWS]
    t_specs = [pl.BlockSpec((1, w, tm), lambda i: (i // tiles_per_seq, 0, i % tiles_per_seq)) for w in t_widths]
    t_shapes = [jax.ShapeDtypeStruct((n_tok // seq_len, w, seq_len), bf16) for w in t_widths]
    return pl.pallas_call(
        _proj_kernel,
        grid=(n_tok // tm,),
        in_specs=[row_spec(D_MODEL)] + [_const_spec(c, layer) for c in consts] + [tab_spec()] * 4,
        out_specs=[row_spec(w) for w in out_widths] + t_specs,
        out_shape=[jax.ShapeDtypeStruct((n_tok, w), bf16) for w in out_widths] + t_shapes,
        scratch_shapes=[pltpu.VMEM((tm, D_MODEL), bf16),
                        pltpu.VMEM((tm, MLA_Q_LORA), bf16),
                        pltpu.VMEM((tm, 2 * LANES), bf16),
                        pltpu.VMEM((tm, LANES), jnp.float32)],
        compiler_params=pltpu.CompilerParams(
            dimension_semantics=("arbitrary",), vmem_limit_bytes=VMEM_LIMIT_BYTES),
        name="proj",
    )(x2d, *consts, tabs["tq1"], tabs["tq2"], tabs["e1"], tabs["e2"])


def _na_band_start(i, rows):
    return jnp.clip(NA_QROWS * i - NA_WIN_ROWS // 2, 0, rows - NA_BAND)


def _natt_kernel(q_ref, k_ref, v_ref, gate_ref, tbl_ref, o_ref, s_scr, p_scr):
    bf16 = jnp.bfloat16
    i = pl.program_id(1)
    rows = k_ref.shape[1] // GRID_W
    n_blk = pl.num_programs(1) * NA_SUB
    subs = range(NA_SUB)
    blks = [NA_SUB * i + sb for sb in subs]
    starts = [pl.multiple_of(_na_band_start(blk, rows) * GRID_W, GRID_W) for blk in blks]
    pats = [jnp.minimum(blk, 1) + (blk == n_blk - 1).astype(jnp.int32) for blk in blks]
    lo_half = _lane_iota() < NA_HEAD_DIM

    def pcols(head):
        return slice((head // 2) * LANES, (head // 2 + 1) * LANES)

    def qrows(sb):
        return slice(sb * NA_TQ, (sb + 1) * NA_TQ)

    def fold(x, op):
        run = None
        for t in range(x.shape[1] // LANES):
            tile = x[:, t * LANES:(t + 1) * LANES]
            run = tile if run is None else op(run, tile)
        return run

    def scores(head, sb):
        qp = q_ref[0, qrows(sb), pcols(head)]
        sel = lo_half if head % 2 == 0 else jnp.logical_not(lo_half)
        qm = jnp.where(sel, qp, jnp.zeros_like(qp))
        keys = k_ref[0, pl.ds(starts[sb], NA_TK), pcols(head)]
        s = _dot_nt(qm, keys) + tbl_ref[pats[sb], head]
        s_scr[sb, head % 2] = s
        return jnp.max(fold(s, jnp.maximum), axis=-1, keepdims=True)

    def probs(head, sb, m):
        p = jnp.exp2(s_scr[sb, head % 2] - m)
        p_scr[sb, head % 2] = p.astype(bf16)
        return fold(p, jnp.add)

    def pv(head, sb):
        return _dot(p_scr[sb, head % 2], v_ref[0, pl.ds(starts[sb], NA_TK), pcols(head)])

    def finish(head, sb, acc, run_sum, prev):
        o = acc / jnp.sum(run_sum, axis=-1, keepdims=True)
        if head % 2 == 0:
            return o
        pair = jnp.where(lo_half, prev, o)
        gate = gate_ref[0, qrows(sb), pcols(head)].astype(jnp.float32)
        o_ref[0, qrows(sb), pcols(head)] = (pair * gate).astype(bf16)
        return None

    m = [scores(0, sb) for sb in subs]
    prev_sum, even_out = [None] * NA_SUB, [None] * NA_SUB
    for head in range(NA_HEADS):
        m_next = [scores(head + 1, sb) for sb in subs] if head + 1 < NA_HEADS else None
        acc = [pv(head - 1, sb) for sb in subs] if head >= 1 else None
        cur_sum = [probs(head, sb, m[sb]) for sb in subs]
        if head >= 1:
            even_out = [finish(head - 1, sb, acc[sb], prev_sum[sb], even_out[sb]) for sb in subs]
        prev_sum, m = cur_sum, m_next
    for sb in subs:
        finish(NA_HEADS - 1, sb, pv(NA_HEADS - 1, sb), prev_sum[sb], even_out[sb])


def _natt_call(qa, ka, va, nag, tbl, layer):
    bsz, seq_len, _ = qa.shape
    n_blk = seq_len // NA_TQ
    assert n_blk % NA_SUB == 0
    blk = pl.BlockSpec((1, NA_SUB * NA_TQ, NA_WIDTH), lambda b, i: (b, i, 0))
    full = pl.BlockSpec((1, seq_len, NA_WIDTH), lambda b, i: (b, 0, 0))
    return pl.pallas_call(
        _natt_kernel,
        grid=(bsz, n_blk // NA_SUB),
        in_specs=[blk, full, full, blk,
                  pl.BlockSpec((None,) + tbl.shape[1:], lambda b, i: (layer, 0, 0, 0, 0),
                               pipeline_mode=pl.Buffered(1))],
        out_specs=blk,
        out_shape=jax.ShapeDtypeStruct((bsz, seq_len, NA_WIDTH), jnp.bfloat16),
        scratch_shapes=[pltpu.VMEM((NA_SUB, 2, NA_TQ, NA_TK), jnp.float32),
                        pltpu.VMEM((NA_SUB, 2, NA_TQ, NA_TK), jnp.bfloat16)],
        compiler_params=pltpu.CompilerParams(
            dimension_semantics=("arbitrary", "arbitrary"), vmem_limit_bytes=VMEM_LIMIT_BYTES),
        name="natt",
    )(qa, ka, va, nag, tbl)


def _mla_kernel(q_ref, k_ref, vt_ref, gate_ref, o_ref, s_scr, p_scr):
    bf16 = jnp.bfloat16
    seq_len = k_ref.shape[1]
    n_chunks = seq_len // MLA_KC
    n_heads = q_ref.shape[2] // HEAD_PAD
    lo_half = _lane_iota() < MLA_V
    sublanes = 8

    def hcols(head):
        return slice(head * HEAD_PAD, (head + 1) * HEAD_PAD)

    def kslice(c):
        return slice(c * MLA_KC, (c + 1) * MLA_KC)

    def qk_chunk(head, c, run_max):
        s = _dot_nt(k_ref[0, kslice(c), hcols(head)], q_ref[0, :, hcols(head)])
        s_scr[head % 2, kslice(c), :] = s
        part = jnp.max(s.reshape(MLA_KC // sublanes, sublanes, MLA_TQ), axis=0)
        return part if run_max is None else jnp.maximum(run_max, part)

    def exp_chunk(head, c, m):
        p_scr[head % 2, kslice(c), :] = jnp.exp2(s_scr[head % 2, kslice(c), :] - m).astype(bf16)

    def pv_chunk(head, c, acc):
        vrows = slice(head * MLA_VT_ROWS, (head + 1) * MLA_VT_ROWS)
        o = _dot(vt_ref[0, vrows, kslice(c)], p_scr[head % 2, kslice(c), :])
        return o if acc is None else acc + o

    def finish(head, acc, prev):
        ones_row = _mla_ones_row(head % 2)
        o = (acc / acc[ones_row:ones_row + 1, :]).T
        if head % 2 == 0:
            return o
        cols = slice((head // 2) * LANES, (head // 2 + 1) * LANES)
        pair = jnp.where(lo_half, prev, o)
        o_ref[0, :, cols] = (pair * gate_ref[0, :, cols].astype(jnp.float32)).astype(bf16)
        return None

    run = None
    for c in range(n_chunks):
        run = qk_chunk(0, c, run)
    m = jnp.max(run, axis=0, keepdims=True)
    acc, even_out = None, None
    for head in range(n_heads):
        run, next_acc = None, None
        for c in range(n_chunks):
            if head + 1 < n_heads:
                run = qk_chunk(head + 1, c, run)
            if head >= 1:
                next_acc = pv_chunk(head - 1, c, next_acc)
            exp_chunk(head, c, m)
        if head >= 1:
            even_out = finish(head - 1, next_acc, even_out)
        if head + 1 < n_heads:
            m = jnp.max(run, axis=0, keepdims=True)
    for c in range(n_chunks):
        acc = pv_chunk(n_heads - 1, c, acc)
    finish(n_heads - 1, acc, even_out)


def _mla_call(qf, kf, vbt, mg):
    bsz, seq_len, _ = qf.shape
    n_groups = MLA_HEADS // MLA_GROUP
    assert seq_len % MLA_KC == 0 and seq_len % MLA_TQ == 0 and MLA_GROUP % 2 == 0
    return pl.pallas_call(
        _mla_kernel,
        grid=(bsz, n_groups, seq_len // MLA_TQ),
        in_specs=[pl.BlockSpec((1, MLA_TQ, MLA_GROUP * HEAD_PAD), lambda b, p, j: (b, j, p)),
                  pl.BlockSpec((1, seq_len, MLA_GROUP * HEAD_PAD), lambda b, p, j: (b, 0, p)),
                  pl.BlockSpec((1, MLA_GROUP * MLA_VT_ROWS, seq_len), lambda b, p, j: (b, p, 0)),
                  pl.BlockSpec((1, MLA_TQ, MLA_GROUP * MLA_V), lambda b, p, j: (b, j, p))],
        out_specs=pl.BlockSpec((1, MLA_TQ, MLA_GROUP * MLA_V), lambda b, p, j: (b, j, p)),
        out_shape=jax.ShapeDtypeStruct((bsz, seq_len, MLA_WIDTH), jnp.bfloat16),
        scratch_shapes=[pltpu.VMEM((2, seq_len, MLA_TQ), jnp.float32),
                        pltpu.VMEM((2, seq_len, MLA_TQ), jnp.bfloat16)],
        compiler_params=pltpu.CompilerParams(
            dimension_semantics=("arbitrary", "arbitrary", "arbitrary"),
            vmem_limit_bytes=VMEM_LIMIT_BYTES),
        name="mla",
    )(qf, kf, vbt, mg)


def _run_tasks(tasks):
    pending = None
    for dot_fn, post_fn, barrier in tasks:
        if barrier and pending is not None:
            pending[0](pending[1])
            pending = None
        y = dot_fn()
        if pending is not None:
            pending[0](pending[1])
        pending = (post_fn, y)
    pending[0](pending[1])


def _out_kernel(x_ref, ana_ref, amla_ref, gna_ref, gmla_ref, wna_ref, wmla_ref, wout_ref, o_ref, y_scr):
    f32 = jnp.float32
    row_tiles = [slice(r, r + PROJ_TILE) for r in range(0, o_ref.shape[0], PROJ_TILE)]
    col_blocks = [slice(c, c + PROJ_BLK) for c in range(0, o_ref.shape[1], PROJ_BLK)]

    def branch_dots(rows, cols):
        return lambda: (_dot(ana_ref[rows, :], wna_ref[:, cols]), _dot(amla_ref[rows, :], wmla_ref[:, cols]))

    def merge(rows, cols):
        def post(us):
            y = gna_ref[rows, cols].astype(f32) * us[0] + gmla_ref[rows, cols].astype(f32) * us[1]
            y_scr[rows, cols] = y.astype(jnp.bfloat16)
        return post

    def out_dot(rows, cols):
        return lambda: _dot(y_scr[rows, :], wout_ref[:, cols])

    def residual(rows, cols):
        def post(y):
            o_ref[rows, cols] = x_ref[rows, cols] + y
        return post

    tasks = [(branch_dots(r, c), merge(r, c), False) for r in row_tiles for c in col_blocks]
    tasks += [(out_dot(r, c), residual(r, c), i == 0)
              for i, (r, c) in enumerate((r, c) for r in row_tiles for c in col_blocks)]
    _run_tasks(tasks)


def _out_call(x2d, a_na, a_mla, gna, gmla, lp, layer):
    n_tok = x2d.shape[0]
    tm = OUT_TILE
    assert n_tok % tm == 0 and tm % PROJ_TILE == 0

    def row_spec(width):
        return pl.BlockSpec((tm, width), lambda i: (i, 0))

    consts = [lp["w_o_na"], lp["w_o_mla"], lp["w_out"]]
    return pl.pallas_call(
        _out_kernel,
        grid=(n_tok // tm,),
        in_specs=[row_spec(D_MODEL), row_spec(NA_WIDTH), row_spec(MLA_WIDTH), row_spec(D_MODEL),
                  row_spec(D_MODEL)] + [_const_spec(c, layer) for c in consts],
        out_specs=row_spec(D_MODEL),
        out_shape=jax.ShapeDtypeStruct((n_tok, D_MODEL), jnp.float32),
        scratch_shapes=[pltpu.VMEM((tm, D_MODEL), jnp.bfloat16)],
        compiler_params=pltpu.CompilerParams(
            dimension_semantics=("arbitrary",), vmem_limit_bytes=VMEM_LIMIT_BYTES),
        name="out",
    )(x2d, a_na, a_mla, gna, gmla, *consts)


_N_DROW = 2 * NA_WIN_ROWS - 1
_N_DCOL = 2 * NA_WIN_COLS - 1


def _na_table_static(seq_len):
    rows = seq_len // GRID_W
    n_blk = rows // NA_QROWS
    qc = np.arange(GRID_W)
    kj = np.arange(GRID_W)
    cs = np.clip(qc - NA_WIN_COLS // 2, 0, GRID_W - NA_WIN_COLS)
    col_valid = (kj[None, :] >= cs[:, None]) & (kj[None, :] < cs[:, None] + NA_WIN_COLS)
    dcol = kj[None, :] - qc[:, None] + (NA_WIN_COLS - 1)
    onehot = (dcol[None] == np.arange(_N_DCOL)[:, None, None]) & col_valid[None]
    col_mask = np.where(col_valid, 0.0, MASK_VALUE).astype(np.float32)
    qr = np.arange(NA_QROWS)
    kn = np.arange(NA_BAND)
    idx = []
    for i in (0, 1, n_blk - 1):
        band = int(np.clip(NA_QROWS * i - NA_WIN_ROWS // 2, 0, rows - NA_BAND))
        r = NA_QROWS * i + qr
        rs = np.clip(r - NA_WIN_ROWS // 2, 0, rows - NA_WIN_ROWS)
        krow = band + kn
        valid = (krow[None, :] >= rs[:, None]) & (krow[None, :] < rs[:, None] + NA_WIN_ROWS)
        drow = krow[None, :] - r[:, None] + (NA_WIN_ROWS - 1)
        idx.append(np.where(valid, drow, _N_DROW))
    return onehot.astype(np.float32), col_mask, np.stack(idx).astype(np.int32)


def _na_table(rel_bias, onehot, col_mask, idx):
    blocks = jnp.einsum("hdc,cqj->hdqj", rel_bias, onehot, precision=lax.Precision.HIGHEST) + col_mask
    blocks = jnp.concatenate(
        [blocks, jnp.full((NA_HEADS, 1, GRID_W, GRID_W), MASK_VALUE, jnp.float32)], axis=1) * LOG2E
    tbl = jnp.take(blocks, idx.reshape(-1), axis=1)
    tbl = tbl.reshape(NA_HEADS, 3, NA_QROWS, NA_BAND, GRID_W, GRID_W)
    return tbl.transpose(1, 0, 2, 4, 3, 5).reshape(3, NA_HEADS, NA_TQ, NA_TK)


def _rope_tables(seq_len):
    t = jnp.arange(seq_len)
    row = (t // GRID_W).astype(jnp.float32)
    col = (t % GRID_W).astype(jnp.float32)
    half = MLA_ROPE // 2
    n_freq = half // 2
    inv = jnp.power(jnp.float32(ROPE_BASE), -jnp.arange(n_freq, dtype=jnp.float32) / n_freq)
    ang = jnp.concatenate([row[:, None] * inv, col[:, None] * inv], axis=-1)
    return jnp.cos(ang), jnp.sin(ang)


def _place(cols_to_blocks, total):
    parts = []
    for arr, width in cols_to_blocks:
        pad = width - arr.shape[-1]
        parts.append(jnp.pad(arr, [(0, 0)] * (arr.ndim - 1) + [(0, pad)]) if pad else arr)
    out = jnp.concatenate(parts, axis=-1)
    assert out.shape[-1] == total
    return out


def _layer_params(p, cos, sin):
    f32, bf16 = jnp.float32, jnp.bfloat16
    half = MLA_ROPE // 2
    w_in = p["w_in"]
    o = 0
    pieces = {}
    for name, size in (("naq", NA_WIDTH), ("nak", NA_WIDTH), ("nav", NA_WIDTH), ("nag", NA_WIDTH),
                       ("cq", MLA_Q_LORA), ("ckv", MLA_KV_LORA), ("kpe", MLA_ROPE),
                       ("mg", MLA_WIDTH), ("gna", D_MODEL), ("gmla", D_MODEL)):
        pieces[name] = w_in[:, o:o + size]
        o += size
    kpe = pieces["kpe"]
    kpe_sw = jnp.concatenate([kpe[:, half:], kpe[:, :half]], axis=-1)
    ckvx = _place([(pieces["ckv"], LANES), (jnp.concatenate([kpe, kpe_sw], axis=-1), LANES)], 2 * LANES)
    w_in_packed = jnp.concatenate(
        [pieces["naq"], pieces["nak"], pieces["nav"], pieces["nag"], pieces["cq"], ckvx,
         pieces["mg"], pieces["gna"], pieces["gmla"]], axis=-1).astype(bf16)
    assert w_in_packed.shape[-1] == _W_IN_COLS

    w_uq = p["w_uq"].reshape(MLA_Q_LORA, MLA_HEADS, MLA_QK_DIM)
    wq = _place([(w_uq, HEAD_PAD)], HEAD_PAD).reshape(MLA_Q_LORA, MLA_PAD_WIDTH).astype(bf16)

    w_ukv = p["w_ukv"].reshape(MLA_KV_LORA, MLA_HEADS, MLA_NOPE + MLA_V)
    k_nope = _place([(w_ukv[..., :MLA_NOPE], HEAD_PAD)], HEAD_PAD).reshape(MLA_KV_LORA, MLA_PAD_WIDTH)
    eye = jnp.eye(MLA_ROPE, dtype=f32)
    put = jnp.pad(eye, ((0, 0), (MLA_NOPE, HEAD_PAD - MLA_QK_DIM)))
    put = jnp.tile(put, (1, MLA_HEADS))
    wk = jnp.concatenate([k_nope, put, put, jnp.zeros((2 * LANES - MLA_KV_LORA - 2 * MLA_ROPE,
                                                        MLA_PAD_WIDTH), f32)], axis=0).astype(bf16)
    w_uv = w_ukv[..., MLA_NOPE:].reshape(MLA_KV_LORA, MLA_HEADS // 2, 2, MLA_V)
    zv = jnp.zeros_like(w_uv[:, :, 0])
    wvt = jnp.stack([jnp.concatenate([w_uv[:, :, 0], zv], -1), jnp.concatenate([zv, w_uv[:, :, 1]], -1)], axis=2)
    wvt = wvt.reshape(MLA_KV_LORA, MLA_HEADS * MLA_VT_ROWS).T.astype(bf16)

    gq = p["mla_q_norm"]
    gk = p["mla_k_norm"]
    scale = MLA_QK_DIM ** -0.5 * LOG2E
    seq_len = cos.shape[0]
    ones = jnp.ones((seq_len, 1), f32)
    tq1 = _place([(ones * gq[None, :MLA_NOPE], MLA_NOPE),
                  (cos * gq[None, MLA_NOPE:MLA_NOPE + half], half),
                  (cos * gq[None, MLA_NOPE + half:], half)], MLA_QK_DIM) * scale
    tq2 = _place([(jnp.zeros((seq_len, MLA_NOPE), f32), MLA_NOPE),
                  (-sin * gq[None, MLA_NOPE + half:], half),
                  (sin * gq[None, MLA_NOPE:MLA_NOPE + half], half)], MLA_QK_DIM) * scale
    e1 = jnp.concatenate([cos * gk[None, MLA_NOPE:MLA_NOPE + half], cos * gk[None, MLA_NOPE + half:]], -1)
    e2 = jnp.concatenate([-sin * gk[None, MLA_NOPE + half:], sin * gk[None, MLA_NOPE:MLA_NOPE + half]], -1)
    tabs = {"tq1": _place([(tq1, LANES)], LANES), "tq2": _place([(tq2, LANES)], LANES),
            "e1": _place([(e1, LANES)], LANES), "e2": _place([(e2, LANES)], LANES)}
    gkf = _place([(gk[None, :MLA_NOPE], MLA_NOPE), (jnp.ones((1, MLA_ROPE), f32), LANES - MLA_NOPE)], LANES)

    na_scale = NA_HEAD_DIM ** -0.5 * LOG2E
    lp = {
        "ln_g": p["ln_g"][None, :],
        "w_in": w_in_packed,
        "gq2": jnp.tile(p["na_q_norm"], 2)[None, :] * na_scale,
        "gk2": jnp.tile(p["na_k_norm"], 2)[None, :],
        "gcq": p["mla_cq_norm"][None, :],
        "gckv": p["mla_ckv_norm"][None, :],
        "wq": wq, "wk": wk, "wvt": wvt, "gkf": gkf,
        "w_o_na": p["w_o_na"].astype(bf16),
        "w_o_mla": p["w_o_mla"].astype(bf16),
        "w_out": p["w_out"].astype(bf16),
    }
    return lp, tabs


@jax.jit
def _forward(x, p):
    bsz, seq_len, _ = x.shape
    depth = p["w_in"].shape[0]
    cos, sin = _rope_tables(seq_len)
    onehot, col_mask, tbl_idx = _na_table_static(seq_len)
    x2d = x.reshape(bsz * seq_len, D_MODEL)
    lp, tabs = jax.vmap(lambda pl_: _layer_params(pl_, cos, sin))(p)
    tbl = jax.vmap(lambda rb: _na_table(rb, onehot, col_mask, tbl_idx))(p["na_rel_bias"])
    for layer in range(depth):
        qa, ka, va, nag, qf, kf, mg, gna, gmla, vbt = _proj_call(x2d, lp, tabs, layer, seq_len)

        def b3(a):
            return a.reshape(bsz, seq_len, a.shape[-1])

        a_na = _natt_call(b3(qa), b3(ka), b3(va), b3(nag), tbl, layer)
        a_mla = _mla_call(b3(qf), b3(kf), vbt, b3(mg))
        x2d = _out_call(x2d, a_na.reshape(-1, NA_WIDTH), a_mla.reshape(-1, MLA_WIDTH), gna, gmla, lp, layer)
    return x2d.reshape(bsz, seq_len, D_MODEL)


def kernel(x, ln_g, w_in, na_q_norm, na_k_norm, na_rel_bias, mla_cq_norm, mla_ckv_norm, w_uq, w_ukv,
           mla_q_norm, mla_k_norm, w_o_na, w_o_mla, w_out):
    params = dict(ln_g=ln_g, w_in=w_in, na_q_norm=na_q_norm, na_k_norm=na_k_norm,
                  na_rel_bias=na_rel_bias, mla_cq_norm=mla_cq_norm, mla_ckv_norm=mla_ckv_norm,
                  w_uq=w_uq, w_ukv=w_ukv, mla_q_norm=mla_q_norm, mla_k_norm=mla_k_norm,
                  w_o_na=w_o_na, w_o_mla=w_o_mla, w_out=w_out)
    return _forward(x, params)
```

```python
import numpy as np
import jax
import jax.numpy as jnp
from jax import lax
from jax.experimental import pallas as pl
from jax.experimental.pallas import tpu as pltpu

D_MODEL = 1024
GRID_W = 64
NA_HEADS = 8
NA_HEAD_DIM = 64
NA_WIDTH = NA_HEADS * NA_HEAD_DIM
NA_WIN_ROWS = 8
NA_WIN_COLS = 16
MLA_HEADS = 8
MLA_NOPE = 64
MLA_ROPE = 32
MLA_QK_DIM = MLA_NOPE + MLA_ROPE
MLA_V = 64
MLA_Q_LORA = 256
MLA_KV_LORA = 128
MLA_WIDTH = MLA_HEADS * MLA_V
ROPE_BASE = 10000.0
EPS = 1e-6

LANES = 128
HEAD_PAD = LANES
MLA_PAD_WIDTH = MLA_HEADS * HEAD_PAD
MLA_VT_ROWS = 128
VMEM_LIMIT_BYTES = 56 * 1024 * 1024

_OFF_NAQ = 0
_OFF_NAK = _OFF_NAQ + NA_WIDTH
_OFF_NAV = _OFF_NAK + NA_WIDTH
_OFF_NAG = _OFF_NAV + NA_WIDTH
_OFF_CQ = _OFF_NAG + NA_WIDTH
_OFF_CKV = _OFF_CQ + MLA_Q_LORA
_OFF_MG = _OFF_CKV + 2 * LANES
_OFF_GNA = _OFF_MG + MLA_WIDTH
_OFF_GMLA = _OFF_GNA + D_MODEL
_W_IN_COLS = _OFF_GMLA + D_MODEL

PROJ_TILE = 512
OUT_TILE = 1024
PROJ_BLK = 256
NA_QROWS = 4
NA_BAND = 12
NA_TQ = NA_QROWS * GRID_W
NA_TK = NA_BAND * GRID_W
NA_SUB = 4
MLA_TQ = 256
MLA_KC = 512
MLA_GROUP = 8
MASK_VALUE = -1e30
LOG2E = 1.4426950408889634

_NT_DIMS = (((1,), (1,)), ((), ()))


def _dot(a, b):
    return jnp.dot(a, b, preferred_element_type=jnp.float32)


def _dot_nt(a, b):
    return lax.dot_general(a, b, _NT_DIMS, preferred_element_type=jnp.float32)


def _lane_iota():
    return lax.broadcasted_iota(jnp.int32, (1, LANES), 1)


def _mla_ones_row(parity):
    return MLA_V if parity == 0 else 0


def _proj_kernel(x_ref, lng_ref, w_ref, gq_ref, gk_ref, gcq_ref, gckv_ref, wq_ref,
                 wk_ref, wvt_ref, gkf_ref, tq1_ref, tq2_ref, e1_ref, e2_ref,
                 qa_ref, ka_ref, va_ref, nag_ref, qf_ref, kf_ref, mg_ref,
                 gna_ref, gmla_ref, vbt_ref, hb_scr, cqn_scr, lhsk_scr, sspe_scr):
    bf16 = jnp.bfloat16
    f32 = jnp.float32
    half_rows = hb_scr.shape[0] // 2
    x = x_ref[...]
    hb_scr[...] = (x * lax.rsqrt(jnp.mean(x * x, axis=-1, keepdims=True) + EPS) * lng_ref[...]).astype(bf16)
    lo_half = _lane_iota() < NA_HEAD_DIM
    rope_lanes = _lane_iota() < MLA_ROPE
    row_iota = lax.broadcasted_iota(jnp.int32, (MLA_VT_ROWS, 1), 0)

    def tiles(y):
        return [y[:, i * LANES:(i + 1) * LANES] for i in range(y.shape[1] // LANES)]

    def in_dot(off):
        return lambda: _dot(hb_scr[...], w_ref[:, off:off + PROJ_BLK])

    def headnorm64(g_ref):
        def fn(y):
            out = []
            for pc in tiles(y):
                sq = pc * pc
                s_lo = jnp.sum(jnp.where(lo_half, sq, 0.0), axis=-1, keepdims=True)
                s_hi = jnp.sum(jnp.where(lo_half, 0.0, sq), axis=-1, keepdims=True)
                r = jnp.where(lo_half,
                              lax.rsqrt(s_lo * (1.0 / NA_HEAD_DIM) + EPS),
                              lax.rsqrt(s_hi * (1.0 / NA_HEAD_DIM) + EPS))
                out.append(pc * r * g_ref[...])
            return jnp.concatenate(out, axis=-1)
        return fn

    def silu(g):
        return g * jax.nn.sigmoid(g)

    def post_cq(y):
        cqn_scr[...] = (y * lax.rsqrt(jnp.mean(y * y, axis=-1, keepdims=True) + EPS) * gcq_ref[...]).astype(bf16)

    def post_ckv(y):
        ckv, e = tiles(y)
        lhsk_scr[:, :LANES] = (ckv * lax.rsqrt(jnp.mean(ckv * ckv, axis=-1, keepdims=True) + EPS)
                               * gckv_ref[...]).astype(bf16)
        ss_pe = jnp.sum(jnp.where(rope_lanes, e * e, 0.0), axis=-1, keepdims=True)
        sspe_scr[...] = jnp.broadcast_to(ss_pe, sspe_scr.shape)
        rot = e * e1_ref[...] + pltpu.roll(e, LANES - MLA_ROPE, axis=1) * e2_ref[...]
        rot_hi = rot.astype(bf16).astype(f32)
        rot_lo = (rot - rot_hi).astype(bf16).astype(f32)
        lhsk_scr[:, LANES:] = (rot_hi + pltpu.roll(rot_lo, MLA_ROPE, axis=1)).astype(bf16)

    def q_dot(pair):
        cols = slice(pair * PROJ_BLK, (pair + 1) * PROJ_BLK)
        return lambda: _dot(cqn_scr[...], wq_ref[:, cols])

    def q_post(pair):
        first_rope_half = _lane_iota() < MLA_NOPE + MLA_ROPE // 2

        def post(y):
            for i, a in enumerate(tiles(y)):
                b = jnp.where(first_rope_half, pltpu.roll(a, LANES - MLA_ROPE // 2, axis=1),
                              pltpu.roll(a, MLA_ROPE // 2, axis=1))
                r = lax.rsqrt(jnp.sum(a * a, axis=-1, keepdims=True) / MLA_QK_DIM + EPS)
                col0 = pair * PROJ_BLK + i * HEAD_PAD
                qf_ref[:, col0:col0 + HEAD_PAD] = ((a * tq1_ref[...] + b * tq2_ref[...]) * r).astype(bf16)
        return post

    def k_dot(pair):
        cols = slice(pair * PROJ_BLK, (pair + 1) * PROJ_BLK)
        return lambda: _dot(lhsk_scr[...], wk_ref[:, cols])

    def k_post(pair):
        def post(y):
            for i, a in enumerate(tiles(y)):
                ss = jnp.sum(jnp.where(lo_half, a * a, 0.0), axis=-1, keepdims=True) + sspe_scr[...]
                r = lax.rsqrt(ss / MLA_QK_DIM + EPS)
                col0 = pair * PROJ_BLK + i * HEAD_PAD
                kf_ref[:, col0:col0 + HEAD_PAD] = (a * gkf_ref[...] * r).astype(bf16)
        return post

    def v_dot(pair):
        rows = slice(pair * 2 * MLA_VT_ROWS, (pair + 1) * 2 * MLA_VT_ROWS)
        return lambda: _dot_nt(wvt_ref[rows, :], lhsk_scr[:, :LANES])

    def v_post(pair):
        def post(y):
            for i in range(2):
                row0 = (2 * pair + i) * MLA_VT_ROWS
                a = y[i * MLA_VT_ROWS:(i + 1) * MLA_VT_ROWS, :]
                vbt_ref[0, row0:row0 + MLA_VT_ROWS, :] = jnp.where(row_iota == _mla_ones_row(i), 1.0, a).astype(bf16)
        return post

    def wide_blocks(off, width, out_ref, fn):
        tasks = []
        for r0 in (0, half_rows):
            for c in range(0, width, 2 * PROJ_BLK):
                def dot_fn(r0=r0, c=c):
                    return _dot(hb_scr[r0:r0 + half_rows, :], w_ref[:, off + c:off + c + 2 * PROJ_BLK])

                def post(y, r0=r0, c=c):
                    out_ref[r0:r0 + half_rows, c:c + 2 * PROJ_BLK] = fn(y).astype(bf16)
                tasks.append((dot_fn, post))
        return tasks

    tasks = [(in_dot(_OFF_CQ), post_cq), (in_dot(_OFF_CKV), post_ckv)]
    tasks += wide_blocks(_OFF_NAQ, NA_WIDTH, qa_ref, headnorm64(gq_ref))
    tasks += wide_blocks(_OFF_NAK, NA_WIDTH, ka_ref, headnorm64(gk_ref))
    tasks += [(q_dot(p), q_post(p)) for p in range(MLA_PAD_WIDTH // PROJ_BLK)]
    tasks += wide_blocks(_OFF_GNA, D_MODEL, gna_ref, jax.nn.sigmoid)
    tasks += [(k_dot(p), k_post(p)) for p in range(MLA_PAD_WIDTH // PROJ_BLK)]
    tasks += wide_blocks(_OFF_NAG, NA_WIDTH, nag_ref, silu)
    tasks += [(v_dot(p), v_post(p)) for p in range(MLA_PAD_WIDTH // PROJ_BLK)]
    tasks += wide_blocks(_OFF_MG, MLA_WIDTH, mg_ref, silu)
    tasks += wide_blocks(_OFF_GMLA, D_MODEL, gmla_ref, jax.nn.sigmoid)
    tasks += wide_blocks(_OFF_NAV, NA_WIDTH, va_ref, lambda y: y)

    _run_tasks([(dot_fn, post_fn, False) for dot_fn, post_fn in tasks])


def _const_spec(stacked, layer):
    rest = stacked.shape[1:]
    return pl.BlockSpec((None,) + rest, lambda i: (layer,) + (0,) * len(rest), pipeline_mode=pl.Buffered(1))


def _proj_call(x2d, lp, tabs, layer, seq_len):
    n_tok = x2d.shape[0]
    tm = PROJ_TILE
    assert n_tok % tm == 0 and seq_len % tm == 0
    tiles_per_seq = seq_len // tm
    bf16 = jnp.bfloat16

    def row_spec(width):
        return pl.BlockSpec((tm, width), lambda i: (i, 0))

    def tab_spec():
        return pl.BlockSpec((None, tm, LANES), lambda i: (layer, i % tiles_per_seq, 0))

    consts = [lp["ln_g"], lp["w_in"], lp["gq2"], lp["gk2"], lp["gcq"], lp["gckv"],
              lp["wq"], lp["wk"], lp["wvt"], lp["gkf"]]
    out_widths = [NA_WIDTH, NA_WIDTH, NA_WIDTH, NA_WIDTH, MLA_PAD_WIDTH, MLA_PAD_WIDTH,
                  MLA_WIDTH, D_MODEL, D_MODEL]
    t_widths = [MLA_HEADS * MLA_VT_ROWS]
    t_specs = [pl.BlockSpec((1, w, tm), lambda i: (i // tiles_per_seq, 0, i % tiles_per_seq)) for w in t_widths]
    t_shapes = [jax.ShapeDtypeStruct((n_tok // seq_len, w, seq_len), bf16) for w in t_widths]
    return pl.pallas_call(
        _proj_kernel,
        grid=(n_tok // tm,),
        in_specs=[row_spec(D_MODEL)] + [_const_spec(c, layer) for c in consts] + [tab_spec()] * 4,
        out_specs=[row_spec(w) for w in out_widths] + t_specs,
        out_shape=[jax.ShapeDtypeStruct((n_tok, w), bf16) for w in out_widths] + t_shapes,
        scratch_shapes=[pltpu.VMEM((tm, D_MODEL), bf16),
                        pltpu.VMEM((tm, MLA_Q_LORA), bf16),
                        pltpu.VMEM((tm, 2 * LANES), bf16),
                        pltpu.VMEM((tm, LANES), jnp.float32)],
        compiler_params=pltpu.CompilerParams(
            dimension_semantics=("arbitrary",), vmem_limit_bytes=VMEM_LIMIT_BYTES),
        name="proj",
    )(x2d, *consts, tabs["tq1"], tabs["tq2"], tabs["e1"], tabs["e2"])


def _na_band_start(i, rows):
    return jnp.clip(NA_QROWS * i - NA_WIN_ROWS // 2, 0, rows - NA_BAND)


def _natt_kernel(q_ref, k_ref, v_ref, gate_ref, tbl_ref, o_ref, s_scr, p_scr):
    bf16 = jnp.bfloat16
    i = pl.program_id(1)
    rows = k_ref.shape[1] // GRID_W
    n_blk = pl.num_programs(1) * NA_SUB
    subs = range(NA_SUB)
    blks = [NA_SUB * i + sb for sb in subs]
    starts = [pl.multiple_of(_na_band_start(blk, rows) * GRID_W, GRID_W) for blk in blks]
    pats = [jnp.minimum(blk, 1) + (blk == n_blk - 1).astype(jnp.int32) for blk in blks]
    lo_half = _lane_iota() < NA_HEAD_DIM

    def pcols(head):
        return slice((head // 2) * LANES, (head // 2 + 1) * LANES)

    def qrows(sb):
        return slice(sb * NA_TQ, (sb + 1) * NA_TQ)

    def fold(x, op):
        run = None
        for t in range(x.shape[1] // LANES):
            tile = x[:, t * LANES:(t + 1) * LANES]
            run = tile if run is None else op(run, tile)
        return run

    def scores(head, sb):
        qp = q_ref[0, qrows(sb), pcols(head)]
        sel = lo_half if head % 2 == 0 else jnp.logical_not(lo_half)
        qm = jnp.where(sel, qp, jnp.zeros_like(qp))
        keys = k_ref[0, pl.ds(starts[sb], NA_TK), pcols(head)]
        s = _dot_nt(qm, keys) + tbl_ref[pats[sb], head]
        s_scr[sb, head % 2] = s
        return jnp.max(fold(s, jnp.maximum), axis=-1, keepdims=True)

    def probs(head, sb, m):
        p = jnp.exp2(s_scr[sb, head % 2] - m)
        p_scr[sb, head % 2] = p.astype(bf16)
        return fold(p, jnp.add)

    def pv(head, sb):
        return _dot(p_scr[sb, head % 2], v_ref[0, pl.ds(starts[sb], NA_TK), pcols(head)])

    def finish(head, sb, acc, run_sum, prev):
        o = acc / jnp.sum(run_sum, axis=-1, keepdims=True)
        if head % 2 == 0:
            return o
        pair = jnp.where(lo_half, prev, o)
        gate = gate_ref[0, qrows(sb), pcols(head)].astype(jnp.float32)
        o_ref[0, qrows(sb), pcols(head)] = (pair * gate).astype(bf16)
        return None

    m = [scores(0, sb) for sb in subs]
    prev_sum, even_out = [None] * NA_SUB, [None] * NA_SUB
    for head in range(NA_HEADS):
        m_next = [scores(head + 1, sb) for sb in subs] if head + 1 < NA_HEADS else None
        acc = [pv(head - 1, sb) for sb in subs] if head >= 1 else None
        cur_sum = [probs(head, sb, m[sb]) for sb in subs]
        if head >= 1:
            even_out = [finish(head - 1, sb, acc[sb], prev_sum[sb], even_out[sb]) for sb in subs]
        prev_sum, m = cur_sum, m_next
    for sb in subs:
        finish(NA_HEADS - 1, sb, pv(NA_HEADS - 1, sb), prev_sum[sb], even_out[sb])


def _natt_call(qa, ka, va, nag, tbl, layer):
    bsz, seq_len, _ = qa.shape
    n_blk = seq_len // NA_TQ
    assert n_blk % NA_SUB == 0
    blk = pl.BlockSpec((1, NA_SUB * NA_TQ, NA_WIDTH), lambda b, i: (b, i, 0))
    full = pl.BlockSpec((1, seq_len, NA_WIDTH), lambda b, i: (b, 0, 0))
    return pl.pallas_call(
        _natt_kernel,
        grid=(bsz, n_blk // NA_SUB),
        in_specs=[blk, full, full, blk,
                  pl.BlockSpec((None,) + tbl.shape[1:], lambda b, i: (layer, 0, 0, 0, 0),
                               pipeline_mode=pl.Buffered(1))],
        out_specs=blk,
        out_shape=jax.ShapeDtypeStruct((bsz, seq_len, NA_WIDTH), jnp.bfloat16),
        scratch_shapes=[pltpu.VMEM((NA_SUB, 2, NA_TQ, NA_TK), jnp.float32),
                        pltpu.VMEM((NA_SUB, 2, NA_TQ, NA_TK), jnp.bfloat16)],
        compiler_params=pltpu.CompilerParams(
            dimension_semantics=("arbitrary", "arbitrary"), vmem_limit_bytes=VMEM_LIMIT_BYTES),
        name="natt",
    )(qa, ka, va, nag, tbl)


def _mla_kernel(q_ref, k_ref, vt_ref, gate_ref, o_ref, s_scr, p_scr):
    bf16 = jnp.bfloat16
    seq_len = k_ref.shape[1]
    n_chunks = seq_len // MLA_KC
    n_heads = q_ref.shape[2] // HEAD_PAD
    lo_half = _lane_iota() < MLA_V
    sublanes = 8

    def hcols(head):
        return slice(head * HEAD_PAD, (head + 1) * HEAD_PAD)

    def kslice(c):
        return slice(c * MLA_KC, (c + 1) * MLA_KC)

    def qk_chunk(head, c, run_max):
        s = _dot_nt(k_ref[0, kslice(c), hcols(head)], q_ref[0, :, hcols(head)])
        s_scr[head % 2, kslice(c), :] = s
        part = jnp.max(s.reshape(MLA_KC // sublanes, sublanes, MLA_TQ), axis=0)
        return part if run_max is None else jnp.maximum(run_max, part)

    def exp_chunk(head, c, m):
        p_scr[head % 2, kslice(c), :] = jnp.exp2(s_scr[head % 2, kslice(c), :] - m).astype(bf16)

    def pv_chunk(head, c, acc):
        vrows = slice(head * MLA_VT_ROWS, (head + 1) * MLA_VT_ROWS)
        o = _dot(vt_ref[0, vrows, kslice(c)], p_scr[head % 2, kslice(c), :])
        return o if acc is None else acc + o

    def finish(head, acc, prev):
        ones_row = _mla_ones_row(head % 2)
        o = (acc / acc[ones_row:ones_row + 1, :]).T
        if head % 2 == 0:
            return o
        cols = slice((head // 2) * LANES, (head // 2 + 1) * LANES)
        pair = jnp.where(lo_half, prev, o)
        o_ref[0, :, cols] = (pair * gate_ref[0, :, cols].astype(jnp.float32)).astype(bf16)
        return None

    run = None
    for c in range(n_chunks):
        run = qk_chunk(0, c, run)
    m = jnp.max(run, axis=0, keepdims=True)
    acc, even_out = None, None
    for head in range(n_heads):
        run, next_acc = None, None
        for c in range(n_chunks):
            if head + 1 < n_heads:
                run = qk_chunk(head + 1, c, run)
            if head >= 1:
                next_acc = pv_chunk(head - 1, c, next_acc)
            exp_chunk(head, c, m)
        if head >= 1:
            even_out = finish(head - 1, next_acc, even_out)
        if head + 1 < n_heads:
            m = jnp.max(run, axis=0, keepdims=True)
    for c in range(n_chunks):
        acc = pv_chunk(n_heads - 1, c, acc)
    finish(n_heads - 1, acc, even_out)


def _mla_call(qf, kf, vbt, mg):
    bsz, seq_len, _ = qf.shape
    n_groups = MLA_HEADS // MLA_GROUP
    assert seq_len % MLA_KC == 0 and seq_len % MLA_TQ == 0 and MLA_GROUP % 2 == 0
    return pl.pallas_call(
        _mla_kernel,
        grid=(bsz, n_groups, seq_len // MLA_TQ),
        in_specs=[pl.BlockSpec((1, MLA_TQ, MLA_GROUP * HEAD_PAD), lambda b, p, j: (b, j, p)),
                  pl.BlockSpec((1, seq_len, MLA_GROUP * HEAD_PAD), lambda b, p, j: (b, 0, p)),
                  pl.BlockSpec((1, MLA_GROUP * MLA_VT_ROWS, seq_len), lambda b, p, j: (b, p, 0)),
                  pl.BlockSpec((1, MLA_TQ, MLA_GROUP * MLA_V), lambda b, p, j: (b, j, p))],
        out_specs=pl.BlockSpec((1, MLA_TQ, MLA_GROUP * MLA_V), lambda b, p, j: (b, j, p)),
        out_shape=jax.ShapeDtypeStruct((bsz, seq_len, MLA_WIDTH), jnp.bfloat16),
        scratch_shapes=[pltpu.VMEM((2, seq_len, MLA_TQ), jnp.float32),
                        pltpu.VMEM((2, seq_len, MLA_TQ), jnp.bfloat16)],
        compiler_params=pltpu.CompilerParams(
            dimension_semantics=("arbitrary", "arbitrary", "arbitrary"),
            vmem_limit_bytes=VMEM_LIMIT_BYTES),
        name="mla",
    )(qf, kf, vbt, mg)


def _run_tasks(tasks):
    pending = None
    for dot_fn, post_fn, barrier in tasks:
        if barrier and pending is not None:
            pending[0](pending[1])
            pending = None
        y = dot_fn()
        if pending is not None:
            pending[0](pending[1])
        pending = (post_fn, y)
    pending[0](pending[1])


def _out_kernel(x_ref, ana_ref, amla_ref, gna_ref, gmla_ref, wna_ref, wmla_ref, wout_ref, o_ref, y_scr):
    f32 = jnp.float32
    row_tiles = [slice(r, r + PROJ_TILE) for r in range(0, o_ref.shape[0], PROJ_TILE)]
    col_blocks = [slice(c, c + PROJ_BLK) for c in range(0, o_ref.shape[1], PROJ_BLK)]

    def branch_dots(rows, cols):
        return lambda: (_dot(ana_ref[rows, :], wna_ref[:, cols]), _dot(amla_ref[rows, :], wmla_ref[:, cols]))

    def merge(rows, cols):
        def post(us):
            y = gna_ref[rows, cols].astype(f32) * us[0] + gmla_ref[rows, cols].astype(f32) * us[1]
            y_scr[rows, cols] = y.astype(jnp.bfloat16)
        return post

    def out_dot(rows, cols):
        return lambda: _dot(y_scr[rows, :], wout_ref[:, cols])

    def residual(rows, cols):
        def post(y):
            o_ref[rows, cols] = x_ref[rows, cols] + y
        return post

    tasks = [(branch_dots(r, c), merge(r, c), False) for r in row_tiles for c in col_blocks]
    tasks += [(out_dot(r, c), residual(r, c), i == 0)
              for i, (r, c) in enumerate((r, c) for r in row_tiles for c in col_blocks)]
    _run_tasks(tasks)


def _out_call(x2d, a_na, a_mla, gna, gmla, lp, layer):
    n_tok = x2d.shape[0]
    tm = OUT_TILE
    assert n_tok % tm == 0 and tm % PROJ_TILE == 0

    def row_spec(width):
        return pl.BlockSpec((tm, width), lambda i: (i, 0))

    consts = [lp["w_o_na"], lp["w_o_mla"], lp["w_out"]]
    return pl.pallas_call(
        _out_kernel,
        grid=(n_tok // tm,),
        in_specs=[row_spec(D_MODEL), row_spec(NA_WIDTH), row_spec(MLA_WIDTH), row_spec(D_MODEL),
                  row_spec(D_MODEL)] + [_const_spec(c, layer) for c in consts],
        out_specs=row_spec(D_MODEL),
        out_shape=jax.ShapeDtypeStruct((n_tok, D_MODEL), jnp.float32),
        scratch_shapes=[pltpu.VMEM((tm, D_MODEL), jnp.bfloat16)],
        compiler_params=pltpu.CompilerParams(
            dimension_semantics=("arbitrary",), vmem_limit_bytes=VMEM_LIMIT_BYTES),
        name="out",
    )(x2d, a_na, a_mla, gna, gmla, *consts)


_N_DROW = 2 * NA_WIN_ROWS - 1
_N_DCOL = 2 * NA_WIN_COLS - 1


def _na_table_static(seq_len):
    rows = seq_len // GRID_W
    n_blk = rows // NA_QROWS
    qc = np.arange(GRID_W)
    kj = np.arange(GRID_W)
    cs = np.clip(qc - NA_WIN_COLS // 2, 0, GRID_W - NA_WIN_COLS)
    col_valid = (kj[None, :] >= cs[:, None]) & (kj[None, :] < cs[:, None] + NA_WIN_COLS)
    dcol = kj[None, :] - qc[:, None] + (NA_WIN_COLS - 1)
    onehot = (dcol[None] == np.arange(_N_DCOL)[:, None, None]) & col_valid[None]
    col_mask = np.where(col_valid, 0.0, MASK_VALUE).astype(np.float32)
    qr = np.arange(NA_QROWS)
    kn = np.arange(NA_BAND)
    idx = []
    for i in (0, 1, n_blk - 1):
        band = int(np.clip(NA_QROWS * i - NA_WIN_ROWS // 2, 0, rows - NA_BAND))
        r = NA_QROWS * i + qr
        rs = np.clip(r - NA_WIN_ROWS // 2, 0, rows - NA_WIN_ROWS)
        krow = band + kn
        valid = (krow[None, :] >= rs[:, None]) & (krow[None, :] < rs[:, None] + NA_WIN_ROWS)
        drow = krow[None, :] - r[:, None] + (NA_WIN_ROWS - 1)
        idx.append(np.where(valid, drow, _N_DROW))
    return onehot.astype(np.float32), col_mask, np.stack(idx).astype(np.int32)


def _na_table(rel_bias, onehot, col_mask, idx):
    blocks = jnp.einsum("hdc,cqj->hdqj", rel_bias, onehot, precision=lax.Precision.HIGHEST) + col_mask
    blocks = jnp.concatenate(
        [blocks, jnp.full((NA_HEADS, 1, GRID_W, GRID_W), MASK_VALUE, jnp.float32)], axis=1) * LOG2E
    tbl = jnp.take(blocks, idx.reshape(-1), axis=1)
    tbl = tbl.reshape(NA_HEADS, 3, NA_QROWS, NA_BAND, GRID_W, GRID_W)
    return tbl.transpose(1, 0, 2, 4, 3, 5).reshape(3, NA_HEADS, NA_TQ, NA_TK)


def _rope_tables(seq_len):
    t = jnp.arange(seq_len)
    row = (t // GRID_W).astype(jnp.float32)
    col = (t % GRID_W).astype(jnp.float32)
    half = MLA_ROPE // 2
    n_freq = half // 2
    inv = jnp.power(jnp.float32(ROPE_BASE), -jnp.arange(n_freq, dtype=jnp.float32) / n_freq)
    ang = jnp.concatenate([row[:, None] * inv, col[:, None] * inv], axis=-1)
    return jnp.cos(ang), jnp.sin(ang)


def _place(cols_to_blocks, total):
    parts = []
    for arr, width in cols_to_blocks:
        pad = width - arr.shape[-1]
        parts.append(jnp.pad(arr, [(0, 0)] * (arr.ndim - 1) + [(0, pad)]) if pad else arr)
    out = jnp.concatenate(parts, axis=-1)
    assert out.shape[-1] == total
    return out


def _layer_params(p, cos, sin):
    f32, bf16 = jnp.float32, jnp.bfloat16
    half = MLA_ROPE // 2
    w_in = p["w_in"]
    o = 0
    pieces = {}
    for name, size in (("naq", NA_WIDTH), ("nak", NA_WIDTH), ("nav", NA_WIDTH), ("nag", NA_WIDTH),
                       ("cq", MLA_Q_LORA), ("ckv", MLA_KV_LORA), ("kpe", MLA_ROPE),
                       ("mg", MLA_WIDTH), ("gna", D_MODEL), ("gmla", D_MODEL)):
        pieces[name] = w_in[:, o:o + size]
        o += size
    kpe = pieces["kpe"]
    kpe_sw = jnp.concatenate([kpe[:, half:], kpe[:, :half]], axis=-1)
    ckvx = _place([(pieces["ckv"], LANES), (jnp.concatenate([kpe, kpe_sw], axis=-1), LANES)], 2 * LANES)
    w_in_packed = jnp.concatenate(
        [pieces["naq"], pieces["nak"], pieces["nav"], pieces["nag"], pieces["cq"], ckvx,
         pieces["mg"], pieces["gna"], pieces["gmla"]], axis=-1).astype(bf16)
    assert w_in_packed.shape[-1] == _W_IN_COLS

    w_uq = p["w_uq"].reshape(MLA_Q_LORA, MLA_HEADS, MLA_QK_DIM)
    wq = _place([(w_uq, HEAD_PAD)], HEAD_PAD).reshape(MLA_Q_LORA, MLA_PAD_WIDTH).astype(bf16)

    w_ukv = p["w_ukv"].reshape(MLA_KV_LORA, MLA_HEADS, MLA_NOPE + MLA_V)
    k_nope = _place([(w_ukv[..., :MLA_NOPE], HEAD_PAD)], HEAD_PAD).reshape(MLA_KV_LORA, MLA_PAD_WIDTH)
    eye = jnp.eye(MLA_ROPE, dtype=f32)
    put = jnp.pad(eye, ((0, 0), (MLA_NOPE, HEAD_PAD - MLA_QK_DIM)))
    put = jnp.tile(put, (1, MLA_HEADS))
    wk = jnp.concatenate([k_nope, put, put, jnp.zeros((2 * LANES - MLA_KV_LORA - 2 * MLA_ROPE,
                                                        MLA_PAD_WIDTH), f32)], axis=0).astype(bf16)
    w_uv = w_ukv[..., MLA_NOPE:].reshape(MLA_KV_LORA, MLA_HEADS // 2, 2, MLA_V)
    zv = jnp.zeros_like(w_uv[:, :, 0])
    wvt = jnp.stack([jnp.concatenate([w_uv[:, :, 0], zv], -1), jnp.concatenate([zv, w_uv[:, :, 1]], -1)], axis=2)
    wvt = wvt.reshape(MLA_KV_LORA, MLA_HEADS * MLA_VT_ROWS).T.astype(bf16)

    gq = p["mla_q_norm"]
    gk = p["mla_k_norm"]
    scale = MLA_QK_DIM ** -0.5 * LOG2E
    seq_len = cos.shape[0]
    ones = jnp.ones((seq_len, 1), f32)
    tq1 = _place([(ones * gq[None, :MLA_NOPE], MLA_NOPE),
                  (cos * gq[None, MLA_NOPE:MLA_NOPE + half], half),
                  (cos * gq[None, MLA_NOPE + half:], half)], MLA_QK_DIM) * scale
    tq2 = _place([(jnp.zeros((seq_len, MLA_NOPE), f32), MLA_NOPE),
                  (-sin * gq[None, MLA_NOPE + half:], half),
                  (sin * gq[None, MLA_NOPE:MLA_NOPE + half], half)], MLA_QK_DIM) * scale
    e1 = jnp.concatenate([cos * gk[None, MLA_NOPE:MLA_NOPE + half], cos * gk[None, MLA_NOPE + half:]], -1)
    e2 = jnp.concatenate([-sin * gk[None, MLA_NOPE + half:], sin * gk[None, MLA_NOPE:MLA_NOPE + half]], -1)
    tabs = {"tq1": _place([(tq1, LANES)], LANES), "tq2": _place([(tq2, LANES)], LANES),
            "e1": _place([(e1, LANES)], LANES), "e2": _place([(e2, LANES)], LANES)}
    gkf = _place([(gk[None, :MLA_NOPE], MLA_NOPE), (jnp.ones((1, MLA_ROPE), f32), LANES - MLA_NOPE)], LANES)

    na_scale = NA_HEAD_DIM ** -0.5 * LOG2E
    lp = {
        "ln_g": p["ln_g"][None, :],
        "w_in": w_in_packed,
        "gq2": jnp.tile(p["na_q_norm"], 2)[None, :] * na_scale,
        "gk2": jnp.tile(p["na_k_norm"], 2)[None, :],
        "gcq": p["mla_cq_norm"][None, :],
        "gckv": p["mla_ckv_norm"][None, :],
        "wq": wq, "wk": wk, "wvt": wvt, "gkf": gkf,
        "w_o_na": p["w_o_na"].astype(bf16),
        "w_o_mla": p["w_o_mla"].astype(bf16),
        "w_out": p["w_out"].astype(bf16),
    }
    return lp, tabs


@jax.jit
def _forward(x, p):
    bsz, seq_len, _ = x.shape
    depth = p["w_in"].shape[0]
    cos, sin = _rope_tables(seq_len)
    onehot, col_mask, tbl_idx = _na_table_static(seq_len)
    x2d = x.reshape(bsz * seq_len, D_MODEL)
    lp, tabs = jax.vmap(lambda pl_: _layer_params(pl_, cos, sin))(p)
    tbl = jax.vmap(lambda rb: _na_table(rb, onehot, col_mask, tbl_idx))(p["na_rel_bias"])
    for layer in range(depth):
        qa, ka, va, nag, qf, kf, mg, gna, gmla, vbt = _proj_call(x2d, lp, tabs, layer, seq_len)

        def b3(a):
            return a.reshape(bsz, seq_len, a.shape[-1])

        a_na = _natt_call(b3(qa), b3(ka), b3(va), b3(nag), tbl, layer)
        a_mla = _mla_call(b3(qf), b3(kf), vbt, b3(mg))
        x2d = _out_call(x2d, a_na.reshape(-1, NA_WIDTH), a_mla.reshape(-1, MLA_WIDTH), gna, gmla, lp, layer)
    return x2d.reshape(bsz, seq_len, D_MODEL)


def kernel(x, ln_g, w_in, na_q_norm, na_k_norm, na_rel_bias, mla_cq_norm, mla_ckv_norm, w_uq, w_ukv,
           mla_q_norm, mla_k_norm, w_o_na, w_o_mla, w_out):
    params = dict(ln_g=ln_g, w_in=w_in, na_q_norm=na_q_norm, na_k_norm=na_k_norm,
                  na_rel_bias=na_rel_bias, mla_cq_norm=mla_cq_norm, mla_ckv_norm=mla_ckv_norm,
                  w_uq=w_uq, w_ukv=w_ukv, mla_q_norm=mla_q_norm, mla_k_norm=mla_k_norm,
                  w_o_na=w_o_na, w_o_mla=w_o_mla, w_out=w_out)
    return _forward(x, params)
```

```python
import numpy as np
import jax
import jax.numpy as jnp
from jax import lax
from jax.experimental import pallas as pl
from jax.experimental.pallas import tpu as pltpu

D_MODEL = 1024
GRID_W = 64
NA_HEADS = 8
NA_HEAD_DIM = 64
NA_WIDTH = NA_HEADS * NA_HEAD_DIM
NA_WIN_ROWS = 8
NA_WIN_COLS = 16
MLA_HEADS = 8
MLA_NOPE = 64
MLA_ROPE = 32
MLA_QK_DIM = MLA_NOPE + MLA_ROPE
MLA_V = 64
MLA_Q_LORA = 256
MLA_KV_LORA = 128
MLA_WIDTH = MLA_HEADS * MLA_V
ROPE_BASE = 10000.0
EPS = 1e-6

LANES = 128
HEAD_PAD = LANES
MLA_PAD_WIDTH = MLA_HEADS * HEAD_PAD
MLA_VT_ROWS = 128
VMEM_LIMIT_BYTES = 56 * 1024 * 1024

_OFF_NAQ = 0
_OFF_NAK = _OFF_NAQ + NA_WIDTH
_OFF_NAV = _OFF_NAK + NA_WIDTH
_OFF_NAG = _OFF_NAV + NA_WIDTH
_OFF_CQ = _OFF_NAG + NA_WIDTH
_OFF_CKV = _OFF_CQ + MLA_Q_LORA
_OFF_MG = _OFF_CKV + 2 * LANES
_OFF_GNA = _OFF_MG + MLA_WIDTH
_OFF_GMLA = _OFF_GNA + D_MODEL
_W_IN_COLS = _OFF_GMLA + D_MODEL

PROJ_TILE = 512
OUT_TILE = 1024
PROJ_BLK = 256
NA_QROWS = 4
NA_BAND = 12
NA_TQ = NA_QROWS * GRID_W
NA_TK = NA_BAND * GRID_W
NA_SUB = 4
MLA_TQ = 256
MLA_KC = 256
MLA_GROUP = 8
MASK_VALUE = -1e30
LOG2E = 1.4426950408889634

_NT_DIMS = (((1,), (1,)), ((), ()))


def _dot(a, b):
    return jnp.dot(a, b, preferred_element_type=jnp.float32)


def _dot_nt(a, b):
    return lax.dot_general(a, b, _NT_DIMS, preferred_element_type=jnp.float32)


def _lane_iota():
    return lax.broadcasted_iota(jnp.int32, (1, LANES), 1)


def _mla_ones_row(parity):
    return MLA_V if parity == 0 else 0


def _proj_kernel(x_ref, lng_ref, w_ref, gq_ref, gk_ref, gcq_ref, gckv_ref, wq_ref,
                 wk_ref, wvt_ref, gkf_ref, tq1_ref, tq2_ref, e1_ref, e2_ref,
                 qa_ref, ka_ref, va_ref, nag_ref, qf_ref, kf_ref, mg_ref,
                 gna_ref, gmla_ref, vbt_ref, hb_scr, cqn_scr, lhsk_scr, sspe_scr):
    bf16 = jnp.bfloat16
    f32 = jnp.float32
    half_rows = hb_scr.shape[0] // 2
    x = x_ref[...]
    hb_scr[...] = (x * lax.rsqrt(jnp.mean(x * x, axis=-1, keepdims=True) + EPS) * lng_ref[...]).astype(bf16)
    lo_half = _lane_iota() < NA_HEAD_DIM
    rope_lanes = _lane_iota() < MLA_ROPE
    row_iota = lax.broadcasted_iota(jnp.int32, (MLA_VT_ROWS, 1), 0)

    def tiles(y):
        return [y[:, i * LANES:(i + 1) * LANES] for i in range(y.shape[1] // LANES)]

    def in_dot(off):
        return lambda: _dot(hb_scr[...], w_ref[:, off:off + PROJ_BLK])

    def headnorm64(g_ref):
        def fn(y):
            out = []
            for pc in tiles(y):
                sq = pc * pc
                s_lo = jnp.sum(jnp.where(lo_half, sq, 0.0), axis=-1, keepdims=True)
                s_hi = jnp.sum(jnp.where(lo_half, 0.0, sq), axis=-1, keepdims=True)
                r = jnp.where(lo_half,
                              lax.rsqrt(s_lo * (1.0 / NA_HEAD_DIM) + EPS),
                              lax.rsqrt(s_hi * (1.0 / NA_HEAD_DIM) + EPS))
                out.append(pc * r * g_ref[...])
            return jnp.concatenate(out, axis=-1)
        return fn

    def silu(g):
        return g * jax.nn.sigmoid(g)

    def post_cq(y):
        cqn_scr[...] = (y * lax.rsqrt(jnp.mean(y * y, axis=-1, keepdims=True) + EPS) * gcq_ref[...]).astype(bf16)

    def post_ckv(y):
        ckv, e = tiles(y)
        lhsk_scr[:, :LANES] = (ckv * lax.rsqrt(jnp.mean(ckv * ckv, axis=-1, keepdims=True) + EPS)
                               * gckv_ref[...]).astype(bf16)
        ss_pe = jnp.sum(jnp.where(rope_lanes, e * e, 0.0), axis=-1, keepdims=True)
        sspe_scr[...] = jnp.broadcast_to(ss_pe, sspe_scr.shape)
        rot = e * e1_ref[...] + pltpu.roll(e, LANES - MLA_ROPE, axis=1) * e2_ref[...]
        rot_hi = rot.astype(bf16).astype(f32)
        rot_lo = (rot - rot_hi).astype(bf16).astype(f32)
        lhsk_scr[:, LANES:] = (rot_hi + pltpu.roll(rot_lo, MLA_ROPE, axis=1)).astype(bf16)

    def head_tasks(src_scr, w2_ref, out_ref, head_fn):
        tasks = []
        for r0 in (0, half_rows):
            rows = slice(r0, r0 + half_rows)
            for c in range(0, MLA_PAD_WIDTH, 2 * PROJ_BLK):
                def dot_fn(rows=rows, c=c):
                    return _dot(src_scr[rows, :], w2_ref[:, c:c + 2 * PROJ_BLK])

                def post(y, rows=rows, c=c):
                    for i, a in enumerate(tiles(y)):
                        out_ref[rows, c + i * HEAD_PAD:c + (i + 1) * HEAD_PAD] = head_fn(a, rows).astype(bf16)
                tasks.append((dot_fn, post))
        return tasks

    first_rope_half = _lane_iota() < MLA_NOPE + MLA_ROPE // 2

    def q_head(a, rows):
        b = jnp.where(first_rope_half, pltpu.roll(a, LANES - MLA_ROPE // 2, axis=1),
                      pltpu.roll(a, MLA_ROPE // 2, axis=1))
        r = lax.rsqrt(jnp.sum(a * a, axis=-1, keepdims=True) / MLA_QK_DIM + EPS)
        return (a * tq1_ref[rows, :] + b * tq2_ref[rows, :]) * r

    def k_head(a, rows):
        ss = jnp.sum(jnp.where(lo_half, a * a, 0.0), axis=-1, keepdims=True) + sspe_scr[rows, :]
        return a * gkf_ref[...] * lax.rsqrt(ss / MLA_QK_DIM + EPS)

    def v_dot(pair):
        rows = slice(pair * 2 * MLA_VT_ROWS, (pair + 1) * 2 * MLA_VT_ROWS)
        return lambda: _dot_nt(wvt_ref[rows, :], lhsk_scr[:, :LANES])

    def v_post(pair):
        def post(y):
            for i in range(2):
                row0 = (2 * pair + i) * MLA_VT_ROWS
                a = y[i * MLA_VT_ROWS:(i + 1) * MLA_VT_ROWS, :]
                vbt_ref[0, row0:row0 + MLA_VT_ROWS, :] = jnp.where(row_iota == _mla_ones_row(i), 1.0, a).astype(bf16)
        return post

    def wide_blocks(off, width, out_ref, fn):
        tasks = []
        for r0 in (0, half_rows):
            for c in range(0, width, 2 * PROJ_BLK):
                def dot_fn(r0=r0, c=c):
                    return _dot(hb_scr[r0:r0 + half_rows, :], w_ref[:, off + c:off + c + 2 * PROJ_BLK])

                def post(y, r0=r0, c=c):
                    out_ref[r0:r0 + half_rows, c:c + 2 * PROJ_BLK] = fn(y).astype(bf16)
                tasks.append((dot_fn, post))
        return tasks

    tasks = [(in_dot(_OFF_CQ), post_cq), (in_dot(_OFF_CKV), post_ckv)]
    tasks += wide_blocks(_OFF_NAQ, NA_WIDTH, qa_ref, headnorm64(gq_ref))
    tasks += wide_blocks(_OFF_NAK, NA_WIDTH, ka_ref, headnorm64(gk_ref))
    tasks += head_tasks(cqn_scr, wq_ref, qf_ref, q_head)
    tasks += wide_blocks(_OFF_GNA, D_MODEL, gna_ref, jax.nn.sigmoid)
    tasks += head_tasks(lhsk_scr, wk_ref, kf_ref, k_head)
    tasks += wide_blocks(_OFF_NAG, NA_WIDTH, nag_ref, silu)
    tasks += [(v_dot(p), v_post(p)) for p in range(MLA_PAD_WIDTH // PROJ_BLK)]
    tasks += wide_blocks(_OFF_MG, MLA_WIDTH, mg_ref, silu)
    tasks += wide_blocks(_OFF_GMLA, D_MODEL, gmla_ref, jax.nn.sigmoid)
    tasks += wide_blocks(_OFF_NAV, NA_WIDTH, va_ref, lambda y: y)

    _run_tasks([(dot_fn, post_fn, False) for dot_fn, post_fn in tasks])


def _const_spec(stacked, layer):
    rest = stacked.shape[1:]
    return pl.BlockSpec((None,) + rest, lambda i: (layer,) + (0,) * len(rest), pipeline_mode=pl.Buffered(1))


def _proj_call(x2d, lp, tabs, layer, seq_len):
    n_tok = x2d.shape[0]
    tm = PROJ_TILE
    assert n_tok % tm == 0 and seq_len % tm == 0
    tiles_per_seq = seq_len // tm
    bf16 = jnp.bfloat16

    def row_spec(width):
        return pl.BlockSpec((tm, width), lambda i: (i, 0))

    def tab_spec():
        return pl.BlockSpec((None, tm, LANES), lambda i: (layer, i % tiles_per_seq, 0))

    consts = [lp["ln_g"], lp["w_in"], lp["gq2"], lp["gk2"], lp["gcq"], lp["gckv"],
              lp["wq"], lp["wk"], lp["wvt"], lp["gkf"]]
    out_widths = [NA_WIDTH, NA_WIDTH, NA_WIDTH, NA_WIDTH, MLA_PAD_WIDTH, MLA_PAD_WIDTH,
                  MLA_WIDTH, D_MODEL, D_MODEL]
    t_widths = [MLA_HEADS * MLA_VT_ROWS]
    t_specs = [pl.BlockSpec((1, w, tm), lambda i: (i // tiles_per_seq, 0, i % tiles_per_seq)) for w in t_widths]
    t_shapes = [jax.ShapeDtypeStruct((n_tok // seq_len, w, seq_len), bf16) for w in t_widths]
    return pl.pallas_call(
        _proj_kernel,
        grid=(n_tok // tm,),
        in_specs=[row_spec(D_MODEL)] + [_const_spec(c, layer) for c in consts] + [tab_spec()] * 4,
        out_specs=[row_spec(w) for w in out_widths] + t_specs,
        out_shape=[jax.ShapeDtypeStruct((n_tok, w), bf16) for w in out_widths] + t_shapes,
        scratch_shapes=[pltpu.VMEM((tm, D_MODEL), bf16),
                        pltpu.VMEM((tm, MLA_Q_LORA), bf16),
                        pltpu.VMEM((tm, 2 * LANES), bf16),
                        pltpu.VMEM((tm, LANES), jnp.float32)],
        compiler_params=pltpu.CompilerParams(
            dimension_semantics=("arbitrary",), vmem_limit_bytes=VMEM_LIMIT_BYTES),
        name="proj",
    )(x2d, *consts, tabs["tq1"], tabs["tq2"], tabs["e1"], tabs["e2"])


def _na_band_start(i, rows):
    return jnp.clip(NA_QROWS * i - NA_WIN_ROWS // 2, 0, rows - NA_BAND)


def _natt_kernel(q_ref, k_ref, v_ref, gate_ref, tbl_ref, o_ref, s_scr, p_scr):
    bf16 = jnp.bfloat16
    i = pl.program_id(1)
    rows = k_ref.shape[1] // GRID_W
    n_blk = pl.num_programs(1) * NA_SUB
    subs = range(NA_SUB)
    blks = [NA_SUB * i + sb for sb in subs]
    starts = [pl.multiple_of(_na_band_start(blk, rows) * GRID_W, GRID_W) for blk in blks]
    pats = [jnp.minimum(blk, 1) + (blk == n_blk - 1).astype(jnp.int32) for blk in blks]
    lo_half = _lane_iota() < NA_HEAD_DIM

    def pcols(head):
        return slice((head // 2) * LANES, (head // 2 + 1) * LANES)

    def qrows(sb):
        return slice(sb * NA_TQ, (sb + 1) * NA_TQ)

    def fold(x, op):
        run = None
        for t in range(x.shape[1] // LANES):
            tile = x[:, t * LANES:(t + 1) * LANES]
            run = tile if run is None else op(run, tile)
        return run

    def scores(head, sb):
        qp = q_ref[0, qrows(sb), pcols(head)]
        sel = lo_half if head % 2 == 0 else jnp.logical_not(lo_half)
        qm = jnp.where(sel, qp, jnp.zeros_like(qp))
        keys = k_ref[0, pl.ds(starts[sb], NA_TK), pcols(head)]
        s = _dot_nt(qm, keys) + tbl_ref[pats[sb], head]
        s_scr[sb, head % 2] = s
        return jnp.max(fold(s, jnp.maximum), axis=-1, keepdims=True)

    def probs(head, sb, m):
        p = jnp.exp2(s_scr[sb, head % 2] - m)
        p_scr[sb, head % 2] = p.astype(bf16)
        return fold(p, jnp.add)

    def pv(head, sb):
        return _dot(p_scr[sb, head % 2], v_ref[0, pl.ds(starts[sb], NA_TK), pcols(head)])

    def finish(head, sb, acc, run_sum, prev):
        o = acc / jnp.sum(run_sum, axis=-1, keepdims=True)
        if head % 2 == 0:
            return o
        pair = jnp.where(lo_half, prev, o)
        gate = gate_ref[0, qrows(sb), pcols(head)].astype(jnp.float32)
        o_ref[0, qrows(sb), pcols(head)] = (pair * gate).astype(bf16)
        return None

    m = [scores(0, sb) for sb in subs]
    prev_sum, even_out = [None] * NA_SUB, [None] * NA_SUB
    for head in range(NA_HEADS):
        m_next = [scores(head + 1, sb) for sb in subs] if head + 1 < NA_HEADS else None
        acc = [pv(head - 1, sb) for sb in subs] if head >= 1 else None
        cur_sum = [probs(head, sb, m[sb]) for sb in subs]
        if head >= 1:
            even_out = [finish(head - 1, sb, acc[sb], prev_sum[sb], even_out[sb]) for sb in subs]
        prev_sum, m = cur_sum, m_next
    for sb in subs:
        finish(NA_HEADS - 1, sb, pv(NA_HEADS - 1, sb), prev_sum[sb], even_out[sb])


def _natt_call(qa, ka, va, nag, tbl, layer):
    bsz, seq_len, _ = qa.shape
    n_blk = seq_len // NA_TQ
    assert n_blk % NA_SUB == 0
    blk = pl.BlockSpec((1, NA_SUB * NA_TQ, NA_WIDTH), lambda b, i: (b, i, 0))
    full = pl.BlockSpec((1, seq_len, NA_WIDTH), lambda b, i: (b, 0, 0))
    return pl.pallas_call(
        _natt_kernel,
        grid=(bsz, n_blk // NA_SUB),
        in_specs=[blk, full, full, blk,
                  pl.BlockSpec((None,) + tbl.shape[1:], lambda b, i: (layer, 0, 0, 0, 0),
                               pipeline_mode=pl.Buffered(1))],
        out_specs=blk,
        out_shape=jax.ShapeDtypeStruct((bsz, seq_len, NA_WIDTH), jnp.bfloat16),
        scratch_shapes=[pltpu.VMEM((NA_SUB, 2, NA_TQ, NA_TK), jnp.float32),
                        pltpu.VMEM((NA_SUB, 2, NA_TQ, NA_TK), jnp.bfloat16)],
        compiler_params=pltpu.CompilerParams(
            dimension_semantics=("arbitrary", "arbitrary"), vmem_limit_bytes=VMEM_LIMIT_BYTES),
        name="natt",
    )(qa, ka, va, nag, tbl)


def _mla_kernel(q_ref, k_ref, vt_ref, gate_ref, o_ref, s_scr, p_scr):
    bf16 = jnp.bfloat16
    seq_len = k_ref.shape[1]
    n_chunks = seq_len // MLA_KC
    n_heads = q_ref.shape[2] // HEAD_PAD
    lo_half = _lane_iota() < MLA_V
    sublanes = 8

    def hcols(head):
        return slice(head * HEAD_PAD, (head + 1) * HEAD_PAD)

    def kslice(c):
        return slice(c * MLA_KC, (c + 1) * MLA_KC)

    def qk_chunk(head, c, run_max):
        s = _dot_nt(k_ref[0, kslice(c), hcols(head)], q_ref[0, :, hcols(head)])
        s_scr[head % 2, kslice(c), :] = s
        part = jnp.max(s.reshape(MLA_KC // sublanes, sublanes, MLA_TQ), axis=0)
        return part if run_max is None else jnp.maximum(run_max, part)

    def exp_chunk(head, c, m):
        p_scr[head % 2, kslice(c), :] = jnp.exp2(s_scr[head % 2, kslice(c), :] - m).astype(bf16)

    def pv_chunk(head, c, acc):
        vrows = slice(head * MLA_VT_ROWS, (head + 1) * MLA_VT_ROWS)
        o = _dot(vt_ref[0, vrows, kslice(c)], p_scr[head % 2, kslice(c), :])
        return o if acc is None else acc + o

    def finish(head, acc, prev):
        ones_row = _mla_ones_row(head % 2)
        o = (acc / acc[ones_row:ones_row + 1, :]).T
        if head % 2 == 0:
            return o
        cols = slice((head // 2) * LANES, (head // 2 + 1) * LANES)
        pair = jnp.where(lo_half, prev, o)
        o_ref[0, :, cols] = (pair * gate_ref[0, :, cols].astype(jnp.float32)).astype(bf16)
        return None

    run = None
    for c in range(n_chunks):
        run = qk_chunk(0, c, run)
    m = jnp.max(run, axis=0, keepdims=True)
    acc, even_out = None, None
    for head in range(n_heads):
        run, next_acc = None, None
        for c in range(n_chunks):
            if head + 1 < n_heads:
                run = qk_chunk(head + 1, c, run)
            if head >= 1:
                next_acc = pv_chunk(head - 1, c, next_acc)
            exp_chunk(head, c, m)
        if head >= 1:
            even_out = finish(head - 1, next_acc, even_out)
        if head + 1 < n_heads:
            m = jnp.max(run, axis=0, keepdims=True)
    for c in range(n_chunks):
        acc = pv_chunk(n_heads - 1, c, acc)
    finish(n_heads - 1, acc, even_out)


def _mla_call(qf, kf, vbt, mg):
    bsz, seq_len, _ = qf.shape
    n_groups = MLA_HEADS // MLA_GROUP
    assert seq_len % MLA_KC == 0 and seq_len % MLA_TQ == 0 and MLA_GROUP % 2 == 0
    return pl.pallas_call(
        _mla_kernel,
        grid=(bsz, n_groups, seq_len // MLA_TQ),
        in_specs=[pl.BlockSpec((1, MLA_TQ, MLA_GROUP * HEAD_PAD), lambda b, p, j: (b, j, p)),
                  pl.BlockSpec((1, seq_len, MLA_GROUP * HEAD_PAD), lambda b, p, j: (b, 0, p)),
                  pl.BlockSpec((1, MLA_GROUP * MLA_VT_ROWS, seq_len), lambda b, p, j: (b, p, 0)),
                  pl.BlockSpec((1, MLA_TQ, MLA_GROUP * MLA_V), lambda b, p, j: (b, j, p))],
        out_specs=pl.BlockSpec((1, MLA_TQ, MLA_GROUP * MLA_V), lambda b, p, j: (b, j, p)),
        out_shape=jax.ShapeDtypeStruct((bsz, seq_len, MLA_WIDTH), jnp.bfloat16),
        scratch_shapes=[pltpu.VMEM((2, seq_len, MLA_TQ), jnp.float32),
                        pltpu.VMEM((2, seq_len, MLA_TQ), jnp.bfloat16)],
        compiler_params=pltpu.CompilerParams(
            dimension_semantics=("arbitrary", "arbitrary", "arbitrary"),
            vmem_limit_bytes=VMEM_LIMIT_BYTES),
        name="mla",
    )(qf, kf, vbt, mg)


def _run_tasks(tasks):
    pending = None
    for dot_fn, post_fn, barrier in tasks:
        if barrier and pending is not None:
            pending[0](pending[1])
            pending = None
        y = dot_fn()
        if pending is not None:
            pending[0](pending[1])
        pending = (post_fn, y)
    pending[0](pending[1])


def _out_kernel(x_ref, ana_ref, amla_ref, gna_ref, gmla_ref, wna_ref, wmla_ref, wout_ref, o_ref, y_scr):
    f32 = jnp.float32
    row_tiles = [slice(r, r + PROJ_TILE) for r in range(0, o_ref.shape[0], PROJ_TILE)]
    col_blocks = [slice(c, c + PROJ_BLK) for c in range(0, o_ref.shape[1], PROJ_BLK)]

    def branch_dots(rows, cols):
        return lambda: (_dot(ana_ref[rows, :], wna_ref[:, cols]), _dot(amla_ref[rows, :], wmla_ref[:, cols]))

    def merge(rows, cols):
        def post(us):
            y = gna_ref[rows, cols].astype(f32) * us[0] + gmla_ref[rows, cols].astype(f32) * us[1]
            y_scr[rows, cols] = y.astype(jnp.bfloat16)
        return post

    def out_dot(rows, cols):
        return lambda: _dot(y_scr[rows, :], wout_ref[:, cols])

    def residual(rows, cols):
        def post(y):
            o_ref[rows, cols] = x_ref[rows, cols] + y
        return post

    tasks = [(branch_dots(r, c), merge(r, c), False) for r in row_tiles for c in col_blocks]
    tasks += [(out_dot(r, c), residual(r, c), i == 0)
              for i, (r, c) in enumerate((r, c) for r in row_tiles for c in col_blocks)]
    _run_tasks(tasks)


def _out_call(x2d, a_na, a_mla, gna, gmla, lp, layer):
    n_tok = x2d.shape[0]
    tm = OUT_TILE
    assert n_tok % tm == 0 and tm % PROJ_TILE == 0

    def row_spec(width):
        return pl.BlockSpec((tm, width), lambda i: (i, 0))

    consts = [lp["w_o_na"], lp["w_o_mla"], lp["w_out"]]
    return pl.pallas_call(
        _out_kernel,
        grid=(n_tok // tm,),
        in_specs=[row_spec(D_MODEL), row_spec(NA_WIDTH), row_spec(MLA_WIDTH), row_spec(D_MODEL),
                  row_spec(D_MODEL)] + [_const_spec(c, layer) for c in consts],
        out_specs=row_spec(D_MODEL),
        out_shape=jax.ShapeDtypeStruct((n_tok, D_MODEL), jnp.float32),
        scratch_shapes=[pltpu.VMEM((tm, D_MODEL), jnp.bfloat16)],
        compiler_params=pltpu.CompilerParams(
            dimension_semantics=("arbitrary",), vmem_limit_bytes=VMEM_LIMIT_BYTES),
        name="out",
    )(x2d, a_na, a_mla, gna, gmla, *consts)


_N_DROW = 2 * NA_WIN_ROWS - 1
_N_DCOL = 2 * NA_WIN_COLS - 1


def _na_table_static(seq_len):
    rows = seq_len // GRID_W
    n_blk = rows // NA_QROWS
    qc = np.arange(GRID_W)
    kj = np.arange(GRID_W)
    cs = np.clip(qc - NA_WIN_COLS // 2, 0, GRID_W - NA_WIN_COLS)
    col_valid = (kj[None, :] >= cs[:, None]) & (kj[None, :] < cs[:, None] + NA_WIN_COLS)
    dcol = kj[None, :] - qc[:, None] + (NA_WIN_COLS - 1)
    onehot = (dcol[None] == np.arange(_N_DCOL)[:, None, None]) & col_valid[None]
    col_mask = np.where(col_valid, 0.0, MASK_VALUE).astype(np.float32)
    qr = np.arange(NA_QROWS)
    kn = np.arange(NA_BAND)
    idx = []
    for i in (0, 1, n_blk - 1):
        band = int(np.clip(NA_QROWS * i - NA_WIN_ROWS // 2, 0, rows - NA_BAND))
        r = NA_QROWS * i + qr
        rs = np.clip(r - NA_WIN_ROWS // 2, 0, rows - NA_WIN_ROWS)
        krow = band + kn
        valid = (krow[None, :] >= rs[:, None]) & (krow[None, :] < rs[:, None] + NA_WIN_ROWS)
        drow = krow[None, :] - r[:, None] + (NA_WIN_ROWS - 1)
        idx.append(np.where(valid, drow, _N_DROW))
    return onehot.astype(np.float32), col_mask, np.stack(idx).astype(np.int32)


def _na_table(rel_bias, onehot, col_mask, idx):
    blocks = jnp.einsum("hdc,cqj->hdqj", rel_bias, onehot, precision=lax.Precision.HIGHEST) + col_mask
    blocks = jnp.concatenate(
        [blocks, jnp.full((NA_HEADS, 1, GRID_W, GRID_W), MASK_VALUE, jnp.float32)], axis=1) * LOG2E
    tbl = jnp.take(blocks, idx.reshape(-1), axis=1)
    tbl = tbl.reshape(NA_HEADS, 3, NA_QROWS, NA_BAND, GRID_W, GRID_W)
    return tbl.transpose(1, 0, 2, 4, 3, 5).reshape(3, NA_HEADS, NA_TQ, NA_TK)


def _rope_tables(seq_len):
    t = jnp.arange(seq_len)
    row = (t // GRID_W).astype(jnp.float32)
    col = (t % GRID_W).astype(jnp.float32)
    half = MLA_ROPE // 2
    n_freq = half // 2
    inv = jnp.power(jnp.float32(ROPE_BASE), -jnp.arange(n_freq, dtype=jnp.float32) / n_freq)
    ang = jnp.concatenate([row[:, None] * inv, col[:, None] * inv], axis=-1)
    return jnp.cos(ang), jnp.sin(ang)


def _place(cols_to_blocks, total):
    parts = []
    for arr, width in cols_to_blocks:
        pad = width - arr.shape[-1]
        parts.append(jnp.pad(arr, [(0, 0)] * (arr.ndim - 1) + [(0, pad)]) if pad else arr)
    out = jnp.concatenate(parts, axis=-1)
    assert out.shape[-1] == total
    return out


def _layer_params(p, cos, sin):
    f32, bf16 = jnp.float32, jnp.bfloat16
    half = MLA_ROPE // 2
    w_in = p["w_in"]
    o = 0
    pieces = {}
    for name, size in (("naq", NA_WIDTH), ("nak", NA_WIDTH), ("nav", NA_WIDTH), ("nag", NA_WIDTH),
                       ("cq", MLA_Q_LORA), ("ckv", MLA_KV_LORA), ("kpe", MLA_ROPE),
                       ("mg", MLA_WIDTH), ("gna", D_MODEL), ("gmla", D_MODEL)):
        pieces[name] = w_in[:, o:o + size]
        o += size
    kpe = pieces["kpe"]
    kpe_sw = jnp.concatenate([kpe[:, half:], kpe[:, :half]], axis=-1)
    ckvx = _place([(pieces["ckv"], LANES), (jnp.concatenate([kpe, kpe_sw], axis=-1), LANES)], 2 * LANES)
    w_in_packed = jnp.concatenate(
        [pieces["naq"], pieces["nak"], pieces["nav"], pieces["nag"], pieces["cq"], ckvx,
         pieces["mg"], pieces["gna"], pieces["gmla"]], axis=-1).astype(bf16)
    assert w_in_packed.shape[-1] == _W_IN_COLS

    w_uq = p["w_uq"].reshape(MLA_Q_LORA, MLA_HEADS, MLA_QK_DIM)
    wq = _place([(w_uq, HEAD_PAD)], HEAD_PAD).reshape(MLA_Q_LORA, MLA_PAD_WIDTH).astype(bf16)

    w_ukv = p["w_ukv"].reshape(MLA_KV_LORA, MLA_HEADS, MLA_NOPE + MLA_V)
    k_nope = _place([(w_ukv[..., :MLA_NOPE], HEAD_PAD)], HEAD_PAD).reshape(MLA_KV_LORA, MLA_PAD_WIDTH)
    eye = jnp.eye(MLA_ROPE, dtype=f32)
    put = jnp.pad(eye, ((0, 0), (MLA_NOPE, HEAD_PAD - MLA_QK_DIM)))
    put = jnp.tile(put, (1, MLA_HEADS))
    wk = jnp.concatenate([k_nope, put, put, jnp.zeros((2 * LANES - MLA_KV_LORA - 2 * MLA_ROPE,
                                                        MLA_PAD_WIDTH), f32)], axis=0).astype(bf16)
    w_uv = w_ukv[..., MLA_NOPE:].reshape(MLA_KV_LORA, MLA_HEADS // 2, 2, MLA_V)
    zv = jnp.zeros_like(w_uv[:, :, 0])
    wvt = jnp.stack([jnp.concatenate([w_uv[:, :, 0], zv], -1), jnp.concatenate([zv, w_uv[:, :, 1]], -1)], axis=2)
    wvt = wvt.reshape(MLA_KV_LORA, MLA_HEADS * MLA_VT_ROWS).T.astype(bf16)

    gq = p["mla_q_norm"]
    gk = p["mla_k_norm"]
    scale = MLA_QK_DIM ** -0.5 * LOG2E
    seq_len = cos.shape[0]
    ones = jnp.ones((seq_len, 1), f32)
    tq1 = _place([(ones * gq[None, :MLA_NOPE], MLA_NOPE),
                  (cos * gq[None, MLA_NOPE:MLA_NOPE + half], half),
                  (cos * gq[None, MLA_NOPE + half:], half)], MLA_QK_DIM) * scale
    tq2 = _place([(jnp.zeros((seq_len, MLA_NOPE), f32), MLA_NOPE),
                  (-sin * gq[None, MLA_NOPE + half:], half),
                  (sin * gq[None, MLA_NOPE:MLA_NOPE + half], half)], MLA_QK_DIM) * scale
    e1 = jnp.concatenate([cos * gk[None, MLA_NOPE:MLA_NOPE + half], cos * gk[None, MLA_NOPE + half:]], -1)
    e2 = jnp.concatenate([-sin * gk[None, MLA_NOPE + half:], sin * gk[None, MLA_NOPE:MLA_NOPE + half]], -1)
    tabs = {"tq1": _place([(tq1, LANES)], LANES), "tq2": _place([(tq2, LANES)], LANES),
            "e1": _place([(e1, LANES)], LANES), "e2": _place([(e2, LANES)], LANES)}
    gkf = _place([(gk[None, :MLA_NOPE], MLA_NOPE), (jnp.ones((1, MLA_ROPE), f32), LANES - MLA_NOPE)], LANES)

    na_scale = NA_HEAD_DIM ** -0.5 * LOG2E
    lp = {
        "ln_g": p["ln_g"][None, :],
        "w_in": w_in_packed,
        "gq2": jnp.tile(p["na_q_norm"], 2)[None, :] * na_scale,
        "gk2": jnp.tile(p["na_k_norm"], 2)[None, :],
        "gcq": p["mla_cq_norm"][None, :],
        "gckv": p["mla_ckv_norm"][None, :],
        "wq": wq, "wk": wk, "wvt": wvt, "gkf": gkf,
        "w_o_na": p["w_o_na"].astype(bf16),
        "w_o_mla": p["w_o_mla"].astype(bf16),
        "w_out": p["w_out"].astype(bf16),
    }
    return lp, tabs


@jax.jit
def _forward(x, p):
    bsz, seq_len, _ = x.shape
    depth = p["w_in"].shape[0]
    cos, sin = _rope_tables(seq_len)
    onehot, col_mask, tbl_idx = _na_table_static(seq_len)
    x2d = x.reshape(bsz * seq_len, D_MODEL)
    lp, tabs = jax.vmap(lambda pl_: _layer_params(pl_, cos, sin))(p)
    tbl = jax.vmap(lambda rb: _na_table(rb, onehot, col_mask, tbl_idx))(p["na_rel_bias"])
    for layer in range(depth):
        qa, ka, va, nag, qf, kf, mg, gna, gmla, vbt = _proj_call(x2d, lp, tabs, layer, seq_len)

        def b3(a):
            return a.reshape(bsz, seq_len, a.shape[-1])

        a_na = _natt_call(b3(qa), b3(ka), b3(va), b3(nag), tbl, layer)
        a_mla = _mla_call(b3(qf), b3(kf), vbt, b3(mg))
        x2d = _out_call(x2d, a_na.reshape(-1, NA_WIDTH), a_mla.reshape(-1, MLA_WIDTH), gna, gmla, lp, layer)
    return x2d.reshape(bsz, seq_len, D_MODEL)


def kernel(x, ln_g, w_in, na_q_norm, na_k_norm, na_rel_bias, mla_cq_norm, mla_ckv_norm, w_uq, w_ukv,
           mla_q_norm, mla_k_norm, w_o_na, w_o_mla, w_out):
    params = dict(ln_g=ln_g, w_in=w_in, na_q_norm=na_q_norm, na_k_norm=na_k_norm,
                  na_rel_bias=na_rel_bias, mla_cq_norm=mla_cq_norm, mla_ckv_norm=mla_ckv_norm,
                  w_uq=w_uq, w_ukv=w_ukv, mla_q_norm=mla_q_norm, mla_k_norm=mla_k_norm,
                  w_o_na=w_o_na, w_o_mla=w_o_mla, w_out=w_out)
    return _forward(x, params)
```

```python
import numpy as np
import jax
import jax.numpy as jnp
from jax import lax
from jax.experimental import pallas as pl
from jax.experimental.pallas import tpu as pltpu

D_MODEL = 1024
GRID_W = 64
NA_HEADS = 8
NA_HEAD_DIM = 64
NA_WIDTH = NA_HEADS * NA_HEAD_DIM
NA_WIN_ROWS = 8
NA_WIN_COLS = 16
MLA_HEADS = 8
MLA_NOPE = 64
MLA_ROPE = 32
MLA_QK_DIM = MLA_NOPE + MLA_ROPE
MLA_V = 64
MLA_Q_LORA = 256
MLA_KV_LORA = 128
MLA_WIDTH = MLA_HEADS * MLA_V
ROPE_BASE = 10000.0
EPS = 1e-6

LANES = 128
HEAD_PAD = LANES
MLA_PAD_WIDTH = MLA_HEADS * HEAD_PAD
MLA_VT_ROWS = 128
VMEM_LIMIT_BYTES = 56 * 1024 * 1024

_OFF_NAQ = 0
_OFF_NAK = _OFF_NAQ + NA_WIDTH
_OFF_NAV = _OFF_NAK + NA_WIDTH
_OFF_NAG = _OFF_NAV + NA_WIDTH
_OFF_CQ = _OFF_NAG + NA_WIDTH
_OFF_CKV = _OFF_CQ + MLA_Q_LORA
_OFF_MG = _OFF_CKV + 2 * LANES
_OFF_GNA = _OFF_MG + MLA_WIDTH
_OFF_GMLA = _OFF_GNA + D_MODEL
_W_IN_COLS = _OFF_GMLA + D_MODEL

PROJ_TILE = 512
OUT_TILE = 1024
PROJ_BLK = 256
NA_QROWS = 4
NA_BAND = 12
NA_TQ = NA_QROWS * GRID_W
NA_TK = NA_BAND * GRID_W
NA_SUB = 4
MLA_TQ = 256
MLA_KC = 512
MLA_GROUP = 8
MASK_VALUE = -1e30
LOG2E = 1.4426950408889634

_NT_DIMS = (((1,), (1,)), ((), ()))


def _dot(a, b):
    return jnp.dot(a, b, preferred_element_type=jnp.float32)


def _dot_nt(a, b):
    return lax.dot_general(a, b, _NT_DIMS, preferred_element_type=jnp.float32)


def _lane_iota():
    return lax.broadcasted_iota(jnp.int32, (1, LANES), 1)


def _mla_ones_row(parity):
    return MLA_V if parity == 0 else 0


def _proj_kernel(x_ref, lng_ref, w_ref, gq_ref, gk_ref, gcq_ref, gckv_ref, wq_ref,
                 wk_ref, wvt_ref, gkf_ref, tq1_ref, tq2_ref, e1_ref, e2_ref,
                 qa_ref, ka_ref, va_ref, nag_ref, qf_ref, kf_ref, mg_ref,
                 gna_ref, gmla_ref, vbt_ref, hb_scr, cqn_scr, lhsk_scr, sspe_scr):
    bf16 = jnp.bfloat16
    f32 = jnp.float32
    half_rows = hb_scr.shape[0] // 2
    x = x_ref[...]
    hb_scr[...] = (x * lax.rsqrt(jnp.mean(x * x, axis=-1, keepdims=True) + EPS) * lng_ref[...]).astype(bf16)
    lo_half = _lane_iota() < NA_HEAD_DIM
    rope_lanes = _lane_iota() < MLA_ROPE
    row_iota = lax.broadcasted_iota(jnp.int32, (MLA_VT_ROWS, 1), 0)

    def tiles(y):
        return [y[:, i * LANES:(i + 1) * LANES] for i in range(y.shape[1] // LANES)]

    def in_dot(off):
        return lambda: _dot(hb_scr[...], w_ref[:, off:off + PROJ_BLK])

    def headnorm64(g_ref):
        def fn(y):
            out = []
            for pc in tiles(y):
                sq = pc * pc
                s_lo = jnp.sum(jnp.where(lo_half, sq, 0.0), axis=-1, keepdims=True)
                s_hi = jnp.sum(jnp.where(lo_half, 0.0, sq), axis=-1, keepdims=True)
                r = jnp.where(lo_half,
                              lax.rsqrt(s_lo * (1.0 / NA_HEAD_DIM) + EPS),
                              lax.rsqrt(s_hi * (1.0 / NA_HEAD_DIM) + EPS))
                out.append(pc * r * g_ref[...])
            return jnp.concatenate(out, axis=-1)
        return fn

    def silu(g):
        return g * jax.nn.sigmoid(g)

    def post_cq(y):
        cqn_scr[...] = (y * lax.rsqrt(jnp.mean(y * y, axis=-1, keepdims=True) + EPS) * gcq_ref[...]).astype(bf16)

    def post_ckv(y):
        ckv, e = tiles(y)
        lhsk_scr[:, :LANES] = (ckv * lax.rsqrt(jnp.mean(ckv * ckv, axis=-1, keepdims=True) + EPS)
                               * gckv_ref[...]).astype(bf16)
        ss_pe = jnp.sum(jnp.where(rope_lanes, e * e, 0.0), axis=-1, keepdims=True)
        sspe_scr[...] = jnp.broadcast_to(ss_pe, sspe_scr.shape)
        rot = e * e1_ref[...] + pltpu.roll(e, LANES - MLA_ROPE, axis=1) * e2_ref[...]
        rot_hi = rot.astype(bf16).astype(f32)
        rot_lo = (rot - rot_hi).astype(bf16).astype(f32)
        lhsk_scr[:, LANES:] = (rot_hi + pltpu.roll(rot_lo, MLA_ROPE, axis=1)).astype(bf16)

    def head_tasks(src_scr, w2_ref, out_ref, head_fn):
        tasks = []
        for r0 in (0, half_rows):
            rows = slice(r0, r0 + half_rows)
            for c in range(0, MLA_PAD_WIDTH, 2 * PROJ_BLK):
                def dot_fn(rows=rows, c=c):
                    return _dot(src_scr[rows, :], w2_ref[:, c:c + 2 * PROJ_BLK])

                def post(y, rows=rows, c=c):
                    for i, a in enumerate(tiles(y)):
                        out_ref[rows, c + i * HEAD_PAD:c + (i + 1) * HEAD_PAD] = head_fn(a, rows).astype(bf16)
                tasks.append((dot_fn, post))
        return tasks

    first_rope_half = _lane_iota() < MLA_NOPE + MLA_ROPE // 2

    def q_head(a, rows):
        b = jnp.where(first_rope_half, pltpu.roll(a, LANES - MLA_ROPE // 2, axis=1),
                      pltpu.roll(a, MLA_ROPE // 2, axis=1))
        r = lax.rsqrt(jnp.sum(a * a, axis=-1, keepdims=True) / MLA_QK_DIM + EPS)
        return (a * tq1_ref[rows, :] + b * tq2_ref[rows, :]) * r

    def k_head(a, rows):
        ss = jnp.sum(jnp.where(lo_half, a * a, 0.0), axis=-1, keepdims=True) + sspe_scr[rows, :]
        return a * gkf_ref[...] * lax.rsqrt(ss / MLA_QK_DIM + EPS)

    def v_dot(pair):
        rows = slice(pair * 2 * MLA_VT_ROWS, (pair + 1) * 2 * MLA_VT_ROWS)
        return lambda: _dot_nt(wvt_ref[rows, :], lhsk_scr[:, :LANES])

    def v_post(pair):
        def post(y):
            for i in range(2):
                row0 = (2 * pair + i) * MLA_VT_ROWS
                a = y[i * MLA_VT_ROWS:(i + 1) * MLA_VT_ROWS, :]
                vbt_ref[0, row0:row0 + MLA_VT_ROWS, :] = jnp.where(row_iota == _mla_ones_row(i), 1.0, a).astype(bf16)
        return post

    def wide_blocks(off, width, out_ref, fn):
        tasks = []
        for r0 in (0, half_rows):
            for c in range(0, width, 2 * PROJ_BLK):
                def dot_fn(r0=r0, c=c):
                    return _dot(hb_scr[r0:r0 + half_rows, :], w_ref[:, off + c:off + c + 2 * PROJ_BLK])

                def post(y, r0=r0, c=c):
                    out_ref[r0:r0 + half_rows, c:c + 2 * PROJ_BLK] = fn(y).astype(bf16)
                tasks.append((dot_fn, post))
        return tasks

    tasks = [(in_dot(_OFF_CQ), post_cq), (in_dot(_OFF_CKV), post_ckv)]
    tasks += wide_blocks(_OFF_NAQ, NA_WIDTH, qa_ref, headnorm64(gq_ref))
    tasks += wide_blocks(_OFF_NAK, NA_WIDTH, ka_ref, headnorm64(gk_ref))
    tasks += head_tasks(cqn_scr, wq_ref, qf_ref, q_head)
    tasks += wide_blocks(_OFF_GNA, D_MODEL, gna_ref, jax.nn.sigmoid)
    tasks += head_tasks(lhsk_scr, wk_ref, kf_ref, k_head)
    tasks += wide_blocks(_OFF_NAG, NA_WIDTH, nag_ref, silu)
    tasks += [(v_dot(p), v_post(p)) for p in range(MLA_PAD_WIDTH // PROJ_BLK)]
    tasks += wide_blocks(_OFF_MG, MLA_WIDTH, mg_ref, silu)
    tasks += wide_blocks(_OFF_GMLA, D_MODEL, gmla_ref, jax.nn.sigmoid)
    tasks += wide_blocks(_OFF_NAV, NA_WIDTH, va_ref, lambda y: y)

    _run_tasks([(dot_fn, post_fn, False) for dot_fn, post_fn in tasks])


def _const_spec(stacked, layer):
    rest = stacked.shape[1:]
    return pl.BlockSpec((None,) + rest, lambda i: (layer,) + (0,) * len(rest), pipeline_mode=pl.Buffered(1))


def _proj_call(x2d, lp, tabs, layer, seq_len):
    n_tok = x2d.shape[0]
    tm = PROJ_TILE
    assert n_tok % tm == 0 and seq_len % tm == 0
    tiles_per_seq = seq_len // tm
    bf16 = jnp.bfloat16

    def row_spec(width):
        return pl.BlockSpec((tm, width), lambda i: (i, 0))

    def tab_spec():
        return pl.BlockSpec((None, tm, LANES), lambda i: (layer, i % tiles_per_seq, 0))

    consts = [lp["ln_g"], lp["w_in"], lp["gq2"], lp["gk2"], lp["gcq"], lp["gckv"],
              lp["wq"], lp["wk"], lp["wvt"], lp["gkf"]]
    out_widths = [NA_WIDTH, NA_WIDTH, NA_WIDTH, NA_WIDTH, MLA_PAD_WIDTH, MLA_PAD_WIDTH,
                  MLA_WIDTH, D_MODEL, D_MODEL]
    t_widths = [MLA_HEADS * MLA_VT_ROWS]
    t_specs = [pl.BlockSpec((1, w, tm), lambda i: (i // tiles_per_seq, 0, i % tiles_per_seq)) for w in t_widths]
    t_shapes = [jax.ShapeDtypeStruct((n_tok // seq_len, w, seq_len), bf16) for w in t_widths]
    return pl.pallas_call(
        _proj_kernel,
        grid=(n_tok // tm,),
        in_specs=[row_spec(D_MODEL)] + [_const_spec(c, layer) for c in consts] + [tab_spec()] * 4,
        out_specs=[row_spec(w) for w in out_widths] + t_specs,
        out_shape=[jax.ShapeDtypeStruct((n_tok, w), bf16) for w in out_widths] + t_shapes,
        scratch_shapes=[pltpu.VMEM((tm, D_MODEL), bf16),
                        pltpu.VMEM((tm, MLA_Q_LORA), bf16),
                        pltpu.VMEM((tm, 2 * LANES), bf16),
                        pltpu.VMEM((tm, LANES), jnp.float32)],
        compiler_params=pltpu.CompilerParams(
            dimension_semantics=("arbitrary",), vmem_limit_bytes=VMEM_LIMIT_BYTES),
        name="proj",
    )(x2d, *consts, tabs["tq1"], tabs["tq2"], tabs["e1"], tabs["e2"])


def _na_band_start(i, rows):
    return jnp.clip(NA_QROWS * i - NA_WIN_ROWS // 2, 0, rows - NA_BAND)


def _natt_kernel(q_ref, k_ref, v_ref, gate_ref, tbl_ref, o_ref, s_scr, p_scr):
    bf16 = jnp.bfloat16
    i = pl.program_id(1)
    rows = k_ref.shape[1] // GRID_W
    n_blk = pl.num_programs(1) * NA_SUB
    subs = range(NA_SUB)
    blks = [NA_SUB * i + sb for sb in subs]
    starts = [pl.multiple_of(_na_band_start(blk, rows) * GRID_W, GRID_W) for blk in blks]
    pats = [jnp.minimum(blk, 1) + (blk == n_blk - 1).astype(jnp.int32) for blk in blks]
    lo_half = _lane_iota() < NA_HEAD_DIM

    def pcols(head):
        return slice((head // 2) * LANES, (head // 2 + 1) * LANES)

    def qrows(sb):
        return slice(sb * NA_TQ, (sb + 1) * NA_TQ)

    def fold(x, op):
        run = None
        for t in range(x.shape[1] // LANES):
            tile = x[:, t * LANES:(t + 1) * LANES]
            run = tile if run is None else op(run, tile)
        return run

    def scores(head, sb):
        qp = q_ref[0, qrows(sb), pcols(head)]
        sel = lo_half if head % 2 == 0 else jnp.logical_not(lo_half)
        qm = jnp.where(sel, qp, jnp.zeros_like(qp))
        keys = k_ref[0, pl.ds(starts[sb], NA_TK), pcols(head)]
        s = _dot_nt(qm, keys) + tbl_ref[pats[sb], head]
        s_scr[sb, head % 2] = s
        return jnp.max(fold(s, jnp.maximum), axis=-1, keepdims=True)

    def probs(head, sb, m):
        p = jnp.exp2(s_scr[sb, head % 2] - m)
        p_scr[sb, head % 2] = p.astype(bf16)
        return fold(p, jnp.add)

    def pv(head, sb):
        return _dot(p_scr[sb, head % 2], v_ref[0, pl.ds(starts[sb], NA_TK), pcols(head)])

    def finish(head, sb, acc, run_sum, prev):
        o = acc / jnp.sum(run_sum, axis=-1, keepdims=True)
        if head % 2 == 0:
            return o
        pair = jnp.where(lo_half, prev, o)
        gate = gate_ref[0, qrows(sb), pcols(head)].astype(jnp.float32)
        o_ref[0, qrows(sb), pcols(head)] = (pair * gate).astype(bf16)
        return None

    m = [scores(0, sb) for sb in subs]
    prev_sum, even_out = [None] * NA_SUB, [None] * NA_SUB
    for head in range(NA_HEADS):
        m_next = [scores(head + 1, sb) for sb in subs] if head + 1 < NA_HEADS else None
        acc = [pv(head - 1, sb) for sb in subs] if head >= 1 else None
        cur_sum = [probs(head, sb, m[sb]) for sb in subs]
        if head >= 1:
            even_out = [finish(head - 1, sb, acc[sb], prev_sum[sb], even_out[sb]) for sb in subs]
        prev_sum, m = cur_sum, m_next
    for sb in subs:
        finish(NA_HEADS - 1, sb, pv(NA_HEADS - 1, sb), prev_sum[sb], even_out[sb])


def _natt_call(qa, ka, va, nag, tbl, layer):
    bsz, seq_len, _ = qa.shape
    n_blk = seq_len // NA_TQ
    assert n_blk % NA_SUB == 0
    blk = pl.BlockSpec((1, NA_SUB * NA_TQ, NA_WIDTH), lambda b, i: (b, i, 0))
    full = pl.BlockSpec((1, seq_len, NA_WIDTH), lambda b, i: (b, 0, 0))
    return pl.pallas_call(
        _natt_kernel,
        grid=(bsz, n_blk // NA_SUB),
        in_specs=[blk, full, full, blk,
                  pl.BlockSpec((None,) + tbl.shape[1:], lambda b, i: (layer, 0, 0, 0, 0),
                               pipeline_mode=pl.Buffered(1))],
        out_specs=blk,
        out_shape=jax.ShapeDtypeStruct((bsz, seq_len, NA_WIDTH), jnp.bfloat16),
        scratch_shapes=[pltpu.VMEM((NA_SUB, 2, NA_TQ, NA_TK), jnp.float32),
                        pltpu.VMEM((NA_SUB, 2, NA_TQ, NA_TK), jnp.bfloat16)],
        compiler_params=pltpu.CompilerParams(
            dimension_semantics=("arbitrary", "arbitrary"), vmem_limit_bytes=VMEM_LIMIT_BYTES),
        name="natt",
    )(qa, ka, va, nag, tbl)


def _mla_kernel(q_ref, k_ref, vt_ref, gate_ref, o_ref, s_scr, p_scr):
    bf16 = jnp.bfloat16
    seq_len = k_ref.shape[1]
    n_chunks = seq_len // MLA_KC
    n_heads = q_ref.shape[2] // HEAD_PAD
    lo_half = _lane_iota() < MLA_V
    sublanes = 8

    def hcols(head):
        return slice(head * HEAD_PAD, (head + 1) * HEAD_PAD)

    def kslice(c):
        return slice(c * MLA_KC, (c + 1) * MLA_KC)

    def qk_chunk(head, c, run_max):
        s = _dot_nt(k_ref[0, kslice(c), hcols(head)], q_ref[0, :, hcols(head)])
        s_scr[head % 2, kslice(c), :] = s
        part = jnp.max(s.reshape(MLA_KC // sublanes, sublanes, MLA_TQ), axis=0)
        return part if run_max is None else jnp.maximum(run_max, part)

    def exp_chunk(head, c, m):
        p_scr[head % 2, kslice(c), :] = jnp.exp2(s_scr[head % 2, kslice(c), :] - m).astype(bf16)

    def pv_chunk(head, c, acc):
        vrows = slice(head * MLA_VT_ROWS, (head + 1) * MLA_VT_ROWS)
        o = _dot(vt_ref[0, vrows, kslice(c)], p_scr[head % 2, kslice(c), :])
        return o if acc is None else acc + o

    def finish(head, acc, prev):
        ones_row = _mla_ones_row(head % 2)
        o = (acc / acc[ones_row:ones_row + 1, :]).T
        if head % 2 == 0:
            return o
        cols = slice((head // 2) * LANES, (head // 2 + 1) * LANES)
        pair = jnp.where(lo_half, prev, o)
        o_ref[0, :, cols] = (pair * gate_ref[0, :, cols].astype(jnp.float32)).astype(bf16)
        return None

    run = None
    for c in range(n_chunks):
        run = qk_chunk(0, c, run)
    m = jnp.max(run, axis=0, keepdims=True)
    acc, even_out = None, None
    for head in range(n_heads):
        run, next_acc = None, None
        for c in range(n_chunks):
            if head + 1 < n_heads:
                run = qk_chunk(head + 1, c, run)
            if head >= 1:
                next_acc = pv_chunk(head - 1, c, next_acc)
            exp_chunk(head, c, m)
        if head >= 1:
            even_out = finish(head - 1, next_acc, even_out)
        if head + 1 < n_heads:
            m = jnp.max(run, axis=0, keepdims=True)
    for c in range(n_chunks):
        acc = pv_chunk(n_heads - 1, c, acc)
    finish(n_heads - 1, acc, even_out)


def _mla_call(qf, kf, vbt, mg):
    bsz, seq_len, _ = qf.shape
    n_groups = MLA_HEADS // MLA_GROUP
    assert seq_len % MLA_KC == 0 and seq_len % MLA_TQ == 0 and MLA_GROUP % 2 == 0
    return pl.pallas_call(
        _mla_kernel,
        grid=(bsz, n_groups, seq_len // MLA_TQ),
        in_specs=[pl.BlockSpec((1, MLA_TQ, MLA_GROUP * HEAD_PAD), lambda b, p, j: (b, j, p)),
                  pl.BlockSpec((1, seq_len, MLA_GROUP * HEAD_PAD), lambda b, p, j: (b, 0, p)),
                  pl.BlockSpec((1, MLA_GROUP * MLA_VT_ROWS, seq_len), lambda b, p, j: (b, p, 0)),
                  pl.BlockSpec((1, MLA_TQ, MLA_GROUP * MLA_V), lambda b, p, j: (b, j, p))],
        out_specs=pl.BlockSpec((1, MLA_TQ, MLA_GROUP * MLA_V), lambda b, p, j: (b, j, p)),
        out_shape=jax.ShapeDtypeStruct((bsz, seq_len, MLA_WIDTH), jnp.bfloat16),
        scratch_shapes=[pltpu.VMEM((2, seq_len, MLA_TQ), jnp.float32),
                        pltpu.VMEM((2, seq_len, MLA_TQ), jnp.bfloat16)],
        compiler_params=pltpu.CompilerParams(
            dimension_semantics=("arbitrary", "arbitrary", "arbitrary"),
            vmem_limit_bytes=VMEM_LIMIT_BYTES),
        name="mla",
    )(qf, kf, vbt, mg)


def _run_tasks(tasks):
    pending = None
    for dot_fn, post_fn, barrier in tasks:
        if barrier and pending is not None:
            pending[0](pending[1])
            pending = None
        y = dot_fn()
        if pending is not None:
            pending[0](pending[1])
        pending = (post_fn, y)
    pending[0](pending[1])


def _out_kernel(x_ref, ana_ref, amla_ref, gna_ref, gmla_ref, wna_ref, wmla_ref, wout_ref, o_ref, y_scr):
    f32 = jnp.float32
    row_tiles = [slice(r, r + PROJ_TILE) for r in range(0, o_ref.shape[0], PROJ_TILE)]
    col_blocks = [slice(c, c + PROJ_BLK) for c in range(0, o_ref.shape[1], PROJ_BLK)]

    def branch_dots(rows, cols):
        return lambda: (_dot(ana_ref[rows, :], wna_ref[:, cols]), _dot(amla_ref[rows, :], wmla_ref[:, cols]))

    def merge(rows, cols):
        def post(us):
            y = gna_ref[rows, cols].astype(f32) * us[0] + gmla_ref[rows, cols].astype(f32) * us[1]
            y_scr[rows, cols] = y.astype(jnp.bfloat16)
        return post

    def out_dot(rows, cols):
        return lambda: _dot(y_scr[rows, :], wout_ref[:, cols])

    def residual(rows, cols):
        def post(y):
            o_ref[rows, cols] = x_ref[rows, cols] + y
        return post

    tasks = [(branch_dots(r, c), merge(r, c), False) for r in row_tiles for c in col_blocks]
    tasks += [(out_dot(r, c), residual(r, c), i == 0)
              for i, (r, c) in enumerate((r, c) for r in row_tiles for c in col_blocks)]
    _run_tasks(tasks)


def _out_call(x2d, a_na, a_mla, gna, gmla, lp, layer):
    n_tok = x2d.shape[0]
    tm = OUT_TILE
    assert n_tok % tm == 0 and tm % PROJ_TILE == 0

    def row_spec(width):
        return pl.BlockSpec((tm, width), lambda i: (i, 0))

    consts = [lp["w_o_na"], lp["w_o_mla"], lp["w_out"]]
    return pl.pallas_call(
        _out_kernel,
        grid=(n_tok // tm,),
        in_specs=[row_spec(D_MODEL), row_spec(NA_WIDTH), row_spec(MLA_WIDTH), row_spec(D_MODEL),
                  row_spec(D_MODEL)] + [_const_spec(c, layer) for c in consts],
        out_specs=row_spec(D_MODEL),
        out_shape=jax.ShapeDtypeStruct((n_tok, D_MODEL), jnp.float32),
        scratch_shapes=[pltpu.VMEM((tm, D_MODEL), jnp.bfloat16)],
        compiler_params=pltpu.CompilerParams(
            dimension_semantics=("arbitrary",), vmem_limit_bytes=VMEM_LIMIT_BYTES),
        name="out",
    )(x2d, a_na, a_mla, gna, gmla, *consts)


_N_DROW = 2 * NA_WIN_ROWS - 1
_N_DCOL = 2 * NA_WIN_COLS - 1


def _na_table_static(seq_len):
    rows = seq_len // GRID_W
    n_blk = rows // NA_QROWS
    qc = np.arange(GRID_W)
    kj = np.arange(GRID_W)
    cs = np.clip(qc - NA_WIN_COLS // 2, 0, GRID_W - NA_WIN_COLS)
    col_valid = (kj[None, :] >= cs[:, None]) & (kj[None, :] < cs[:, None] + NA_WIN_COLS)
    dcol = kj[None, :] - qc[:, None] + (NA_WIN_COLS - 1)
    onehot = (dcol[None] == np.arange(_N_DCOL)[:, None, None]) & col_valid[None]
    col_mask = np.where(col_valid, 0.0, MASK_VALUE).astype(np.float32)
    qr = np.arange(NA_QROWS)
    kn = np.arange(NA_BAND)
    idx = []
    for i in (0, 1, n_blk - 1):
        band = int(np.clip(NA_QROWS * i - NA_WIN_ROWS // 2, 0, rows - NA_BAND))
        r = NA_QROWS * i + qr
        rs = np.clip(r - NA_WIN_ROWS // 2, 0, rows - NA_WIN_ROWS)
        krow = band + kn
        valid = (krow[None, :] >= rs[:, None]) & (krow[None, :] < rs[:, None] + NA_WIN_ROWS)
        drow = krow[None, :] - r[:, None] + (NA_WIN_ROWS - 1)
        idx.append(np.where(valid, drow, _N_DROW))
    return onehot.astype(np.float32), col_mask, np.stack(idx).astype(np.int32)


def _na_table(rel_bias, onehot, col_mask, idx):
    blocks = jnp.einsum("hdc,cqj->hdqj", rel_bias, onehot, precision=lax.Precision.HIGHEST) + col_mask
    blocks = jnp.concatenate(
        [blocks, jnp.full((NA_HEADS, 1, GRID_W, GRID_W), MASK_VALUE, jnp.float32)], axis=1) * LOG2E
    tbl = jnp.take(blocks, idx.reshape(-1), axis=1)
    tbl = tbl.reshape(NA_HEADS, 3, NA_QROWS, NA_BAND, GRID_W, GRID_W)
    return tbl.transpose(1, 0, 2, 4, 3, 5).reshape(3, NA_HEADS, NA_TQ, NA_TK)


def _rope_tables(seq_len):
    t = jnp.arange(seq_len)
    row = (t // GRID_W).astype(jnp.float32)
    col = (t % GRID_W).astype(jnp.float32)
    half = MLA_ROPE // 2
    n_freq = half // 2
    inv = jnp.power(jnp.float32(ROPE_BASE), -jnp.arange(n_freq, dtype=jnp.float32) / n_freq)
    ang = jnp.concatenate([row[:, None] * inv, col[:, None] * inv], axis=-1)
    return jnp.cos(ang), jnp.sin(ang)


def _place(cols_to_blocks, total):
    parts = []
    for arr, width in cols_to_blocks:
        pad = width - arr.shape[-1]
        parts.append(jnp.pad(arr, [(0, 0)] * (arr.ndim - 1) + [(0, pad)]) if pad else arr)
    out = jnp.concatenate(parts, axis=-1)
    assert out.shape[-1] == total
    return out


def _layer_params(p, cos, sin):
    f32, bf16 = jnp.float32, jnp.bfloat16
    half = MLA_ROPE // 2
    w_in = p["w_in"]
    o = 0
    pieces = {}
    for name, size in (("naq", NA_WIDTH), ("nak", NA_WIDTH), ("nav", NA_WIDTH), ("nag", NA_WIDTH),
                       ("cq", MLA_Q_LORA), ("ckv", MLA_KV_LORA), ("kpe", MLA_ROPE),
                       ("mg", MLA_WIDTH), ("gna", D_MODEL), ("gmla", D_MODEL)):
        pieces[name] = w_in[:, o:o + size]
        o += size
    kpe = pieces["kpe"]
    kpe_sw = jnp.concatenate([kpe[:, half:], kpe[:, :half]], axis=-1)
    ckvx = _place([(pieces["ckv"], LANES), (jnp.concatenate([kpe, kpe_sw], axis=-1), LANES)], 2 * LANES)
    w_in_packed = jnp.concatenate(
        [pieces["naq"], pieces["nak"], pieces["nav"], pieces["nag"], pieces["cq"], ckvx,
         pieces["mg"], pieces["gna"], pieces["gmla"]], axis=-1).astype(bf16)
    assert w_in_packed.shape[-1] == _W_IN_COLS

    w_uq = p["w_uq"].reshape(MLA_Q_LORA, MLA_HEADS, MLA_QK_DIM)
    wq = _place([(w_uq, HEAD_PAD)], HEAD_PAD).reshape(MLA_Q_LORA, MLA_PAD_WIDTH).astype(bf16)

    w_ukv = p["w_ukv"].reshape(MLA_KV_LORA, MLA_HEADS, MLA_NOPE + MLA_V)
    k_nope = _place([(w_ukv[..., :MLA_NOPE], HEAD_PAD)], HEAD_PAD).reshape(MLA_KV_LORA, MLA_PAD_WIDTH)
    eye = jnp.eye(MLA_ROPE, dtype=f32)
    put = jnp.pad(eye, ((0, 0), (MLA_NOPE, HEAD_PAD - MLA_QK_DIM)))
    put = jnp.tile(put, (1, MLA_HEADS))
    wk = jnp.concatenate([k_nope, put, put, jnp.zeros((2 * LANES - MLA_KV_LORA - 2 * MLA_ROPE,
                                                        MLA_PAD_WIDTH), f32)], axis=0).astype(bf16)
    w_uv = w_ukv[..., MLA_NOPE:].reshape(MLA_KV_LORA, MLA_HEADS // 2, 2, MLA_V)
    zv = jnp.zeros_like(w_uv[:, :, 0])
    wvt = jnp.stack([jnp.concatenate([w_uv[:, :, 0], zv], -1), jnp.concatenate([zv, w_uv[:, :, 1]], -1)], axis=2)
    wvt = wvt.reshape(MLA_KV_LORA, MLA_HEADS * MLA_VT_ROWS).T.astype(bf16)

    gq = p["mla_q_norm"]
    gk = p["mla_k_norm"]
    scale = MLA_QK_DIM ** -0.5 * LOG2E
    seq_len = cos.shape[0]
    ones = jnp.ones((seq_len, 1), f32)
    tq1 = _place([(ones * gq[None, :MLA_NOPE], MLA_NOPE),
                  (cos * gq[None, MLA_NOPE:MLA_NOPE + half], half),
                  (cos * gq[None, MLA_NOPE + half:], half)], MLA_QK_DIM) * scale
    tq2 = _place([(jnp.zeros((seq_len, MLA_NOPE), f32), MLA_NOPE),
                  (-sin * gq[None, MLA_NOPE + half:], half),
                  (sin * gq[None, MLA_NOPE:MLA_NOPE + half], half)], MLA_QK_DIM) * scale
    e1 = jnp.concatenate([cos * gk[None, MLA_NOPE:MLA_NOPE + half], cos * gk[None, MLA_NOPE + half:]], -1)
    e2 = jnp.concatenate([-sin * gk[None, MLA_NOPE + half:], sin * gk[None, MLA_NOPE:MLA_NOPE + half]], -1)
    tabs = {"tq1": _place([(tq1, LANES)], LANES), "tq2": _place([(tq2, LANES)], LANES),
            "e1": _place([(e1, LANES)], LANES), "e2": _place([(e2, LANES)], LANES)}
    gkf = _place([(gk[None, :MLA_NOPE], MLA_NOPE), (jnp.ones((1, MLA_ROPE), f32), LANES - MLA_NOPE)], LANES)

    na_scale = NA_HEAD_DIM ** -0.5 * LOG2E
    lp = {
        "ln_g": p["ln_g"][None, :],
        "w_in": w_in_packed,
        "gq2": jnp.tile(p["na_q_norm"], 2)[None, :] * na_scale,
        "gk2": jnp.tile(p["na_k_norm"], 2)[None, :],
        "gcq": p["mla_cq_norm"][None, :],
        "gckv": p["mla_ckv_norm"][None, :],
        "wq": wq, "wk": wk, "wvt": wvt, "gkf": gkf,
        "w_o_na": p["w_o_na"].astype(bf16),
        "w_o_mla": p["w_o_mla"].astype(bf16),
        "w_out": p["w_out"].astype(bf16),
    }
    return lp, tabs


@jax.jit
def _forward(x, p):
    bsz, seq_len, _ = x.shape
    depth = p["w_in"].shape[0]
    cos, sin = _rope_tables(seq_len)
    onehot, col_mask, tbl_idx = _na_table_static(seq_len)
    x2d = x.reshape(bsz * seq_len, D_MODEL)
    lp, tabs = jax.vmap(lambda pl_: _layer_params(pl_, cos, sin))(p)
    tbl = jax.vmap(lambda rb: _na_table(rb, onehot, col_mask, tbl_idx))(p["na_rel_bias"])
    for layer in range(depth):
        qa, ka, va, nag, qf, kf, mg, gna, gmla, vbt = _proj_call(x2d, lp, tabs, layer, seq_len)

        def b3(a):
            return a.reshape(bsz, seq_len, a.shape[-1])

        a_na = _natt_call(b3(qa), b3(ka), b3(va), b3(nag), tbl, layer)
        a_mla = _mla_call(b3(qf), b3(kf), vbt, b3(mg))
        x2d = _out_call(x2d, a_na.reshape(-1, NA_WIDTH), a_mla.reshape(-1, MLA_WIDTH), gna, gmla, lp, layer)
    return x2d.reshape(bsz, seq_len, D_MODEL)


def kernel(x, ln_g, w_in, na_q_norm, na_k_norm, na_rel_bias, mla_cq_norm, mla_ckv_norm, w_uq, w_ukv,
           mla_q_norm, mla_k_norm, w_o_na, w_o_mla, w_out):
    params = dict(ln_g=ln_g, w_in=w_in, na_q_norm=na_q_norm, na_k_norm=na_k_norm,
                  na_rel_bias=na_rel_bias, mla_cq_norm=mla_cq_norm, mla_ckv_norm=mla_ckv_norm,
                  w_uq=w_uq, w_ukv=w_ukv, mla_q_norm=mla_q_norm, mla_k_norm=mla_k_norm,
                  w_o_na=w_o_na, w_o_mla=w_o_mla, w_out=w_out)
    return _forward(x, params)
```

```python
import numpy as np
import jax
import jax.numpy as jnp
from jax import lax
from jax.experimental import pallas as pl
from jax.experimental.pallas import tpu as pltpu

D_MODEL = 1024
GRID_W = 64
NA_HEADS = 8
NA_HEAD_DIM = 64
NA_WIDTH = NA_HEADS * NA_HEAD_DIM
NA_WIN_ROWS = 8
NA_WIN_COLS = 16
MLA_HEADS = 8
MLA_NOPE = 64
MLA_ROPE = 32
MLA_QK_DIM = MLA_NOPE + MLA_ROPE
MLA_V = 64
MLA_Q_LORA = 256
MLA_KV_LORA = 128
MLA_WIDTH = MLA_HEADS * MLA_V
ROPE_BASE = 10000.0
EPS = 1e-6

LANES = 128
HEAD_PAD = LANES
MLA_PAD_WIDTH = MLA_HEADS * HEAD_PAD
MLA_VT_ROWS = 128
VMEM_LIMIT_BYTES = 56 * 1024 * 1024

_OFF_NAQ = 0
_OFF_NAK = _OFF_NAQ + NA_WIDTH
_OFF_NAV = _OFF_NAK + NA_WIDTH
_OFF_NAG = _OFF_NAV + NA_WIDTH
_OFF_CQ = _OFF_NAG + NA_WIDTH
_OFF_CKV = _OFF_CQ + MLA_Q_LORA
_OFF_MG = _OFF_CKV + 2 * LANES
_OFF_GNA = _OFF_MG + MLA_WIDTH
_OFF_GMLA = _OFF_GNA + D_MODEL
_W_IN_COLS = _OFF_GMLA + D_MODEL

PROJ_TILE = 512
OUT_TILE = 1024
PROJ_BLK = 256
NA_QROWS = 4
NA_BAND = 12
NA_TQ = NA_QROWS * GRID_W
NA_TK = NA_BAND * GRID_W
NA_SUB = 2
MLA_TQ = 256
MLA_KC = 512
MLA_GROUP = 8
MASK_VALUE = -1e30
LOG2E = 1.4426950408889634

_NT_DIMS = (((1,), (1,)), ((), ()))


def _dot(a, b):
    return jnp.dot(a, b, preferred_element_type=jnp.float32)


def _dot_nt(a, b):
    return lax.dot_general(a, b, _NT_DIMS, preferred_element_type=jnp.float32)


def _lane_iota():
    return lax.broadcasted_iota(jnp.int32, (1, LANES), 1)


def _mla_ones_row(parity):
    return MLA_V if parity == 0 else 0


def _proj_kernel(x_ref, lng_ref, w_ref, gq_ref, gk_ref, gcq_ref, gckv_ref, wq_ref,
                 wk_ref, wvt_ref, gkf_ref, tq1_ref, tq2_ref, e1_ref, e2_ref,
                 qa_ref, ka_ref, va_ref, nag_ref, qf_ref, kf_ref, mg_ref,
                 gna_ref, gmla_ref, vbt_ref, hb_scr, cqn_scr, lhsk_scr, sspe_scr):
    bf16 = jnp.bfloat16
    f32 = jnp.float32
    half_rows = hb_scr.shape[0] // 2
    x = x_ref[...]
    hb_scr[...] = (x * lax.rsqrt(jnp.mean(x * x, axis=-1, keepdims=True) + EPS) * lng_ref[...]).astype(bf16)
    lo_half = _lane_iota() < NA_HEAD_DIM
    rope_lanes = _lane_iota() < MLA_ROPE
    row_iota = lax.broadcasted_iota(jnp.int32, (MLA_VT_ROWS, 1), 0)

    def tiles(y):
        return [y[:, i * LANES:(i + 1) * LANES] for i in range(y.shape[1] // LANES)]

    def in_dot(off):
        return lambda: _dot(hb_scr[...], w_ref[:, off:off + PROJ_BLK])

    def headnorm64(g_ref):
        def fn(y):
            out = []
            for pc in tiles(y):
                sq = pc * pc
                s_lo = jnp.sum(jnp.where(lo_half, sq, 0.0), axis=-1, keepdims=True)
                s_hi = jnp.sum(jnp.where(lo_half, 0.0, sq), axis=-1, keepdims=True)
                r = jnp.where(lo_half,
                              lax.rsqrt(s_lo * (1.0 / NA_HEAD_DIM) + EPS),
                              lax.rsqrt(s_hi * (1.0 / NA_HEAD_DIM) + EPS))
                out.append(pc * r * g_ref[...])
            return jnp.concatenate(out, axis=-1)
        return fn

    def silu(g):
        return g * jax.nn.sigmoid(g)

    def post_cq(y):
        cqn_scr[...] = (y * lax.rsqrt(jnp.mean(y * y, axis=-1, keepdims=True) + EPS) * gcq_ref[...]).astype(bf16)

    def post_ckv(y):
        ckv, e = tiles(y)
        lhsk_scr[:, :LANES] = (ckv * lax.rsqrt(jnp.mean(ckv * ckv, axis=-1, keepdims=True) + EPS)
                               * gckv_ref[...]).astype(bf16)
        ss_pe = jnp.sum(jnp.where(rope_lanes, e * e, 0.0), axis=-1, keepdims=True)
        sspe_scr[...] = jnp.broadcast_to(ss_pe, sspe_scr.shape)
        rot = e * e1_ref[...] + pltpu.roll(e, LANES - MLA_ROPE, axis=1) * e2_ref[...]
        rot_hi = rot.astype(bf16).astype(f32)
        rot_lo = (rot - rot_hi).astype(bf16).astype(f32)
        lhsk_scr[:, LANES:] = (rot_hi + pltpu.roll(rot_lo, MLA_ROPE, axis=1)).astype(bf16)

    def head_tasks(src_scr, w2_ref, out_ref, head_fn):
        tasks = []
        for r0 in (0, half_rows):
            rows = slice(r0, r0 + half_rows)
            for c in range(0, MLA_PAD_WIDTH, 2 * PROJ_BLK):
                def dot_fn(rows=rows, c=c):
                    return _dot(src_scr[rows, :], w2_ref[:, c:c + 2 * PROJ_BLK])

                def post(y, rows=rows, c=c):
                    for i, a in enumerate(tiles(y)):
                        out_ref[rows, c + i * HEAD_PAD:c + (i + 1) * HEAD_PAD] = head_fn(a, rows).astype(bf16)
                tasks.append((dot_fn, post))
        return tasks

    first_rope_half = _lane_iota() < MLA_NOPE + MLA_ROPE // 2

    def q_head(a, rows):
        b = jnp.where(first_rope_half, pltpu.roll(a, LANES - MLA_ROPE // 2, axis=1),
                      pltpu.roll(a, MLA_ROPE // 2, axis=1))
        r = lax.rsqrt(jnp.sum(a * a, axis=-1, keepdims=True) / MLA_QK_DIM + EPS)
        return (a * tq1_ref[rows, :] + b * tq2_ref[rows, :]) * r

    def k_head(a, rows):
        ss = jnp.sum(jnp.where(lo_half, a * a, 0.0), axis=-1, keepdims=True) + sspe_scr[rows, :]
        return a * gkf_ref[...] * lax.rsqrt(ss / MLA_QK_DIM + EPS)

    def v_dot(pair):
        rows = slice(pair * 2 * MLA_VT_ROWS, (pair + 1) * 2 * MLA_VT_ROWS)
        return lambda: _dot_nt(wvt_ref[rows, :], lhsk_scr[:, :LANES])

    def v_post(pair):
        def post(y):
            for i in range(2):
                row0 = (2 * pair + i) * MLA_VT_ROWS
                a = y[i * MLA_VT_ROWS:(i + 1) * MLA_VT_ROWS, :]
                vbt_ref[0, row0:row0 + MLA_VT_ROWS, :] = jnp.where(row_iota == _mla_ones_row(i), 1.0, a).astype(bf16)
        return post

    def wide_blocks(off, width, out_ref, fn):
        tasks = []
        for r0 in (0, half_rows):
            for c in range(0, width, 2 * PROJ_BLK):
                def dot_fn(r0=r0, c=c):
                    return _dot(hb_scr[r0:r0 + half_rows, :], w_ref[:, off + c:off + c + 2 * PROJ_BLK])

                def post(y, r0=r0, c=c):
                    out_ref[r0:r0 + half_rows, c:c + 2 * PROJ_BLK] = fn(y).astype(bf16)
                tasks.append((dot_fn, post))
        return tasks

    tasks = [(in_dot(_OFF_CQ), post_cq), (in_dot(_OFF_CKV), post_ckv)]
    tasks += wide_blocks(_OFF_NAQ, NA_WIDTH, qa_ref, headnorm64(gq_ref))
    tasks += wide_blocks(_OFF_NAK, NA_WIDTH, ka_ref, headnorm64(gk_ref))
    tasks += head_tasks(cqn_scr, wq_ref, qf_ref, q_head)
    tasks += wide_blocks(_OFF_GNA, D_MODEL, gna_ref, jax.nn.sigmoid)
    tasks += head_tasks(lhsk_scr, wk_ref, kf_ref, k_head)
    tasks += wide_blocks(_OFF_NAG, NA_WIDTH, nag_ref, silu)
    tasks += [(v_dot(p), v_post(p)) for p in range(MLA_PAD_WIDTH // PROJ_BLK)]
    tasks += wide_blocks(_OFF_MG, MLA_WIDTH, mg_ref, silu)
    tasks += wide_blocks(_OFF_GMLA, D_MODEL, gmla_ref, jax.nn.sigmoid)
    tasks += wide_blocks(_OFF_NAV, NA_WIDTH, va_ref, lambda y: y)

    _run_tasks([(dot_fn, post_fn, False) for dot_fn, post_fn in tasks])


def _const_spec(stacked, layer):
    rest = stacked.shape[1:]
    return pl.BlockSpec((None,) + rest, lambda i: (layer,) + (0,) * len(rest), pipeline_mode=pl.Buffered(1))


def _proj_call(x2d, lp, tabs, layer, seq_len):
    n_tok = x2d.shape[0]
    tm = PROJ_TILE
    assert n_tok % tm == 0 and seq_len % tm == 0
    tiles_per_seq = seq_len // tm
    bf16 = jnp.bfloat16

    def row_spec(width):
        return pl.BlockSpec((tm, width), lambda i: (i, 0))

    def tab_spec():
        return pl.BlockSpec((None, tm, LANES), lambda i: (layer, i % tiles_per_seq, 0))

    consts = [lp["ln_g"], lp["w_in"], lp["gq2"], lp["gk2"], lp["gcq"], lp["gckv"],
              lp["wq"], lp["wk"], lp["wvt"], lp["gkf"]]
    out_widths = [NA_WIDTH, NA_WIDTH, NA_WIDTH, NA_WIDTH, MLA_PAD_WIDTH, MLA_PAD_WIDTH,
                  MLA_WIDTH, D_MODEL, D_MODEL]
    t_widths = [MLA_HEADS * MLA_VT_ROWS]
    t_specs = [pl.BlockSpec((1, w, tm), lambda i: (i // tiles_per_seq, 0, i % tiles_per_seq)) for w in t_widths]
    t_shapes = [jax.ShapeDtypeStruct((n_tok // seq_len, w, seq_len), bf16) for w in t_widths]
    return pl.pallas_call(
        _proj_kernel,
        grid=(n_tok // tm,),
        in_specs=[row_spec(D_MODEL)] + [_const_spec(c, layer) for c in consts] + [tab_spec()] * 4,
        out_specs=[row_spec(w) for w in out_widths] + t_specs,
        out_shape=[jax.ShapeDtypeStruct((n_tok, w), bf16) for w in out_widths] + t_shapes,
        scratch_shapes=[pltpu.VMEM((tm, D_MODEL), bf16),
                        pltpu.VMEM((tm, MLA_Q_LORA), bf16),
                        pltpu.VMEM((tm, 2 * LANES), bf16),
                        pltpu.VMEM((tm, LANES), jnp.float32)],
        compiler_params=pltpu.CompilerParams(
            dimension_semantics=("arbitrary",), vmem_limit_bytes=VMEM_LIMIT_BYTES),
        name="proj",
    )(x2d, *consts, tabs["tq1"], tabs["tq2"], tabs["e1"], tabs["e2"])


def _na_band_start(i, rows):
    return jnp.clip(NA_QROWS * i - NA_WIN_ROWS // 2, 0, rows - NA_BAND)


def _natt_kernel(q_ref, k_ref, v_ref, gate_ref, tbl_ref, o_ref, s_scr, p_scr):
    bf16 = jnp.bfloat16
    i = pl.program_id(1)
    rows = k_ref.shape[1] // GRID_W
    n_blk = pl.num_programs(1) * NA_SUB
    subs = range(NA_SUB)
    blks = [NA_SUB * i + sb for sb in subs]
    starts = [pl.multiple_of(_na_band_start(blk, rows) * GRID_W, GRID_W) for blk in blks]
    pats = [jnp.minimum(blk, 1) + (blk == n_blk - 1).astype(jnp.int32) for blk in blks]
    lo_half = _lane_iota() < NA_HEAD_DIM

    def pcols(head):
        return slice((head // 2) * LANES, (head // 2 + 1) * LANES)

    def qrows(sb):
        return slice(sb * NA_TQ, (sb + 1) * NA_TQ)

    def fold(x, op):
        run = None
        for t in range(x.shape[1] // LANES):
            tile = x[:, t * LANES:(t + 1) * LANES]
            run = tile if run is None else op(run, tile)
        return run

    def scores(head, sb):
        qp = q_ref[0, qrows(sb), pcols(head)]
        sel = lo_half if head % 2 == 0 else jnp.logical_not(lo_half)
        qm = jnp.where(sel, qp, jnp.zeros_like(qp))
        keys = k_ref[0, pl.ds(starts[sb], NA_TK), pcols(head)]
        s = _dot_nt(qm, keys) + tbl_ref[pats[sb], head]
        s_scr[sb, head % 2] = s
        return jnp.max(fold(s, jnp.maximum), axis=-1, keepdims=True)

    def probs(head, sb, m):
        p = jnp.exp2(s_scr[sb, head % 2] - m)
        p_scr[sb, head % 2] = p.astype(bf16)
        return fold(p, jnp.add)

    def pv(head, sb):
        quad = slice((head // 4) * 2 * LANES, (head // 4 + 1) * 2 * LANES)
        o = _dot(p_scr[sb, head % 2], v_ref[0, pl.ds(starts[sb], NA_TK), quad])
        tile = (head // 2) % 2
        return o[:, tile * LANES:(tile + 1) * LANES]

    def finish(head, sb, acc, run_sum, prev):
        o = acc / jnp.sum(run_sum, axis=-1, keepdims=True)
        if head % 2 == 0:
            return o
        pair = jnp.where(lo_half, prev, o)
        gate = gate_ref[0, qrows(sb), pcols(head)].astype(jnp.float32)
        o_ref[0, qrows(sb), pcols(head)] = (pair * gate).astype(bf16)
        return None

    m = [scores(0, sb) for sb in subs]
    prev_sum, even_out = [None] * NA_SUB, [None] * NA_SUB
    for head in range(NA_HEADS):
        m_next = [scores(head + 1, sb) for sb in subs] if head + 1 < NA_HEADS else None
        acc = [pv(head - 1, sb) for sb in subs] if head >= 1 else None
        cur_sum = [probs(head, sb, m[sb]) for sb in subs]
        if head >= 1:
            even_out = [finish(head - 1, sb, acc[sb], prev_sum[sb], even_out[sb]) for sb in subs]
        prev_sum, m = cur_sum, m_next
    for sb in subs:
        finish(NA_HEADS - 1, sb, pv(NA_HEADS - 1, sb), prev_sum[sb], even_out[sb])


def _natt_call(qa, ka, va, nag, tbl, layer):
    bsz, seq_len, _ = qa.shape
    n_blk = seq_len // NA_TQ
    assert n_blk % NA_SUB == 0
    blk = pl.BlockSpec((1, NA_SUB * NA_TQ, NA_WIDTH), lambda b, i: (b, i, 0))
    full = pl.BlockSpec((1, seq_len, NA_WIDTH), lambda b, i: (b, 0, 0))
    return pl.pallas_call(
        _natt_kernel,
        grid=(bsz, n_blk // NA_SUB),
        in_specs=[blk, full, full, blk,
                  pl.BlockSpec((None,) + tbl.shape[1:], lambda b, i: (layer, 0, 0, 0, 0),
                               pipeline_mode=pl.Buffered(1))],
        out_specs=blk,
        out_shape=jax.ShapeDtypeStruct((bsz, seq_len, NA_WIDTH), jnp.bfloat16),
        scratch_shapes=[pltpu.VMEM((NA_SUB, 2, NA_TQ, NA_TK), jnp.float32),
                        pltpu.VMEM((NA_SUB, 2, NA_TQ, NA_TK), jnp.bfloat16)],
        compiler_params=pltpu.CompilerParams(
            dimension_semantics=("arbitrary", "arbitrary"), vmem_limit_bytes=VMEM_LIMIT_BYTES),
        name="natt",
    )(qa, ka, va, nag, tbl)


def _mla_kernel(q_ref, k_ref, vt_ref, gate_ref, o_ref, s_scr, p_scr):
    bf16 = jnp.bfloat16
    seq_len = k_ref.shape[1]
    n_chunks = seq_len // MLA_KC
    n_heads = q_ref.shape[2] // HEAD_PAD
    lo_half = _lane_iota() < MLA_V
    sublanes = 8

    def hcols(head):
        return slice(head * HEAD_PAD, (head + 1) * HEAD_PAD)

    def kslice(c):
        return slice(c * MLA_KC, (c + 1) * MLA_KC)

    def qk_chunk(head, c, run_max):
        s = _dot_nt(k_ref[0, kslice(c), hcols(head)], q_ref[0, :, hcols(head)])
        s_scr[head % 2, kslice(c), :] = s
        part = jnp.max(s.reshape(MLA_KC // sublanes, sublanes, MLA_TQ), axis=0)
        return part if run_max is None else jnp.maximum(run_max, part)

    def exp_chunk(head, c, m):
        p_scr[head % 2, kslice(c), :] = jnp.exp2(s_scr[head % 2, kslice(c), :] - m).astype(bf16)

    def pv_chunk(head, c, acc):
        vrows = slice(head * MLA_VT_ROWS, (head + 1) * MLA_VT_ROWS)
        o = _dot(vt_ref[0, vrows, kslice(c)], p_scr[head % 2, kslice(c), :])
        return o if acc is None else acc + o

    def finish(head, acc, prev):
        ones_row = _mla_ones_row(head % 2)
        o = (acc / acc[ones_row:ones_row + 1, :]).T
        if head % 2 == 0:
            return o
        cols = slice((head // 2) * LANES, (head // 2 + 1) * LANES)
        pair = jnp.where(lo_half, prev, o)
        o_ref[0, :, cols] = (pair * gate_ref[0, :, cols].astype(jnp.float32)).astype(bf16)
        return None

    run = None
    for c in range(n_chunks):
        run = qk_chunk(0, c, run)
    m = jnp.max(run, axis=0, keepdims=True)
    acc, even_out = None, None
    for head in range(n_heads):
        run, next_acc = None, None
        for c in range(n_chunks):
            if head + 1 < n_heads:
                run = qk_chunk(head + 1, c, run)
            if head >= 1:
                next_acc = pv_chunk(head - 1, c, next_acc)
            exp_chunk(head, c, m)
        if head >= 1:
            even_out = finish(head - 1, next_acc, even_out)
        if head + 1 < n_heads:
            m = jnp.max(run, axis=0, keepdims=True)
    for c in range(n_chunks):
        acc = pv_chunk(n_heads - 1, c, acc)
    finish(n_heads - 1, acc, even_out)


def _mla_call(qf, kf, vbt, mg):
    bsz, seq_len, _ = qf.shape
    n_groups = MLA_HEADS // MLA_GROUP
    assert seq_len % MLA_KC == 0 and seq_len % MLA_TQ == 0 and MLA_GROUP % 2 == 0
    return pl.pallas_call(
        _mla_kernel,
        grid=(bsz, n_groups, seq_len // MLA_TQ),
        in_specs=[pl.BlockSpec((1, MLA_TQ, MLA_GROUP * HEAD_PAD), lambda b, p, j: (b, j, p)),
                  pl.BlockSpec((1, seq_len, MLA_GROUP * HEAD_PAD), lambda b, p, j: (b, 0, p)),
                  pl.BlockSpec((1, MLA_GROUP * MLA_VT_ROWS, seq_len), lambda b, p, j: (b, p, 0)),
                  pl.BlockSpec((1, MLA_TQ, MLA_GROUP * MLA_V), lambda b, p, j: (b, j, p))],
        out_specs=pl.BlockSpec((1, MLA_TQ, MLA_GROUP * MLA_V), lambda b, p, j: (b, j, p)),
        out_shape=jax.ShapeDtypeStruct((bsz, seq_len, MLA_WIDTH), jnp.bfloat16),
        scratch_shapes=[pltpu.VMEM((2, seq_len, MLA_TQ), jnp.float32),
                        pltpu.VMEM((2, seq_len, MLA_TQ), jnp.bfloat16)],
        compiler_params=pltpu.CompilerParams(
            dimension_semantics=("arbitrary", "arbitrary", "arbitrary"),
            vmem_limit_bytes=VMEM_LIMIT_BYTES),
        name="mla",
    )(qf, kf, vbt, mg)


def _run_tasks(tasks):
    pending = None
    for dot_fn, post_fn, barrier in tasks:
        if barrier and pending is not None:
            pending[0](pending[1])
            pending = None
        y = dot_fn()
        if pending is not None:
            pending[0](pending[1])
        pending = (post_fn, y)
    pending[0](pending[1])


def _out_kernel(x_ref, ana_ref, amla_ref, gna_ref, gmla_ref, wna_ref, wmla_ref, wout_ref, o_ref, y_scr):
    f32 = jnp.float32
    row_tiles = [slice(r, r + PROJ_TILE) for r in range(0, o_ref.shape[0], PROJ_TILE)]
    col_blocks = [slice(c, c + PROJ_BLK) for c in range(0, o_ref.shape[1], PROJ_BLK)]

    def branch_dots(rows, cols):
        return lambda: (_dot(ana_ref[rows, :], wna_ref[:, cols]), _dot(amla_ref[rows, :], wmla_ref[:, cols]))

    def merge(rows, cols):
        def post(us):
            y = gna_ref[rows, cols].astype(f32) * us[0] + gmla_ref[rows, cols].astype(f32) * us[1]
            y_scr[rows, cols] = y.astype(jnp.bfloat16)
        return post

    def out_dot(rows, cols):
        return lambda: _dot(y_scr[rows, :], wout_ref[:, cols])

    def residual(rows, cols):
        def post(y):
            o_ref[rows, cols] = x_ref[rows, cols] + y
        return post

    tasks = [(branch_dots(r, c), merge(r, c), False) for r in row_tiles for c in col_blocks]
    tasks += [(out_dot(r, c), residual(r, c), i == 0)
              for i, (r, c) in enumerate((r, c) for r in row_tiles for c in col_blocks)]
    _run_tasks(tasks)


def _out_call(x2d, a_na, a_mla, gna, gmla, lp, layer):
    n_tok = x2d.shape[0]
    tm = OUT_TILE
    assert n_tok % tm == 0 and tm % PROJ_TILE == 0

    def row_spec(width):
        return pl.BlockSpec((tm, width), lambda i: (i, 0))

    consts = [lp["w_o_na"], lp["w_o_mla"], lp["w_out"]]
    return pl.pallas_call(
        _out_kernel,
        grid=(n_tok // tm,),
        in_specs=[row_spec(D_MODEL), row_spec(NA_WIDTH), row_spec(MLA_WIDTH), row_spec(D_MODEL),
                  row_spec(D_MODEL)] + [_const_spec(c, layer) for c in consts],
        out_specs=row_spec(D_MODEL),
        out_shape=jax.ShapeDtypeStruct((n_tok, D_MODEL), jnp.float32),
        scratch_shapes=[pltpu.VMEM((tm, D_MODEL), jnp.bfloat16)],
        compiler_params=pltpu.CompilerParams(
            dimension_semantics=("arbitrary",), vmem_limit_bytes=VMEM_LIMIT_BYTES),
        name="out",
    )(x2d, a_na, a_mla, gna, gmla, *consts)


_N_DROW = 2 * NA_WIN_ROWS - 1
_N_DCOL = 2 * NA_WIN_COLS - 1


def _na_table_static(seq_len):
    rows = seq_len // GRID_W
    n_blk = rows // NA_QROWS
    qc = np.arange(GRID_W)
    kj = np.arange(GRID_W)
    cs = np.clip(qc - NA_WIN_COLS // 2, 0, GRID_W - NA_WIN_COLS)
    col_valid = (kj[None, :] >= cs[:, None]) & (kj[None, :] < cs[:, None] + NA_WIN_COLS)
    dcol = kj[None, :] - qc[:, None] + (NA_WIN_COLS - 1)
    onehot = (dcol[None] == np.arange(_N_DCOL)[:, None, None]) & col_valid[None]
    col_mask = np.where(col_valid, 0.0, MASK_VALUE).astype(np.float32)
    qr = np.arange(NA_QROWS)
    kn = np.arange(NA_BAND)
    idx = []
    for i in (0, 1, n_blk - 1):
        band = int(np.clip(NA_QROWS * i - NA_WIN_ROWS // 2, 0, rows - NA_BAND))
        r = NA_QROWS * i + qr
        rs = np.clip(r - NA_WIN_ROWS // 2, 0, rows - NA_WIN_ROWS)
        krow = band + kn
        valid = (krow[None, :] >= rs[:, None]) & (krow[None, :] < rs[:, None] + NA_WIN_ROWS)
        drow = krow[None, :] - r[:, None] + (NA_WIN_ROWS - 1)
        idx.append(np.where(valid, drow, _N_DROW))
    return onehot.astype(np.float32), col_mask, np.stack(idx).astype(np.int32)


def _na_table(rel_bias, onehot, col_mask, idx):
    blocks = jnp.einsum("hdc,cqj->hdqj", rel_bias, onehot, precision=lax.Precision.HIGHEST) + col_mask
    blocks = jnp.concatenate(
        [blocks, jnp.full((NA_HEADS, 1, GRID_W, GRID_W), MASK_VALUE, jnp.float32)], axis=1) * LOG2E
    tbl = jnp.take(blocks, idx.reshape(-1), axis=1)
    tbl = tbl.reshape(NA_HEADS, 3, NA_QROWS, NA_BAND, GRID_W, GRID_W)
    return tbl.transpose(1, 0, 2, 4, 3, 5).reshape(3, NA_HEADS, NA_TQ, NA_TK)


def _rope_tables(seq_len):
    t = jnp.arange(seq_len)
    row = (t // GRID_W).astype(jnp.float32)
    col = (t % GRID_W).astype(jnp.float32)
    half = MLA_ROPE // 2
    n_freq = half // 2
    inv = jnp.power(jnp.float32(ROPE_BASE), -jnp.arange(n_freq, dtype=jnp.float32) / n_freq)
    ang = jnp.concatenate([row[:, None] * inv, col[:, None] * inv], axis=-1)
    return jnp.cos(ang), jnp.sin(ang)


def _place(cols_to_blocks, total):
    parts = []
    for arr, width in cols_to_blocks:
        pad = width - arr.shape[-1]
        parts.append(jnp.pad(arr, [(0, 0)] * (arr.ndim - 1) + [(0, pad)]) if pad else arr)
    out = jnp.concatenate(parts, axis=-1)
    assert out.shape[-1] == total
    return out


def _layer_params(p, cos, sin):
    f32, bf16 = jnp.float32, jnp.bfloat16
    half = MLA_ROPE // 2
    w_in = p["w_in"]
    o = 0
    pieces = {}
    for name, size in (("naq", NA_WIDTH), ("nak", NA_WIDTH), ("nav", NA_WIDTH), ("nag", NA_WIDTH),
                       ("cq", MLA_Q_LORA), ("ckv", MLA_KV_LORA), ("kpe", MLA_ROPE),
                       ("mg", MLA_WIDTH), ("gna", D_MODEL), ("gmla", D_MODEL)):
        pieces[name] = w_in[:, o:o + size]
        o += size
    kpe = pieces["kpe"]
    kpe_sw = jnp.concatenate([kpe[:, half:], kpe[:, :half]], axis=-1)
    ckvx = _place([(pieces["ckv"], LANES), (jnp.concatenate([kpe, kpe_sw], axis=-1), LANES)], 2 * LANES)
    w_in_packed = jnp.concatenate(
        [pieces["naq"], pieces["nak"], pieces["nav"], pieces["nag"], pieces["cq"], ckvx,
         pieces["mg"], pieces["gna"], pieces["gmla"]], axis=-1).astype(bf16)
    assert w_in_packed.shape[-1] == _W_IN_COLS

    w_uq = p["w_uq"].reshape(MLA_Q_LORA, MLA_HEADS, MLA_QK_DIM)
    wq = _place([(w_uq, HEAD_PAD)], HEAD_PAD).reshape(MLA_Q_LORA, MLA_PAD_WIDTH).astype(bf16)

    w_ukv = p["w_ukv"].reshape(MLA_KV_LORA, MLA_HEADS, MLA_NOPE + MLA_V)
    k_nope = _place([(w_ukv[..., :MLA_NOPE], HEAD_PAD)], HEAD_PAD).reshape(MLA_KV_LORA, MLA_PAD_WIDTH)
    eye = jnp.eye(MLA_ROPE, dtype=f32)
    put = jnp.pad(eye, ((0, 0), (MLA_NOPE, HEAD_PAD - MLA_QK_DIM)))
    put = jnp.tile(put, (1, MLA_HEADS))
    wk = jnp.concatenate([k_nope, put, put, jnp.zeros((2 * LANES - MLA_KV_LORA - 2 * MLA_ROPE,
                                                        MLA_PAD_WIDTH), f32)], axis=0).astype(bf16)
    w_uv = w_ukv[..., MLA_NOPE:].reshape(MLA_KV_LORA, MLA_HEADS // 2, 2, MLA_V)
    zv = jnp.zeros_like(w_uv[:, :, 0])
    wvt = jnp.stack([jnp.concatenate([w_uv[:, :, 0], zv], -1), jnp.concatenate([zv, w_uv[:, :, 1]], -1)], axis=2)
    wvt = wvt.reshape(MLA_KV_LORA, MLA_HEADS * MLA_VT_ROWS).T.astype(bf16)

    gq = p["mla_q_norm"]
    gk = p["mla_k_norm"]
    scale = MLA_QK_DIM ** -0.5 * LOG2E
    seq_len = cos.shape[0]
    ones = jnp.ones((seq_len, 1), f32)
    tq1 = _place([(ones * gq[None, :MLA_NOPE], MLA_NOPE),
                  (cos * gq[None, MLA_NOPE:MLA_NOPE + half], half),
                  (cos * gq[None, MLA_NOPE + half:], half)], MLA_QK_DIM) * scale
    tq2 = _place([(jnp.zeros((seq_len, MLA_NOPE), f32), MLA_NOPE),
                  (-sin * gq[None, MLA_NOPE + half:], half),
                  (sin * gq[None, MLA_NOPE:MLA_NOPE + half], half)], MLA_QK_DIM) * scale
    e1 = jnp.concatenate([cos * gk[None, MLA_NOPE:MLA_NOPE + half], cos * gk[None, MLA_NOPE + half:]], -1)
    e2 = jnp.concatenate([-sin * gk[None, MLA_NOPE + half:], sin * gk[None, MLA_NOPE:MLA_NOPE + half]], -1)
    tabs = {"tq1": _place([(tq1, LANES)], LANES), "tq2": _place([(tq2, LANES)], LANES),
            "e1": _place([(e1, LANES)], LANES), "e2": _place([(e2, LANES)], LANES)}
    gkf = _place([(gk[None, :MLA_NOPE], MLA_NOPE), (jnp.ones((1, MLA_ROPE), f32), LANES - MLA_NOPE)], LANES)

    na_scale = NA_HEAD_DIM ** -0.5 * LOG2E
    lp = {
        "ln_g": p["ln_g"][None, :],
        "w_in": w_in_packed,
        "gq2": jnp.tile(p["na_q_norm"], 2)[None, :] * na_scale,
        "gk2": jnp.tile(p["na_k_norm"], 2)[None, :],
        "gcq": p["mla_cq_norm"][None, :],
        "gckv": p["mla_ckv_norm"][None, :],
        "wq": wq, "wk": wk, "wvt": wvt, "gkf": gkf,
        "w_o_na": p["w_o_na"].astype(bf16),
        "w_o_mla": p["w_o_mla"].astype(bf16),
        "w_out": p["w_out"].astype(bf16),
    }
    return lp, tabs


@jax.jit
def _forward(x, p):
    bsz, seq_len, _ = x.shape
    depth = p["w_in"].shape[0]
    cos, sin = _rope_tables(seq_len)
    onehot, col_mask, tbl_idx = _na_table_static(seq_len)
    x2d = x.reshape(bsz * seq_len, D_MODEL)
    lp, tabs = jax.vmap(lambda pl_: _layer_params(pl_, cos, sin))(p)
    tbl = jax.vmap(lambda rb: _na_table(rb, onehot, col_mask, tbl_idx))(p["na_rel_bias"])
    for layer in range(depth):
        qa, ka, va, nag, qf, kf, mg, gna, gmla, vbt = _proj_call(x2d, lp, tabs, layer, seq_len)

        def b3(a):
            return a.reshape(bsz, seq_len, a.shape[-1])

        a_na = _natt_call(b3(qa), b3(ka), b3(va), b3(nag), tbl, layer)
        a_mla = _mla_call(b3(qf), b3(kf), vbt, b3(mg))
        x2d = _out_call(x2d, a_na.reshape(-1, NA_WIDTH), a_mla.reshape(-1, MLA_WIDTH), gna, gmla, lp, layer)
    return x2d.reshape(bsz, seq_len, D_MODEL)


def kernel(x, ln_g, w_in, na_q_norm, na_k_norm, na_rel_bias, mla_cq_norm, mla_ckv_norm, w_uq, w_ukv,
           mla_q_norm, mla_k_norm, w_o_na, w_o_mla, w_out):
    params = dict(ln_g=ln_g, w_in=w_in, na_q_norm=na_q_norm, na_k_norm=na_k_norm,
                  na_rel_bias=na_rel_bias, mla_cq_norm=mla_cq_norm, mla_ckv_norm=mla_ckv_norm,
                  w_uq=w_uq, w_ukv=w_ukv, mla_q_norm=mla_q_norm, mla_k_norm=mla_k_norm,
                  w_o_na=w_o_na, w_o_mla=w_o_mla, w_out=w_out)
    return _forward(x, params)
```
